```python
import jax, jax.numpy as jnp
from jax import lax
import numpy as np

D_MODEL = 4096
BATCH = 2
SEQ = 8192
DEPTH = 4

CHUNK = 64
Q_BLOCK = 128
N_MIXERS = 3
ROPE_THETA = 10000.0
NORM_EPS = 1e-6
D_FF = 6144

A_HEADS = 64
A_Q_LORA = 1024
A_KV_LORA = 512
A_NOPE = 128
A_ROPE = 64
A_V = 128
A_IN = A_Q_LORA + A_KV_LORA + A_ROPE

B_HEADS = 32
B_KV_HEADS = 8
B_HEAD_DIM = 128
B_IDX_HEADS = 32
B_IDX_DIM = 128
B_TOPK_MAX = 256
B_SPLITS = (B_HEADS * B_HEAD_DIM, B_KV_HEADS * B_HEAD_DIM, B_KV_HEADS * B_HEAD_DIM,
            B_IDX_HEADS * B_IDX_DIM, B_IDX_DIM, B_IDX_HEADS)
B_IN = sum(B_SPLITS)

C_HEADS = 8
C_V_DIM = D_MODEL // C_HEADS
C_QK_DIM = C_V_DIM // 2
C_FGATE_BIAS = 3.0
C_SPLITS = (C_HEADS * C_QK_DIM, C_HEADS * C_QK_DIM, C_HEADS * C_V_DIM,
            C_HEADS * C_V_DIM, C_HEADS, C_HEADS)
C_IN = sum(C_SPLITS)

kernel_name = 'hybrid_chunk_causal_mla_dsa_mlstm_macaron'


def split_cols(t, sizes):
    offs = np.cumsum(sizes)[:-1]
    return jnp.split(t, [int(o) for o in offs], axis=-1)


def rmsnorm(x, g):
    xf = x.astype(jnp.float32)
    y = xf * lax.rsqrt(jnp.mean(xf * xf, axis=-1, keepdims=True) + NORM_EPS)
    return (y * g.astype(jnp.float32)).astype(x.dtype)


def rope(x, positions):
    d = x.shape[-1]
    inv = jnp.power(ROPE_THETA, -jnp.arange(0, d, 2, dtype=jnp.float32) / d)
    ang = positions.astype(jnp.float32)[..., None] * inv
    ang = ang.reshape(ang.shape[:2] + (1,) * (x.ndim - 3) + (d // 2,))
    cos, sin = jnp.cos(ang), jnp.sin(ang)
    xf = x.astype(jnp.float32)
    x1, x2 = xf[..., : d // 2], xf[..., d // 2:]
    return jnp.concatenate([x1 * cos - x2 * sin, x1 * sin + x2 * cos], axis=-1).astype(x.dtype)


def chunk_mask(q0, nq, nk):
    qc = (q0 + jnp.arange(nq)) // CHUNK
    kc = jnp.arange(nk) // CHUNK
    return qc[:, None] >= kc[None, :]


def swiglu(h, w_in, w_out):
    g, u = jnp.split(h @ w_in, 2, axis=-1)
    return (jax.nn.silu(g) * u) @ w_out


def mla_mixer(h, positions, w_in, q_norm, kv_norm, w_uq, w_ukv, w_out):
    B, S, _ = h.shape
    c_q, c_kv, k_r = split_cols(h @ w_in, (A_Q_LORA, A_KV_LORA, A_ROPE))
    q = (rmsnorm(c_q, q_norm) @ w_uq).reshape(B, S, A_HEADS, A_NOPE + A_ROPE)
    q_n, q_r = q[..., :A_NOPE], rope(q[..., A_NOPE:], positions)
    kv = (rmsnorm(c_kv, kv_norm) @ w_ukv).reshape(B, S, A_HEADS, A_NOPE + A_V)
    k_n, v = kv[..., :A_NOPE], kv[..., A_NOPE:]
    k_r = rope(k_r, positions)
    scale = (A_NOPE + A_ROPE) ** -0.5
    outs = []
    for q0 in range(0, S, Q_BLOCK):
        kend = q0 + Q_BLOCK
        s = (jnp.einsum('bqhd,bkhd->bhqk', q_n[:, q0:kend], k_n[:, :kend])
             + jnp.einsum('bqhd,bkd->bhqk', q_r[:, q0:kend], k_r[:, :kend]))
        s = jnp.where(chunk_mask(q0, Q_BLOCK, kend), s.astype(jnp.float32) * scale, -jnp.inf)
        p = jax.nn.softmax(s, axis=-1).astype(v.dtype)
        outs.append(jnp.einsum('bhqk,bkhd->bqhd', p, v[:, :kend]))
    o = jnp.concatenate(outs, axis=1).reshape(B, S, A_HEADS * A_V)
    return o @ w_out


def dsa_mixer(h, positions, w_in, idx_k_norm, w_out):
    B, S, _ = h.shape
    q, k, v, qi, ki, wi = split_cols(h @ w_in, B_SPLITS)
    G = B_HEADS // B_KV_HEADS
    q = rope(q.reshape(B, S, B_HEADS, B_HEAD_DIM), positions).reshape(B, S, B_KV_HEADS, G, B_HEAD_DIM)
    k = rope(k.reshape(B, S, B_KV_HEADS, B_HEAD_DIM), positions)
    v = v.reshape(B, S, B_KV_HEADS, B_HEAD_DIM)
    qi = rope(qi.reshape(B, S, B_IDX_HEADS, B_IDX_DIM), positions)
    ki = rope(rmsnorm(ki, idx_k_norm), positions)
    wi = wi.astype(jnp.float32) * (B_IDX_HEADS ** -0.5 * B_IDX_DIM ** -0.5)
    topk = min(B_TOPK_MAX, S // 4)
    gather = jax.vmap(lambda t, i: t[i])
    outs = []
    for q0 in range(0, S, Q_BLOCK):
        kend = q0 + Q_BLOCK
        ksel = min(topk, kend)
        dots = jnp.einsum('bqhd,bkd->bqhk', qi[:, q0:kend], ki[:, :kend]).astype(jnp.float32)
        score = jnp.einsum('bqhk,bqh->bqk', jax.nn.relu(dots), wi[:, q0:kend])
        score = jnp.where(chunk_mask(q0, Q_BLOCK, kend)[None], score, -jnp.inf)
        val, idx = lax.top_k(score, ksel)
        valid = val > -jnp.inf
        kg = gather(k, idx)
        vg = gather(v, idx)
        s = jnp.einsum('bqhgd,bqkhd->bqhgk', q[:, q0:kend], kg).astype(jnp.float32) * B_HEAD_DIM ** -0.5
        s = jnp.where(valid[:, :, None, None, :], s, -jnp.inf)
        p = jax.nn.softmax(s, axis=-1).astype(vg.dtype)
        outs.append(jnp.einsum('bqhgk,bqkhd->bqhgd', p, vg))
    o = jnp.concatenate(outs, axis=1).reshape(B, S, B_HEADS * B_HEAD_DIM)
    return o @ w_out


def mlstm_mixer(h, w_in, gate_bias, head_norm, w_out):
    B, S, _ = h.shape
    NC = S // CHUNK
    q, k, v, o, gi, gf = split_cols(h @ w_in, C_SPLITS)

    def to_chunks(t, d):
        return t.reshape(B, NC, CHUNK, C_HEADS, d).transpose(1, 0, 3, 2, 4).astype(jnp.float32)

    def gate_chunks(t):
        return t.reshape(B, NC, CHUNK, C_HEADS).transpose(1, 0, 3, 2)

    q = to_chunks(q, C_QK_DIM) * C_QK_DIM ** -0.5
    k = to_chunks(k, C_QK_DIM)
    v = to_chunks(v, C_V_DIM)
    gb = gate_bias.astype(jnp.float32)
    li = gate_chunks(gi.astype(jnp.float32) + gb[:C_HEADS])
    lf = gate_chunks(jax.nn.log_sigmoid(gf.astype(jnp.float32) + gb[C_HEADS:]))
    tril = jnp.tril(jnp.ones((CHUNK, CHUNK), dtype=bool))

    def step(carry, inp):
        C, n, m = carry
        qc, kc, vc, lic, lfc = inp
        b = jnp.cumsum(lfc, axis=-1)
        a = b + m[..., None]
        d = jnp.where(tril, b[..., :, None] - b[..., None, :] + lic[..., None, :], -jnp.inf)
        m_t = jnp.maximum(a, jnp.max(d, axis=-1))
        inter = jnp.exp(a - m_t)
        qk = jnp.einsum('bhtd,bhsd->bhts', qc, kc) * jnp.exp(d - m_t[..., None])
        num = inter[..., None] * jnp.einsum('bhvd,bhtd->bhtv', C, qc) + jnp.einsum('bhts,bhsv->bhtv', qk, vc)
        den = inter * jnp.einsum('bhd,bhtd->bht', n, qc) + jnp.sum(qk, axis=-1)
        hc = num / jnp.maximum(jnp.abs(den), jnp.exp(-m_t))[..., None]
        b_end = b[..., -1]
        g = b_end[..., None] - b + lic
        m_new = jnp.maximum(b_end + m, jnp.max(g, axis=-1))
        decay = jnp.exp(b_end + m - m_new)
        wk = jnp.exp(g - m_new[..., None])
        C = decay[..., None, None] * C + jnp.einsum('bhsv,bhsd->bhvd', vc * wk[..., None], kc)
        n = decay[..., None] * n + jnp.einsum('bhs,bhsd->bhd', wk, kc)
        return (C, n, m_new), hc

    init = (jnp.zeros((B, C_HEADS, C_V_DIM, C_QK_DIM), jnp.float32),
            jnp.zeros((B, C_HEADS, C_QK_DIM), jnp.float32),
            jnp.zeros((B, C_HEADS), jnp.float32))
    _, hs = lax.scan(step, init, (q, k, v, li, lf))
    hs = hs.transpose(1, 0, 3, 2, 4).reshape(B, S, C_HEADS, C_V_DIM)
    hs = hs * lax.rsqrt(jnp.mean(hs * hs, axis=-1, keepdims=True) + NORM_EPS)
    hs = hs.reshape(B, S, C_HEADS * C_V_DIM) * head_norm.astype(jnp.float32)
    y = (hs * jax.nn.sigmoid(o.astype(jnp.float32))).astype(h.dtype)
    return y @ w_out


def setup_inputs(seed: int = 0) -> dict:
    key = jax.random.key(seed)
    ks = iter(jax.random.split(key, 32))
    n_a = len(range(0, DEPTH, N_MIXERS))
    n_b = len(range(1, DEPTH, N_MIXERS))
    n_c = len(range(2, DEPTH, N_MIXERS))

    def w(shape):
        return jax.random.normal(next(ks), shape, jnp.float32) * shape[-2] ** -0.5

    def gain(shape):
        return 1.0 + 0.05 * jax.random.normal(next(ks), shape, jnp.float32)

    x = jax.random.normal(next(ks), (BATCH, SEQ, D_MODEL), jnp.float32)
    offset = jax.random.randint(next(ks), (BATCH, 1), 0, 4096, dtype=jnp.int32)
    positions = offset + jnp.arange(SEQ, dtype=jnp.int32)[None, :]
    i_bias = 0.1 * jax.random.normal(next(ks), (n_c, C_HEADS), jnp.float32)
    f_bias = C_FGATE_BIAS + 0.5 * jax.random.normal(next(ks), (n_c, C_HEADS), jnp.float32)
    return {
        'x': x,
        'positions': positions,
        'ffn1_norm': gain((DEPTH, D_MODEL)),
        'ffn1_w_in': w((DEPTH, D_MODEL, 2 * D_FF)),
        'ffn1_w_out': w((DEPTH, D_FF, D_MODEL)),
        'mix_norm': gain((DEPTH, D_MODEL)),
        'ffn2_norm': gain((DEPTH, D_MODEL)),
        'ffn2_w_in': w((DEPTH, D_MODEL, 2 * D_FF)),
        'ffn2_w_out': w((DEPTH, D_FF, D_MODEL)),
        'a_w_in': w((n_a, D_MODEL, A_IN)),
        'a_q_norm': gain((n_a, A_Q_LORA)),
        'a_kv_norm': gain((n_a, A_KV_LORA)),
        'a_w_uq': w((n_a, A_Q_LORA, A_HEADS * (A_NOPE + A_ROPE))),
        'a_w_ukv': w((n_a, A_KV_LORA, A_HEADS * (A_NOPE + A_V))),
        'a_w_out': w((n_a, A_HEADS * A_V, D_MODEL)),
        'b_w_in': w((n_b, D_MODEL, B_IN)),
        'b_idx_k_norm': gain((n_b, B_IDX_DIM)),
        'b_w_out': w((n_b, B_HEADS * B_HEAD_DIM, D_MODEL)),
        'c_w_in': w((n_c, D_MODEL, C_IN)),
        'c_gate_bias': jnp.concatenate([i_bias, f_bias], axis=-1),
        'c_head_norm': gain((n_c, C_HEADS * C_V_DIM)),
        'c_w_out': w((n_c, C_HEADS * C_V_DIM, D_MODEL)),
        'final_norm': gain((D_MODEL,)),
    }


def reference(x, positions, ffn1_norm, ffn1_w_in, ffn1_w_out, mix_norm, ffn2_norm, ffn2_w_in, ffn2_w_out,
              a_w_in, a_q_norm, a_kv_norm, a_w_uq, a_w_ukv, a_w_out,
              b_w_in, b_idx_k_norm, b_w_out,
              c_w_in, c_gate_bias, c_head_norm, c_w_out, final_norm):
    for i in range(DEPTH):
        x = x + 0.5 * swiglu(rmsnorm(x, ffn1_norm[i]), ffn1_w_in[i], ffn1_w_out[i])
        h = rmsnorm(x, mix_norm[i])
        kind, j = i % N_MIXERS, i // N_MIXERS
        if kind == 0:
            y = mla_mixer(h, positions, a_w_in[j], a_q_norm[j], a_kv_norm[j], a_w_uq[j], a_w_ukv[j], a_w_out[j])
        elif kind == 1:
            y = dsa_mixer(h, positions, b_w_in[j], b_idx_k_norm[j], b_w_out[j])
        else:
            y = mlstm_mixer(h, c_w_in[j], c_gate_bias[j], c_head_norm[j], c_w_out[j])
        x = x + y
        x = x + 0.5 * swiglu(rmsnorm(x, ffn2_norm[i]), ffn2_w_in[i], ffn2_w_out[i])
    return rmsnorm(x, final_norm)
```

```python
import functools

import jax
import jax.numpy as jnp
import numpy as np
from jax import lax
from jax.experimental import pallas as pl
from jax.experimental.pallas import tpu as pltpu

CHUNK = 64
ROPE_THETA = 10000.0
NORM_EPS = 1e-6
D_FF = 6144

A_HEADS = 64
A_Q_LORA = 1024
A_KV_LORA = 512
A_NOPE = 128
A_ROPE = 64
A_V = 128

B_HEADS = 32
B_KV_HEADS = 8
B_HEAD_DIM = 128
B_IDX_HEADS = 32
B_IDX_DIM = 128
B_TOPK_MAX = 256

C_HEADS = 8
N_MIXERS = 3

LANES = 128
MXU_DTYPE = jnp.bfloat16
VMEM_LIMIT_MB = 56

ROW_TILE = 256
MM_BM = 512
MM_BN = 1024
MLA_TQ = 512
DSA_TQ = 256
DSA_TK = 256

_F32 = jnp.float32
_I32 = jnp.int32
_NEG_BIG = -1e30
_INT_MIN = -(2 ** 31)
_KEY_OF_NEG_INF = -2139095041


def _params(semantics):
    return pltpu.CompilerParams(dimension_semantics=semantics,
                                vmem_limit_bytes=VMEM_LIMIT_MB << 20)


def _rot_half(x):
    return pltpu.roll(x, LANES // 2, 1)


def _rmsnorm_kernel(x_ref, g_ref, o_ref):
    x = x_ref[...].astype(_F32)
    y = x * lax.rsqrt(jnp.mean(x * x, axis=-1, keepdims=True) + NORM_EPS)
    o_ref[...] = (y * g_ref[...]).astype(o_ref.dtype)


def _rmsnorm(x, g, out_dtype, col_block=0, width=None):
    t = x.shape[0]
    width = x.shape[1] if width is None else width
    bt = min(ROW_TILE, t)
    return pl.pallas_call(
        _rmsnorm_kernel,
        grid=(t // bt,),
        in_specs=[pl.BlockSpec((bt, width), lambda i: (i, col_block)),
                  pl.BlockSpec((1, width), lambda i: (0, 0))],
        out_specs=pl.BlockSpec((bt, width), lambda i: (i, 0)),
        out_shape=jax.ShapeDtypeStruct((t, width), out_dtype),
        compiler_params=_params(("parallel",)),
        name="rmsnorm",
    )(x, g.reshape(1, width).astype(_F32))


def _rope_kernel(x_ref, ra_ref, rb_ref, o_ref, *, n_blocks):
    ra = ra_ref[...]
    rb = rb_ref[...]
    for c in range(n_blocks):
        sl = slice(c * LANES, (c + 1) * LANES)
        x = x_ref[:, sl].astype(_F32)
        o_ref[:, sl] = (x * ra + _rot_half(x) * rb).astype(o_ref.dtype)


def _rope(x, ra, rb, out_dtype, col_block=0, width=None):
    t = x.shape[0]
    width = x.shape[1] if width is None else width
    bt = min(ROW_TILE, t)
    return pl.pallas_call(
        functools.partial(_rope_kernel, n_blocks=width // LANES),
        grid=(t // bt,),
        in_specs=[pl.BlockSpec((bt, width), lambda i: (i, col_block)),
                  pl.BlockSpec((bt, LANES), lambda i: (i, 0)),
                  pl.BlockSpec((bt, LANES), lambda i: (i, 0))],
        out_specs=pl.BlockSpec((bt, width), lambda i: (i, 0)),
        out_shape=jax.ShapeDtypeStruct((t, width), out_dtype),
        compiler_params=_params(("parallel",)),
        name="rope",
    )(x, ra, rb)


def _rmsnorm_rope_kernel(x_ref, g_ref, ra_ref, rb_ref, o_ref):
    x = x_ref[...].astype(_F32)
    y = x * lax.rsqrt(jnp.mean(x * x, axis=-1, keepdims=True) + NORM_EPS) * g_ref[...]
    o_ref[...] = (y * ra_ref[...] + _rot_half(y) * rb_ref[...]).astype(o_ref.dtype)


def _rmsnorm_rope(x, g, ra, rb, out_dtype, col_block=0):
    t = x.shape[0]
    bt = min(ROW_TILE, t)
    return pl.pallas_call(
        _rmsnorm_rope_kernel,
        grid=(t // bt,),
        in_specs=[pl.BlockSpec((bt, LANES), lambda i: (i, col_block)),
                  pl.BlockSpec((1, LANES), lambda i: (0, 0)),
                  pl.BlockSpec((bt, LANES), lambda i: (i, 0)),
                  pl.BlockSpec((bt, LANES), lambda i: (i, 0))],
        out_specs=pl.BlockSpec((bt, LANES), lambda i: (i, 0)),
        out_shape=jax.ShapeDtypeStruct((t, LANES), out_dtype),
        compiler_params=_params(("parallel",)),
        name="rmsnorm_rope",
    )(x, g.reshape(1, LANES).astype(_F32), ra, rb)


def _mm_kernel(x_ref, w_ref, o_ref):
    o_ref[...] = jnp.dot(x_ref[...], w_ref[...],
                         preferred_element_type=_F32).astype(o_ref.dtype)


def _mm_swiglu_kernel(x_ref, wg_ref, wu_ref, o_ref):
    x = x_ref[...]
    g = jnp.dot(x, wg_ref[...], preferred_element_type=_F32)
    u = jnp.dot(x, wu_ref[...], preferred_element_type=_F32)
    o_ref[...] = (g * jax.nn.sigmoid(g) * u).astype(o_ref.dtype)


def _mm_residual_kernel(x_ref, w_ref, r_ref, o_ref, *, scale):
    y = jnp.dot(x_ref[...], w_ref[...], preferred_element_type=_F32)
    o_ref[...] = r_ref[...] + scale * y


def _tile(n, pref):
    return pref if n % pref == 0 else n


def _matmul(x, w, out_dtype, bm=None, bn=None):
    m, k = x.shape
    n = w.shape[1]
    bm = _tile(m, bm or MM_BM)
    bn = _tile(n, bn or MM_BN)
    return pl.pallas_call(
        _mm_kernel,
        grid=(m // bm, n // bn),
        in_specs=[pl.BlockSpec((bm, k), lambda i, j: (i, 0)),
                  pl.BlockSpec((k, bn), lambda i, j: (0, j))],
        out_specs=pl.BlockSpec((bm, bn), lambda i, j: (i, j)),
        out_shape=jax.ShapeDtypeStruct((m, n), out_dtype),
        compiler_params=_params(("parallel", "arbitrary")),
        name="matmul",
    )(x, w)


def _matmul_swiglu(x, w_in, bm=None, bn=512):
    m, k = x.shape
    f = w_in.shape[1] // 2
    bm = _tile(m, bm or MM_BM)
    bn = _tile(f, bn)
    nb = f // bn
    return pl.pallas_call(
        _mm_swiglu_kernel,
        grid=(m // bm, nb),
        in_specs=[pl.BlockSpec((bm, k), lambda i, j: (i, 0)),
                  pl.BlockSpec((k, bn), lambda i, j: (0, j)),
                  pl.BlockSpec((k, bn), lambda i, j: (0, j + nb))],
        out_specs=pl.BlockSpec((bm, bn), lambda i, j: (i, j)),
        out_shape=jax.ShapeDtypeStruct((m, f), MXU_DTYPE),
        compiler_params=_params(("parallel", "arbitrary")),
        name="matmul_swiglu",
    )(x, w_in, w_in)


def _matmul_residual(x, w, res, scale, bm=None, bn=512):
    m, k = x.shape
    n = w.shape[1]
    bm = _tile(m, bm or MM_BM)
    bn = _tile(n, bn)
    return pl.pallas_call(
        functools.partial(_mm_residual_kernel, scale=scale),
        grid=(m // bm, n // bn),
        in_specs=[pl.BlockSpec((bm, k), lambda i, j: (i, 0)),
                  pl.BlockSpec((k, bn), lambda i, j: (0, j)),
                  pl.BlockSpec((bm, bn), lambda i, j: (i, j))],
        out_specs=pl.BlockSpec((bm, bn), lambda i, j: (i, j)),
        out_shape=jax.ShapeDtypeStruct((m, n), _F32),
        compiler_params=_params(("parallel", "arbitrary")),
        name="matmul_residual",
    )(x, w, res)


def _ffn(x, norm_g, w_in, w_out):
    h = _rmsnorm(x, norm_g, MXU_DTYPE)
    act = _matmul_swiglu(h, w_in.astype(MXU_DTYPE))
    return _matmul_residual(act, w_out.astype(MXU_DTYPE), x, 0.5)


def _rope_tables(positions, d, half_width):
    inv = jnp.power(ROPE_THETA, -jnp.arange(0, d, 2, dtype=_F32) / d)
    ang = positions.astype(_F32).reshape(-1, 1) * inv
    pad = jnp.zeros((ang.shape[0], half_width - d // 2), _F32)
    cos, sin = jnp.cos(ang), jnp.sin(ang)
    ra = jnp.concatenate([cos, pad, cos, pad], axis=1)
    rb = jnp.concatenate([-sin, pad, sin, pad], axis=1)
    return ra, rb


def _mla_attn_kernel(q_ref, kv_ref, kr_ref, ra_ref, rb_ref, o_ref,
                     qs_ref, m_ref, l_ref, acc_ref, *, tq, scale):
    i = pl.program_id(2)
    q = q_ref[...]
    qr = q[:, LANES:].astype(_F32)
    qr = qr * ra_ref[...] + _rot_half(qr) * rb_ref[...]
    qs_ref[:, :LANES] = q[:, :LANES]
    qs_ref[:, LANES:] = qr.astype(qs_ref.dtype)
    m_ref[...] = jnp.full(m_ref.shape, -jnp.inf, _F32)
    l_ref[...] = jnp.zeros(l_ref.shape, _F32)
    acc_ref[...] = jnp.zeros(acc_ref.shape, _F32)

    def step(j, masked):
        off = pl.multiple_of(j * tq, tq)
        kv = kv_ref[pl.ds(off, tq), :]
        kc = jnp.concatenate([kv[:, :LANES], kr_ref[pl.ds(off, tq), :]], axis=1)
        s = lax.dot_general(qs_ref[...], kc, (((1,), (1,)), ((), ())),
                            preferred_element_type=_F32) * scale
        if masked:
            r = lax.broadcasted_iota(_I32, (tq, tq), 0) // CHUNK
            c = lax.broadcasted_iota(_I32, (tq, tq), 1) // CHUNK
            s = jnp.where(r >= c, s, -jnp.inf)
        m_prev = m_ref[...]
        m_new = jnp.maximum(m_prev, jnp.max(s, axis=1, keepdims=True))
        alpha = jnp.exp(m_prev - m_new)
        p = jnp.exp(s - m_new)
        l_ref[...] = alpha * l_ref[...] + jnp.sum(p, axis=1, keepdims=True)
        acc_ref[...] = alpha * acc_ref[...] + jnp.dot(
            p.astype(MXU_DTYPE), kv[:, LANES:], preferred_element_type=_F32)
        m_ref[...] = m_new

    def body(j, carry):
        step(j, False)
        return carry

    lax.fori_loop(0, i, body, 0)
    step(i, True)
    o_ref[...] = (acc_ref[...] / l_ref[...]).astype(o_ref.dtype)


def _mla_attention(q, kv, kr, ra, rb, batch, seq):
    tq = min(MLA_TQ, seq)
    nq = seq // tq
    scale = (A_NOPE + A_ROPE) ** -0.5
    return pl.pallas_call(
        functools.partial(_mla_attn_kernel, tq=tq, scale=scale),
        grid=(batch, A_HEADS, nq),
        in_specs=[pl.BlockSpec((tq, 2 * LANES), lambda b, h, i: (b * nq + i, h)),
                  pl.BlockSpec((seq, 2 * LANES), lambda b, h, i: (b, h)),
                  pl.BlockSpec((seq, LANES), lambda b, h, i: (b, 0)),
                  pl.BlockSpec((tq, LANES), lambda b, h, i: (b * nq + i, 0)),
                  pl.BlockSpec((tq, LANES), lambda b, h, i: (b * nq + i, 0))],
        out_specs=pl.BlockSpec((tq, A_V), lambda b, h, i: (b * nq + i, h)),
        out_shape=jax.ShapeDtypeStruct((batch * seq, A_HEADS * A_V), MXU_DTYPE),
        scratch_shapes=[pltpu.VMEM((tq, 2 * LANES), MXU_DTYPE),
                        pltpu.VMEM((tq, 1), _F32),
                        pltpu.VMEM((tq, 1), _F32),
                        pltpu.VMEM((tq, A_V), _F32)],
        compiler_params=_params(("parallel", "parallel", "arbitrary")),
        name="mla_attention",
    )(q, kv, kr, ra, rb)


def _pad_rope_cols(w, n_heads, nope):
    k = w.shape[0]
    w = w.reshape(k, n_heads, nope + A_ROPE)
    half = A_ROPE // 2
    z = jnp.zeros((k, n_heads, LANES // 2 - half), w.dtype)
    w = jnp.concatenate([w[..., :nope], w[..., nope:nope + half], z, w[..., nope + half:], z], axis=-1)
    return w.reshape(k, n_heads * (nope + LANES))


def _mla_mixer(x, positions, mix_g, w_in, q_norm, kv_norm, w_uq, w_ukv, w_out, batch, seq):
    assert A_NOPE == LANES and A_V == LANES and A_ROPE == LANES // 2
    h = _rmsnorm(x, mix_g, MXU_DTYPE)
    n_lat = A_Q_LORA + A_KV_LORA
    w_in_p = jnp.concatenate([w_in[:, :n_lat], _pad_rope_cols(w_in[:, n_lat:], 1, 0)], axis=1)
    lat = _matmul(h, w_in_p.astype(MXU_DTYPE), _F32, bn=w_in_p.shape[1])
    cq = _rmsnorm(lat, q_norm, MXU_DTYPE, col_block=0, width=A_Q_LORA)
    ckv = _rmsnorm(lat, kv_norm, MXU_DTYPE, col_block=A_Q_LORA // A_KV_LORA, width=A_KV_LORA)
    ra, rb = _rope_tables(positions, A_ROPE, LANES // 2)
    kr = _rope(lat, ra, rb, MXU_DTYPE, col_block=n_lat // LANES, width=LANES)
    q = _matmul(cq, _pad_rope_cols(w_uq, A_HEADS, A_NOPE).astype(MXU_DTYPE), MXU_DTYPE)
    kv = _matmul(ckv, w_ukv.astype(MXU_DTYPE), MXU_DTYPE)
    o = _mla_attention(q, kv, kr, ra, rb, batch, seq)
    return _matmul_residual(o, w_out.astype(MXU_DTYPE), x, 1.0)


def _dsa_kernel(qi_ref, wi_ref, ki_ref, ra_ref, rb_ref, q_ref, k_ref, v_ref, o_ref,
                qir_ref, key_ref, thr_ref, qs_ref, m_ref, l_ref, acc_ref,
                *, tq, tk, topk, group, wi_scale, scale):
    i = pl.program_id(1)
    h = pl.program_id(2)
    nkb = (i + 1) * (tq // tk)
    ra = ra_ref[...]
    rb = rb_ref[...]

    @pl.when(h == 0)
    def _select():
        for hd in range(B_IDX_HEADS):
            sl = slice(hd * LANES, (hd + 1) * LANES)
            x = qi_ref[:, sl].astype(_F32)
            qir_ref[:, sl] = (x * ra + _rot_half(x) * rb).astype(qir_ref.dtype)
        wi = wi_ref[...] * wi_scale

        def score_tile(j, carry):
            off = pl.multiple_of(j * tk, tk)
            ki = ki_ref[pl.ds(off, tk), :]
            sc = jnp.zeros((tq, tk), _F32)
            for hd in range(B_IDX_HEADS):
                d = lax.dot_general(qir_ref[:, hd * LANES:(hd + 1) * LANES], ki,
                                    (((1,), (1,)), ((), ())), preferred_element_type=_F32)
                sc = sc + jnp.maximum(d, 0.0) * wi[:, hd:hd + 1]
            r = (i * tq + lax.broadcasted_iota(_I32, (tq, tk), 0)) // CHUNK
            c = (off + lax.broadcasted_iota(_I32, (tq, tk), 1)) // CHUNK
            sc = jnp.where(r >= c, sc, -jnp.inf)
            bits = pltpu.bitcast(sc, _I32)
            key_ref[j] = jnp.where(bits >= 0, bits, bits ^ 0x7FFFFFFF)
            return carry

        lax.fori_loop(0, nkb, score_tile, 0)

        def bisect(b, t):
            cand = t + jnp.left_shift(jnp.int32(1), 31 - b)

            def count_tile(j, acc):
                kt = key_ref[j]
                for c in range(tk // LANES):
                    acc = acc + jnp.where(kt[:, c * LANES:(c + 1) * LANES] >= cand, 1.0, 0.0)
                return acc

            acc = lax.fori_loop(0, nkb, count_tile, jnp.zeros((tq, LANES), _F32))
            cnt = jnp.sum(acc, axis=1, keepdims=True)
            return jnp.where(cnt >= topk, cand, t)

        t = lax.fori_loop(0, 32, bisect, jnp.full((tq, 1), _INT_MIN, _I32))

        def count_gt_tile(j, acc):
            kt = key_ref[j]
            for c in range(tk // LANES):
                acc = acc + jnp.where(kt[:, c * LANES:(c + 1) * LANES] > t, 1.0, 0.0)
            return acc

        acc = lax.fori_loop(0, nkb, count_gt_tile, jnp.zeros((tq, LANES), _F32))
        need = topk - jnp.sum(acc, axis=1, keepdims=True)
        upper = jnp.where(lax.broadcasted_iota(_I32, (tk, tk), 0)
                          <= lax.broadcasted_iota(_I32, (tk, tk), 1), 1.0, 0.0).astype(MXU_DTYPE)

        def demote_tile(j, seen):
            kt = key_ref[j]
            eq = kt == t
            rank = seen + jnp.dot(jnp.where(eq, 1.0, 0.0).astype(MXU_DTYPE), upper,
                                  preferred_element_type=_F32)
            key_ref[j] = jnp.where(eq, jnp.where(rank > need, t - 1, kt), kt)
            return rank[:, tk - 1:tk]

        lax.fori_loop(0, nkb, demote_tile, jnp.zeros((tq, 1), _F32))
        thr_ref[...] = jnp.maximum(t, _KEY_OF_NEG_INF + 1)

    q = q_ref[...]
    for g in range(group):
        x = q[:, g * LANES:(g + 1) * LANES].astype(_F32)
        qs_ref[g * tq:(g + 1) * tq, :] = (x * ra + _rot_half(x) * rb).astype(qs_ref.dtype)
    m_ref[...] = jnp.full(m_ref.shape, _NEG_BIG, _F32)
    l_ref[...] = jnp.zeros(l_ref.shape, _F32)
    acc_ref[...] = jnp.zeros(acc_ref.shape, _F32)
    thr = thr_ref[...]

    def attend(j, carry):
        off = pl.multiple_of(j * tk, tk)
        k = k_ref[pl.ds(off, tk), :]
        v = v_ref[pl.ds(off, tk), :]
        bias = jnp.where(key_ref[j] >= thr, 0.0, _NEG_BIG)
        bias = jnp.concatenate([bias] * group, axis=0)
        s = lax.dot_general(qs_ref[...], k, (((1,), (1,)), ((), ())),
                            preferred_element_type=_F32) * scale + bias
        m_prev = m_ref[...]
        m_new = jnp.maximum(m_prev, jnp.max(s, axis=1, keepdims=True))
        alpha = jnp.exp(m_prev - m_new)
        p = jnp.exp(s - m_new)
        l_ref[...] = alpha * l_ref[...] + jnp.sum(p, axis=1, keepdims=True)
        acc_ref[...] = alpha * acc_ref[...] + jnp.dot(p.astype(MXU_DTYPE), v,
                                                      preferred_element_type=_F32)
        m_ref[...] = m_new
        return carry

    lax.fori_loop(0, nkb, attend, 0)
    out = acc_ref[...] / l_ref[...]
    for g in range(group):
        o_ref[:, g * LANES:(g + 1) * LANES] = out[g * tq:(g + 1) * tq, :].astype(o_ref.dtype)


def _dsa_attention(proj, k_roped, ki, wi, ra, rb, batch, seq):
    tq = min(DSA_TQ, seq)
    tk = min(DSA_TK, tq)
    nq = seq // tq
    group = B_HEADS // B_KV_HEADS
    topk = min(B_TOPK_MAX, seq // 4)
    qi_w = B_IDX_HEADS * B_IDX_DIM
    q_w = group * B_HEAD_DIM
    q_blk0 = qi_w // q_w
    v_blk0 = (qi_w + (B_HEADS + B_KV_HEADS) * B_HEAD_DIM) // B_HEAD_DIM
    kern = functools.partial(
        _dsa_kernel, tq=tq, tk=tk, topk=float(topk), group=group,
        wi_scale=B_IDX_HEADS ** -0.5 * B_IDX_DIM ** -0.5, scale=B_HEAD_DIM ** -0.5)
    return pl.pallas_call(
        kern,
        grid=(batch, nq, B_KV_HEADS),
        in_specs=[pl.BlockSpec((tq, qi_w), lambda b, i, h: (b * nq + i, 0)),
                  pl.BlockSpec((tq, LANES), lambda b, i, h: (b * nq + i, 0)),
                  pl.BlockSpec((seq, LANES), lambda b, i, h: (b, 0)),
                  pl.BlockSpec((tq, LANES), lambda b, i, h: (b * nq + i, 0)),
                  pl.BlockSpec((tq, LANES), lambda b, i, h: (b * nq + i, 0)),
                  pl.BlockSpec((tq, q_w), lambda b, i, h: (b * nq + i, q_blk0 + h)),
                  pl.BlockSpec((seq, B_HEAD_DIM), lambda b, i, h: (b, h)),
                  pl.BlockSpec((seq, B_HEAD_DIM), lambda b, i, h: (b, v_blk0 + h))],
        out_specs=pl.BlockSpec((tq, q_w), lambda b, i, h: (b * nq + i, h)),
        out_shape=jax.ShapeDtypeStruct((batch * seq, B_HEADS * B_HEAD_DIM), MXU_DTYPE),
        scratch_shapes=[pltpu.VMEM((tq, qi_w), MXU_DTYPE),
                        pltpu.VMEM((seq // tk, tq, tk), _I32),
                        pltpu.VMEM((tq, 1), _I32),
                        pltpu.VMEM((group * tq, B_HEAD_DIM), MXU_DTYPE),
                        pltpu.VMEM((group * tq, 1), _F32),
                        pltpu.VMEM((group * tq, 1), _F32),
                        pltpu.VMEM((group * tq, B_HEAD_DIM), _F32)],
        compiler_params=_params(("parallel", "arbitrary", "arbitrary")),
        name="dsa_attention",
    )(proj, wi, ki, ra, rb, proj, k_roped, proj)


def _dsa_mixer(x, positions, mix_g, w_in, idx_k_norm, w_out, batch, seq):
    assert B_HEAD_DIM == LANES and B_IDX_DIM == LANES and B_IDX_HEADS <= LANES
    h = _rmsnorm(x, mix_g, MXU_DTYPE)
    nq, nk = B_HEADS * B_HEAD_DIM, B_KV_HEADS * B_HEAD_DIM
    nqi = B_IDX_HEADS * B_IDX_DIM
    o_q, o_k, o_v, o_qi, o_ki, o_wi = np.cumsum([0, nq, nk, nk, nqi, B_IDX_DIM])
    w_main = jnp.concatenate([w_in[:, o_qi:o_ki], w_in[:, o_q:o_qi]], axis=1)
    w_side = jnp.concatenate(
        [w_in[:, o_ki:o_wi], w_in[:, o_wi:],
         jnp.zeros((w_in.shape[0], LANES - B_IDX_HEADS), w_in.dtype)], axis=1)
    proj = _matmul(h, w_main.astype(MXU_DTYPE), MXU_DTYPE, bn=512)
    side = _matmul(h, w_side.astype(MXU_DTYPE), _F32, bn=2 * LANES)
    ra, rb = _rope_tables(positions, B_HEAD_DIM, LANES // 2)
    k_roped = _rope(proj, ra, rb, MXU_DTYPE, col_block=(nqi + nq) // nk, width=nk)
    ki = _rmsnorm_rope(side, idx_k_norm, ra, rb, MXU_DTYPE, col_block=0)
    wi = side[:, LANES:]
    o = _dsa_attention(proj, k_roped, ki, wi, ra, rb, batch, seq)
    return _matmul_residual(o, w_out.astype(MXU_DTYPE), x, 1.0)


def _log_sigmoid(x):
    return jnp.minimum(x, 0.0) - jnp.log(1.0 + jnp.exp(-jnp.abs(x)))


def _mlstm_kernel(bias_ref, q_ref, k_ref, v_ref, o_ref, grow_ref, gcol_ref, hn_ref, y_ref,
                  c_ref, n_ref, m_ref, *, n_heads, dk):
    h = pl.program_id(1)
    c = pl.program_id(2)

    @pl.when(c == 0)
    def _init():
        c_ref[...] = jnp.zeros(c_ref.shape, _F32)
        n_ref[...] = jnp.zeros(n_ref.shape, _F32)
        m_ref[...] = jnp.zeros(m_ref.shape, _F32)

    b_i = bias_ref[h]
    b_f = bias_ref[n_heads + h]
    grow = grow_ref[0]
    gcol = gcol_ref[0]
    li_row = grow[0:1, :] + b_i
    lf_row = _log_sigmoid(grow[1:2, :] + b_f)
    li_col = gcol[:, 0:1] + b_i
    lf_col = _log_sigmoid(gcol[:, 1:2] + b_f)

    rows = lax.broadcasted_iota(_I32, (CHUNK, CHUNK), 0)
    cols = lax.broadcasted_iota(_I32, (CHUNK, CHUNK), 1)
    tril = rows >= cols
    b_col = jnp.sum(jnp.where(tril, lf_row, 0.0), axis=1, keepdims=True)
    b_row = jnp.sum(jnp.where(rows <= cols, lf_col, 0.0), axis=0, keepdims=True)
    b_end = jnp.sum(lf_row, axis=1, keepdims=True)

    m_prev = m_ref[...]
    a = b_col + m_prev
    d = jnp.where(tril, b_col - b_row + li_row, -jnp.inf)
    m_t = jnp.maximum(a, jnp.max(d, axis=1, keepdims=True))
    inter = jnp.exp(a - m_t)

    q = q_ref[...]
    k = k_ref[...]
    v = v_ref[...]
    qf = q.astype(_F32) * dk ** -0.5
    qs = qf.astype(MXU_DTYPE)
    qk = lax.dot_general(qs, k, (((1,), (1,)), ((), ())),
                         preferred_element_type=_F32) * jnp.exp(d - m_t)
    ct = c_ref[...]
    num = inter * jnp.dot(qs, ct.astype(MXU_DTYPE), preferred_element_type=_F32) + jnp.dot(
        qk.astype(MXU_DTYPE), v, preferred_element_type=_F32)
    n_row = n_ref[...]
    den = inter * jnp.sum(qf * n_row, axis=1, keepdims=True) + jnp.sum(qk, axis=1, keepdims=True)
    hs = num / jnp.maximum(jnp.abs(den), jnp.exp(-m_t))

    g_col = b_end - b_col + li_col
    m_new = jnp.maximum(b_end + m_prev, jnp.max(g_col, axis=0, keepdims=True))
    decay = jnp.exp(b_end + m_prev - m_new)
    wk = jnp.exp(g_col - m_new)
    vw = (v.astype(_F32) * wk).astype(MXU_DTYPE)
    c_ref[...] = decay * ct + lax.dot_general(k, vw, (((0,), (0,)), ((), ())),
                                              preferred_element_type=_F32)
    n_ref[...] = decay * n_row + jnp.sum(wk * k.astype(_F32), axis=0, keepdims=True)
    m_ref[...] = m_new

    hs = hs * lax.rsqrt(jnp.mean(hs * hs, axis=1, keepdims=True) + NORM_EPS)
    y_ref[...] = (hs * hn_ref[...] * jax.nn.sigmoid(o_ref[...].astype(_F32))).astype(y_ref.dtype)


def _mlstm_scan(proj, g_row, g_col, gate_bias, head_norm, batch, seq, dk, dv):
    nc = seq // CHUNK
    hq = C_HEADS
    v_blk0 = 2 * hq * dk // dv
    o_blk0 = v_blk0 + hq
    return pl.pallas_call(
        functools.partial(_mlstm_kernel, n_heads=hq, dk=dk),
        grid=(batch, hq, nc),
        in_specs=[pl.BlockSpec(memory_space=pltpu.SMEM),
                  pl.BlockSpec((CHUNK, dk), lambda b, h, c: (b * nc + c, h)),
                  pl.BlockSpec((CHUNK, dk), lambda b, h, c: (b * nc + c, hq + h)),
                  pl.BlockSpec((CHUNK, dv), lambda b, h, c: (b * nc + c, v_blk0 + h)),
                  pl.BlockSpec((CHUNK, dv), lambda b, h, c: (b * nc + c, o_blk0 + h)),
                  pl.BlockSpec((1, 2, CHUNK), lambda b, h, c: ((b * hq + h) * nc + c, 0, 0)),
                  pl.BlockSpec((1, CHUNK, 2), lambda b, h, c: ((b * hq + h) * nc + c, 0, 0)),
                  pl.BlockSpec((1, dv), lambda b, h, c: (0, h))],
        out_specs=pl.BlockSpec((CHUNK, dv), lambda b, h, c: (b * nc + c, h)),
        out_shape=jax.ShapeDtypeStruct((batch * seq, hq * dv), MXU_DTYPE),
        scratch_shapes=[pltpu.VMEM((dk, dv), _F32),
                        pltpu.VMEM((1, dk), _F32),
                        pltpu.VMEM((1, 1), _F32)],
        compiler_params=_params(("parallel", "parallel", "arbitrary")),
        name="mlstm_scan",
    )(gate_bias.astype(_F32), proj, proj, proj, proj, g_row, g_col,
      head_norm.reshape(1, hq * dv).astype(_F32))


def _mlstm_mixer(x, mix_g, w_in, gate_bias, head_norm, w_out, batch, seq):
    d_model = x.shape[1]
    dv = d_model // C_HEADS
    dk = dv // 2
    hq = C_HEADS
    h = _rmsnorm(x, mix_g, MXU_DTYPE)
    n_main = 2 * hq * dk + 2 * hq * dv
    w_gate = jnp.concatenate(
        [w_in[:, n_main:], jnp.zeros((w_in.shape[0], LANES - 2 * hq), w_in.dtype)], axis=1)
    proj = _matmul(h, w_in[:, :n_main].astype(MXU_DTYPE), MXU_DTYPE)
    gates = _matmul(h, w_gate.astype(MXU_DTYPE), _F32, bn=LANES)
    nc = seq // CHUNK
    g = gates[:, :2 * hq].reshape(batch, nc, CHUNK, 2, hq)
    g_row = g.transpose(0, 4, 1, 3, 2).reshape(batch * hq * nc, 2, CHUNK)
    g_col = g.transpose(0, 4, 1, 2, 3).reshape(batch * hq * nc, CHUNK, 2)
    y = _mlstm_scan(proj, g_row, g_col, gate_bias, head_norm, batch, seq, dk, dv)
    return _matmul_residual(y, w_out.astype(MXU_DTYPE), x, 1.0)


def kernel(x, positions, ffn1_norm, ffn1_w_in, ffn1_w_out, mix_norm, ffn2_norm, ffn2_w_in, ffn2_w_out,
           a_w_in, a_q_norm, a_kv_norm, a_w_uq, a_w_ukv, a_w_out,
           b_w_in, b_idx_k_norm, b_w_out,
           c_w_in, c_gate_bias, c_head_norm, c_w_out, final_norm):
    batch, seq, d_model = x.shape
    depth = ffn1_norm.shape[0]
    x = x.reshape(batch * seq, d_model)
    for i in range(depth):
        x = _ffn(x, ffn1_norm[i], ffn1_w_in[i], ffn1_w_out[i])
        kind, j = i % N_MIXERS, i // N_MIXERS
        if kind == 0:
            x = _mla_mixer(x, positions, mix_norm[i], a_w_in[j], a_q_norm[j], a_kv_norm[j],
                           a_w_uq[j], a_w_ukv[j], a_w_out[j], batch, seq)
        elif kind == 1:
            x = _dsa_mixer(x, positions, mix_norm[i], b_w_in[j], b_idx_k_norm[j], b_w_out[j], batch, seq)
        else:
            x = _mlstm_mixer(x, mix_norm[i], c_w_in[j], c_gate_bias[j], c_head_norm[j], c_w_out[j],
                             batch, seq)
        x = _ffn(x, ffn2_norm[i], ffn2_w_in[i], ffn2_w_out[i])
    return _rmsnorm(x, final_norm, _F32).reshape(batch, seq, d_model)
```

```python
import functools

import jax
import jax.numpy as jnp
import numpy as np
from jax import lax
from jax.experimental import pallas as pl
from jax.experimental.pallas import tpu as pltpu

CHUNK = 64
ROPE_THETA = 10000.0
NORM_EPS = 1e-6
D_FF = 6144

A_HEADS = 64
A_Q_LORA = 1024
A_KV_LORA = 512
A_NOPE = 128
A_ROPE = 64
A_V = 128

B_HEADS = 32
B_KV_HEADS = 8
B_HEAD_DIM = 128
B_IDX_HEADS = 32
B_IDX_DIM = 128
B_TOPK_MAX = 256

C_HEADS = 8
N_MIXERS = 3

LANES = 128
SUBLANES = 8
ONES_ROWS = 16
LOG2E = 1.4426950408889634
MXU_DTYPE = jnp.bfloat16
VMEM_LIMIT_MB = 56

ROW_TILE = 256
MM_BM = 512
MM_BN = 1024
MLA_TQ = 512
DSA_TQ = 256
DSA_TK = 256

_F32 = jnp.float32
_I32 = jnp.int32
_NEG_BIG = -1e30
_INT_MIN = -(2 ** 31)
_KEY_OF_NEG_INF = -2139095041


def _params(semantics):
    return pltpu.CompilerParams(dimension_semantics=semantics,
                                vmem_limit_bytes=VMEM_LIMIT_MB << 20)


def _rot_half(x):
    return pltpu.roll(x, LANES // 2, 1)


def _rmsnorm_kernel(x_ref, g_ref, o_ref):
    x = x_ref[...].astype(_F32)
    y = x * lax.rsqrt(jnp.mean(x * x, axis=-1, keepdims=True) + NORM_EPS)
    o_ref[...] = (y * g_ref[...]).astype(o_ref.dtype)


def _rmsnorm(x, g, out_dtype, col_block=0, width=None):
    t = x.shape[0]
    width = x.shape[1] if width is None else width
    bt = min(ROW_TILE, t)
    return pl.pallas_call(
        _rmsnorm_kernel,
        grid=(t // bt,),
        in_specs=[pl.BlockSpec((bt, width), lambda i: (i, col_block)),
                  pl.BlockSpec((1, width), lambda i: (0, 0))],
        out_specs=pl.BlockSpec((bt, width), lambda i: (i, 0)),
        out_shape=jax.ShapeDtypeStruct((t, width), out_dtype),
        compiler_params=_params(("parallel",)),
        name="rmsnorm",
    )(x, g.reshape(1, width).astype(_F32))


def _rope_kernel(x_ref, ra_ref, rb_ref, o_ref, *, n_blocks):
    ra = ra_ref[...]
    rb = rb_ref[...]
    for c in range(n_blocks):
        sl = slice(c * LANES, (c + 1) * LANES)
        x = x_ref[:, sl].astype(_F32)
        o_ref[:, sl] = (x * ra + _rot_half(x) * rb).astype(o_ref.dtype)


def _rope(x, ra, rb, out_dtype, col_block=0, width=None):
    t = x.shape[0]
    width = x.shape[1] if width is None else width
    bt = min(ROW_TILE, t)
    return pl.pallas_call(
        functools.partial(_rope_kernel, n_blocks=width // LANES),
        grid=(t // bt,),
        in_specs=[pl.BlockSpec((bt, width), lambda i: (i, col_block)),
                  pl.BlockSpec((bt, LANES), lambda i: (i, 0)),
                  pl.BlockSpec((bt, LANES), lambda i: (i, 0))],
        out_specs=pl.BlockSpec((bt, width), lambda i: (i, 0)),
        out_shape=jax.ShapeDtypeStruct((t, width), out_dtype),
        compiler_params=_params(("parallel",)),
        name="rope",
    )(x, ra, rb)


def _rmsnorm_rope_kernel(x_ref, g_ref, ra_ref, rb_ref, o_ref):
    x = x_ref[...].astype(_F32)
    y = x * lax.rsqrt(jnp.mean(x * x, axis=-1, keepdims=True) + NORM_EPS) * g_ref[...]
    o_ref[...] = (y * ra_ref[...] + _rot_half(y) * rb_ref[...]).astype(o_ref.dtype)


def _rmsnorm_rope(x, g, ra, rb, out_dtype, col_block=0):
    t = x.shape[0]
    bt = min(ROW_TILE, t)
    return pl.pallas_call(
        _rmsnorm_rope_kernel,
        grid=(t // bt,),
        in_specs=[pl.BlockSpec((bt, LANES), lambda i: (i, col_block)),
                  pl.BlockSpec((1, LANES), lambda i: (0, 0)),
                  pl.BlockSpec((bt, LANES), lambda i: (i, 0)),
                  pl.BlockSpec((bt, LANES), lambda i: (i, 0))],
        out_specs=pl.BlockSpec((bt, LANES), lambda i: (i, 0)),
        out_shape=jax.ShapeDtypeStruct((t, LANES), out_dtype),
        compiler_params=_params(("parallel",)),
        name="rmsnorm_rope",
    )(x, g.reshape(1, LANES).astype(_F32), ra, rb)


def _mm_kernel(x_ref, w_ref, o_ref):
    o_ref[...] = jnp.dot(x_ref[...], w_ref[...],
                         preferred_element_type=_F32).astype(o_ref.dtype)


def _mm_swiglu_kernel(x_ref, wg_ref, wu_ref, o_ref):
    x = x_ref[...]
    g = jnp.dot(x, wg_ref[...], preferred_element_type=_F32)
    u = jnp.dot(x, wu_ref[...], preferred_element_type=_F32)
    o_ref[...] = (g * jax.nn.sigmoid(g) * u).astype(o_ref.dtype)


def _mm_residual_kernel(x_ref, w_ref, r_ref, o_ref, *, scale):
    y = jnp.dot(x_ref[...], w_ref[...], preferred_element_type=_F32)
    o_ref[...] = r_ref[...] + scale * y


def _mm_t_kernel(w_ref, x_ref, o_ref):
    o_ref[0] = lax.dot_general(w_ref[...], x_ref[...], (((1,), (1,)), ((), ())),
                               preferred_element_type=_F32).astype(o_ref.dtype)


def _tile(n, pref):
    return pref if n % pref == 0 else n


def _matmul(x, w, out_dtype, bm=None, bn=None):
    m, k = x.shape
    n = w.shape[1]
    bm = _tile(m, bm or MM_BM)
    bn = _tile(n, bn or MM_BN)
    return pl.pallas_call(
        _mm_kernel,
        grid=(m // bm, n // bn),
        in_specs=[pl.BlockSpec((bm, k), lambda i, j: (i, 0)),
                  pl.BlockSpec((k, bn), lambda i, j: (0, j))],
        out_specs=pl.BlockSpec((bm, bn), lambda i, j: (i, j)),
        out_shape=jax.ShapeDtypeStruct((m, n), out_dtype),
        compiler_params=_params(("parallel", "arbitrary")),
        name="matmul",
    )(x, w)


def _matmul_t(w_t, x, tk, out_dtype, bn=None):
    n, k = w_t.shape
    t = x.shape[0]
    bn = _tile(n, bn or MM_BN)
    return pl.pallas_call(
        _mm_t_kernel,
        grid=(t // tk, n // bn),
        in_specs=[pl.BlockSpec((bn, k), lambda i, j: (j, 0)),
                  pl.BlockSpec((tk, k), lambda i, j: (i, 0))],
        out_specs=pl.BlockSpec((1, bn, tk), lambda i, j: (i, j, 0)),
        out_shape=jax.ShapeDtypeStruct((t // tk, n, tk), out_dtype),
        compiler_params=_params(("parallel", "arbitrary")),
        name="matmul_t",
    )(w_t, x)


def _matmul_swiglu(x, w_in, bm=None, bn=512):
    m, k = x.shape
    f = w_in.shape[1] // 2
    bm = _tile(m, bm or MM_BM)
    bn = _tile(f, bn)
    nb = f // bn
    return pl.pallas_call(
        _mm_swiglu_kernel,
        grid=(m // bm, nb),
        in_specs=[pl.BlockSpec((bm, k), lambda i, j: (i, 0)),
                  pl.BlockSpec((k, bn), lambda i, j: (0, j)),
                  pl.BlockSpec((k, bn), lambda i, j: (0, j + nb))],
        out_specs=pl.BlockSpec((bm, bn), lambda i, j: (i, j)),
        out_shape=jax.ShapeDtypeStruct((m, f), MXU_DTYPE),
        compiler_params=_params(("parallel", "arbitrary")),
        name="matmul_swiglu",
    )(x, w_in, w_in)


def _matmul_residual(x, w, res, scale, bm=None, bn=512):
    m, k = x.shape
    n = w.shape[1]
    bm = _tile(m, bm or MM_BM)
    bn = _tile(n, bn)
    return pl.pallas_call(
        functools.partial(_mm_residual_kernel, scale=scale),
        grid=(m // bm, n // bn),
        in_specs=[pl.BlockSpec((bm, k), lambda i, j: (i, 0)),
                  pl.BlockSpec((k, bn), lambda i, j: (0, j)),
                  pl.BlockSpec((bm, bn), lambda i, j: (i, j))],
        out_specs=pl.BlockSpec((bm, bn), lambda i, j: (i, j)),
        out_shape=jax.ShapeDtypeStruct((m, n), _F32),
        compiler_params=_params(("parallel", "arbitrary")),
        name="matmul_residual",
    )(x, w, res)


def _ffn(x, norm_g, w_in, w_out):
    h = _rmsnorm(x, norm_g, MXU_DTYPE)
    act = _matmul_swiglu(h, w_in.astype(MXU_DTYPE))
    return _matmul_residual(act, w_out.astype(MXU_DTYPE), x, 0.5)


def _rope_tables(positions, d, half_width):
    inv = jnp.power(ROPE_THETA, -jnp.arange(0, d, 2, dtype=_F32) / d)
    ang = positions.astype(_F32).reshape(-1, 1) * inv
    pad = jnp.zeros((ang.shape[0], half_width - d // 2), _F32)
    cos, sin = jnp.cos(ang), jnp.sin(ang)
    ra = jnp.concatenate([cos, pad, cos, pad], axis=1)
    rb = jnp.concatenate([-sin, pad, sin, pad], axis=1)
    return ra, rb


def _softmax_step(s, vt, m_ref, acc_ref):
    m_prev = m_ref[...]
    m_new = jnp.maximum(m_prev, jnp.max(s, axis=0, keepdims=True))
    alpha = jnp.exp2(m_prev - m_new)
    p = jnp.exp2(s - m_new).astype(MXU_DTYPE)
    vt1 = jnp.concatenate([vt, jnp.ones((ONES_ROWS, vt.shape[1]), vt.dtype)], axis=0)
    acc_ref[...] = alpha * acc_ref[...] + jnp.dot(vt1, p, preferred_element_type=_F32)
    m_ref[...] = m_new


def _pipelined_sweep(n, score, update, sa_ref, sb_ref):
    sa_ref[...] = score(0, True)

    def body(p, carry):
        t = 2 * p
        sb_ref[...] = score(t + 1, False)
        update(t, sa_ref[...])
        sa_ref[...] = score(t + 2, False)
        update(t + 1, sb_ref[...])
        return carry

    n_pairs = (n - 1) // 2
    lax.fori_loop(0, n_pairs, body, 0)
    t = 2 * n_pairs

    @pl.when(t == n - 1)
    def _odd_tail():
        update(t, sa_ref[...])

    @pl.when(t < n - 1)
    def _even_tail():
        sb_ref[...] = score(t + 1, False)
        update(t, sa_ref[...])
        update(t + 1, sb_ref[...])


def _softmax_finish(acc_ref, dv):
    acc = acc_ref[...]
    return acc[:dv, :] / acc[dv:dv + 1, :]


def _mla_attn_kernel(q_ref, kn_ref, kr_ref, vt_ref, ra_ref, rb_ref, o_ref,
                     qt_ref, m_ref, acc_ref, sa_ref, sb_ref, *, tq, qscale):
    i = pl.program_id(2)
    q = q_ref[...]
    qn = q[:, :LANES].astype(_F32) * qscale
    qr = q[:, LANES:].astype(_F32)
    qr = (qr * ra_ref[...] + _rot_half(qr) * rb_ref[...]) * qscale
    qt_ref[:LANES, :] = qn.T.astype(qt_ref.dtype)
    qt_ref[LANES:, :] = qr.T.astype(qt_ref.dtype)
    m_ref[...] = jnp.full(m_ref.shape, -jnp.inf, _F32)
    acc_ref[...] = jnp.zeros(acc_ref.shape, _F32)

    def key_tile(t):
        return jnp.where(t == 0, i, t - 1)

    def score(t, first):
        off = pl.multiple_of(key_tile(t) * tq, tq)
        kc = jnp.concatenate([kn_ref[pl.ds(off, tq), :], kr_ref[pl.ds(off, tq), :]], axis=1)
        s = jnp.dot(kc, qt_ref[...], preferred_element_type=_F32)
        if first:
            kchunk = lax.broadcasted_iota(_I32, (tq, tq), 0) // CHUNK
            qchunk = lax.broadcasted_iota(_I32, (tq, tq), 1) // CHUNK
            s = jnp.where(qchunk >= kchunk, s, -jnp.inf)
        return s

    def update(t, s):
        _softmax_step(s, vt_ref[key_tile(t)], m_ref, acc_ref)

    _pipelined_sweep(i + 1, score, update, sa_ref, sb_ref)
    o_ref[...] = _softmax_finish(acc_ref, A_V).T.astype(o_ref.dtype)


def _mla_attention(q, kn, kr, vt, ra, rb, batch, seq, tq):
    nq = seq // tq
    qscale = (A_NOPE + A_ROPE) ** -0.5 * LOG2E
    return pl.pallas_call(
        functools.partial(_mla_attn_kernel, tq=tq, qscale=qscale),
        grid=(batch, A_HEADS, nq),
        in_specs=[pl.BlockSpec((tq, 2 * LANES), lambda b, h, i: (b * nq + i, h)),
                  pl.BlockSpec((seq, LANES), lambda b, h, i: (b, h)),
                  pl.BlockSpec((seq, LANES), lambda b, h, i: (b, 0)),
                  pl.BlockSpec((nq, A_V, tq), lambda b, h, i: (b, h, 0)),
                  pl.BlockSpec((tq, LANES), lambda b, h, i: (b * nq + i, 0)),
                  pl.BlockSpec((tq, LANES), lambda b, h, i: (b * nq + i, 0))],
        out_specs=pl.BlockSpec((tq, A_V), lambda b, h, i: (b * nq + i, h)),
        out_shape=jax.ShapeDtypeStruct((batch * seq, A_HEADS * A_V), MXU_DTYPE),
        scratch_shapes=[pltpu.VMEM((2 * LANES, tq), MXU_DTYPE),
                        pltpu.VMEM((1, tq), _F32),
                        pltpu.VMEM((A_V + ONES_ROWS, tq), _F32),
                        pltpu.VMEM((tq, tq), _F32),
                        pltpu.VMEM((tq, tq), _F32)],
        compiler_params=_params(("parallel", "parallel", "arbitrary")),
        name="mla_attention",
    )(q, kn, kr, vt, ra, rb)


def _pad_rope_cols(w, n_heads, nope):
    k = w.shape[0]
    w = w.reshape(k, n_heads, nope + A_ROPE)
    half = A_ROPE // 2
    z = jnp.zeros((k, n_heads, LANES // 2 - half), w.dtype)
    w = jnp.concatenate([w[..., :nope], w[..., nope:nope + half], z, w[..., nope + half:], z], axis=-1)
    return w.reshape(k, n_heads * (nope + LANES))


def _mla_mixer(x, positions, mix_g, w_in, q_norm, kv_norm, w_uq, w_ukv, w_out, batch, seq):
    assert A_NOPE == LANES and A_V == LANES and A_ROPE == LANES // 2
    tq = min(MLA_TQ, seq)
    h = _rmsnorm(x, mix_g, MXU_DTYPE)
    n_lat = A_Q_LORA + A_KV_LORA
    w_in_p = jnp.concatenate([w_in[:, :n_lat], _pad_rope_cols(w_in[:, n_lat:], 1, 0)], axis=1)
    lat = _matmul(h, w_in_p.astype(MXU_DTYPE), _F32, bn=w_in_p.shape[1])
    cq = _rmsnorm(lat, q_norm, MXU_DTYPE, col_block=0, width=A_Q_LORA)
    ckv = _rmsnorm(lat, kv_norm, MXU_DTYPE, col_block=A_Q_LORA // A_KV_LORA, width=A_KV_LORA)
    ra, rb = _rope_tables(positions, A_ROPE, LANES // 2)
    kr = _rope(lat, ra, rb, MXU_DTYPE, col_block=n_lat // LANES, width=LANES)
    q = _matmul(cq, _pad_rope_cols(w_uq, A_HEADS, A_NOPE).astype(MXU_DTYPE), MXU_DTYPE)
    w_ukv = w_ukv.reshape(A_KV_LORA, A_HEADS, A_NOPE + A_V)
    w_kn = w_ukv[..., :A_NOPE].reshape(A_KV_LORA, A_HEADS * A_NOPE)
    w_vt = w_ukv[..., A_NOPE:].reshape(A_KV_LORA, A_HEADS * A_V).T
    kn = _matmul(ckv, w_kn.astype(MXU_DTYPE), MXU_DTYPE)
    vt = _matmul_t(w_vt.astype(MXU_DTYPE), ckv, tq, MXU_DTYPE)
    o = _mla_attention(q, kn, kr, vt, ra, rb, batch, seq, tq)
    return _matmul_residual(o, w_out.astype(MXU_DTYPE), x, 1.0)


def _dsa_kernel(qi_ref, wit_ref, ki_ref, ra_ref, rb_ref, q_ref, k_ref, vt_ref, o_ref,
                qit_ref, key_ref, thr_ref, qt_ref, m_ref, acc_ref, sa_ref, sb_ref,
                *, tq, tk, topk, group, wi_scale, qscale):
    i = pl.program_id(1)
    h = pl.program_id(2)
    nkb = (i + 1) * (tq // tk)
    ra = ra_ref[...]
    rb = rb_ref[...]

    @pl.when(h == 0)
    def _select():
        for hd in range(B_IDX_HEADS):
            sl = slice(hd * LANES, (hd + 1) * LANES)
            x = qi_ref[:, sl].astype(_F32)
            x = x * ra + _rot_half(x) * rb
            qit_ref[sl, :] = x.T.astype(qit_ref.dtype)
        wit = wit_ref[...] * wi_scale

        def score_tile(j, carry):
            off = pl.multiple_of(j * tk, tk)
            ki = ki_ref[pl.ds(off, tk), :]
            sc = jnp.zeros((tk, tq), _F32)
            for hd in range(B_IDX_HEADS):
                d = jnp.dot(ki, qit_ref[hd * LANES:(hd + 1) * LANES, :],
                            preferred_element_type=_F32)
                sc = sc + jnp.maximum(d, 0.0) * wit[hd:hd + 1, :]
            kchunk = (off + lax.broadcasted_iota(_I32, (tk, tq), 0)) // CHUNK
            qchunk = (i * tq + lax.broadcasted_iota(_I32, (tk, tq), 1)) // CHUNK
            sc = jnp.where(qchunk >= kchunk, sc, -jnp.inf)
            bits = pltpu.bitcast(sc, _I32)
            key_ref[j] = jnp.where(bits >= 0, bits, bits ^ 0x7FFFFFFF)
            return carry

        lax.fori_loop(0, nkb, score_tile, 0)

        def count(pred):
            def tile(j, acc):
                hit = jnp.where(pred(key_ref[j]), 1.0, 0.0)
                return acc + jnp.sum(hit.reshape(tk // SUBLANES, SUBLANES, tq), axis=0)

            acc = lax.fori_loop(0, nkb, tile, jnp.zeros((SUBLANES, tq), _F32))
            return jnp.sum(acc, axis=0, keepdims=True)

        def bisect(b, t):
            cand = t + jnp.left_shift(jnp.int32(1), 31 - b)
            return jnp.where(count(lambda kt: kt >= cand) >= topk, cand, t)

        t = lax.fori_loop(0, 32, bisect, jnp.full((1, tq), _INT_MIN, _I32))

        need = topk - count(lambda kt: kt > t)
        lower = jnp.where(lax.broadcasted_iota(_I32, (tk, tk), 0)
                          >= lax.broadcasted_iota(_I32, (tk, tk), 1), 1.0, 0.0).astype(MXU_DTYPE)

        def demote_tile(j, seen):
            kt = key_ref[j]
            eq = kt == t
            rank = seen + jnp.dot(lower, jnp.where(eq, 1.0, 0.0).astype(MXU_DTYPE),
                                  preferred_element_type=_F32)
            key_ref[j] = jnp.where(eq, jnp.where(rank > need, t - 1, kt), kt)
            return rank[tk - 1:tk, :]

        lax.fori_loop(0, nkb, demote_tile, jnp.zeros((1, tq), _F32))
        thr_ref[...] = jnp.maximum(t, _KEY_OF_NEG_INF + 1)

    q = q_ref[...]
    for g in range(group):
        x = q[:, g * LANES:(g + 1) * LANES].astype(_F32)
        x = (x * ra + _rot_half(x) * rb) * qscale
        qt_ref[:, g * tq:(g + 1) * tq] = x.T.astype(qt_ref.dtype)
    m_ref[...] = jnp.full(m_ref.shape, _NEG_BIG, _F32)
    acc_ref[...] = jnp.zeros(acc_ref.shape, _F32)
    thr = thr_ref[...]

    def score(j, first):
        off = pl.multiple_of(j * tk, tk)
        bias = jnp.where(key_ref[j] >= thr, 0.0, _NEG_BIG)
        bias = jnp.concatenate([bias] * group, axis=1)
        return jnp.dot(k_ref[pl.ds(off, tk), :], qt_ref[...], preferred_element_type=_F32) + bias

    def update(j, s):
        _softmax_step(s, vt_ref[j], m_ref, acc_ref)

    _pipelined_sweep(nkb, score, update, sa_ref, sb_ref)
    out = _softmax_finish(acc_ref, B_HEAD_DIM)
    for g in range(group):
        o_ref[:, g * LANES:(g + 1) * LANES] = out[:, g * tq:(g + 1) * tq].T.astype(o_ref.dtype)


def _dsa_attention(proj, k_roped, vt, ki, wit, ra, rb, batch, seq, tq, tk):
    nq = seq // tq
    group = B_HEADS // B_KV_HEADS
    topk = min(B_TOPK_MAX, seq // 4)
    qi_w = B_IDX_HEADS * B_IDX_DIM
    q_w = group * B_HEAD_DIM
    q_blk0 = qi_w // q_w
    kern = functools.partial(
        _dsa_kernel, tq=tq, tk=tk, topk=float(topk), group=group,
        wi_scale=B_IDX_HEADS ** -0.5 * B_IDX_DIM ** -0.5, qscale=B_HEAD_DIM ** -0.5 * LOG2E)
    return pl.pallas_call(
        kern,
        grid=(batch, nq, B_KV_HEADS),
        in_specs=[pl.BlockSpec((tq, qi_w), lambda b, i, h: (b * nq + i, 0)),
                  pl.BlockSpec((B_IDX_HEADS, tq), lambda b, i, h: (0, b * nq + i)),
                  pl.BlockSpec((seq, LANES), lambda b, i, h: (b, 0)),
                  pl.BlockSpec((tq, LANES), lambda b, i, h: (b * nq + i, 0)),
                  pl.BlockSpec((tq, LANES), lambda b, i, h: (b * nq + i, 0)),
                  pl.BlockSpec((tq, q_w), lambda b, i, h: (b * nq + i, q_blk0 + h)),
                  pl.BlockSpec((seq, B_HEAD_DIM), lambda b, i, h: (b, h)),
                  pl.BlockSpec((seq // tk, B_HEAD_DIM, tk), lambda b, i, h: (b, h, 0))],
        out_specs=pl.BlockSpec((tq, q_w), lambda b, i, h: (b * nq + i, h)),
        out_shape=jax.ShapeDtypeStruct((batch * seq, B_HEADS * B_HEAD_DIM), MXU_DTYPE),
        scratch_shapes=[pltpu.VMEM((qi_w, tq), MXU_DTYPE),
                        pltpu.VMEM((seq // tk, tk, tq), _I32),
                        pltpu.VMEM((1, tq), _I32),
                        pltpu.VMEM((B_HEAD_DIM, group * tq), MXU_DTYPE),
                        pltpu.VMEM((1, group * tq), _F32),
                        pltpu.VMEM((B_HEAD_DIM + ONES_ROWS, group * tq), _F32),
                        pltpu.VMEM((tk, group * tq), _F32),
                        pltpu.VMEM((tk, group * tq), _F32)],
        compiler_params=_params(("parallel", "arbitrary", "arbitrary")),
        name="dsa_attention",
    )(proj, wit, ki, ra, rb, proj, k_roped, vt)


def _dsa_mixer(x, positions, mix_g, w_in, idx_k_norm, w_out, batch, seq):
    assert B_HEAD_DIM == LANES and B_IDX_DIM == LANES and B_IDX_HEADS % SUBLANES == 0
    tq = min(DSA_TQ, seq)
    tk = min(DSA_TK, tq)
    h = _rmsnorm(x, mix_g, MXU_DTYPE)
    nq, nk = B_HEADS * B_HEAD_DIM, B_KV_HEADS * B_HEAD_DIM
    nqi = B_IDX_HEADS * B_IDX_DIM
    o_q, o_k, o_v, o_qi, o_ki, o_wi = np.cumsum([0, nq, nk, nk, nqi, B_IDX_DIM])
    w_main = jnp.concatenate([w_in[:, o_qi:o_ki], w_in[:, o_q:o_v]], axis=1)
    w_side = jnp.concatenate(
        [w_in[:, o_ki:o_wi], w_in[:, o_wi:],
         jnp.zeros((w_in.shape[0], LANES - B_IDX_HEADS), w_in.dtype)], axis=1)
    proj = _matmul(h, w_main.astype(MXU_DTYPE), MXU_DTYPE, bn=512)
    side = _matmul(h, w_side.astype(MXU_DTYPE), _F32, bn=2 * LANES)
    vt = _matmul_t(w_in[:, o_v:o_qi].T.astype(MXU_DTYPE), h, tk, MXU_DTYPE)
    ra, rb = _rope_tables(positions, B_HEAD_DIM, LANES // 2)
    k_roped = _rope(proj, ra, rb, MXU_DTYPE, col_block=(nqi + nq) // nk, width=nk)
    ki = _rmsnorm_rope(side, idx_k_norm, ra, rb, MXU_DTYPE, col_block=0)
    wit = side[:, LANES:LANES + B_IDX_HEADS].T
    o = _dsa_attention(proj, k_roped, vt, ki, wit, ra, rb, batch, seq, tq, tk)
    return _matmul_residual(o, w_out.astype(MXU_DTYPE), x, 1.0)


def _log_sigmoid(x):
    return jnp.minimum(x, 0.0) - jnp.log(1.0 + jnp.exp(-jnp.abs(x)))


def _mlstm_kernel(bias_ref, q_ref, k_ref, v_ref, o_ref, grow_ref, gcol_ref, hn_ref, y_ref,
                  c_ref, n_ref, m_ref, *, n_heads, dk):
    h = pl.program_id(1)
    c = pl.program_id(2)

    @pl.when(c == 0)
    def _init():
        c_ref[...] = jnp.zeros(c_ref.shape, _F32)
        n_ref[...] = jnp.zeros(n_ref.shape, _F32)
        m_ref[...] = jnp.zeros(m_ref.shape, _F32)

    b_i = bias_ref[h]
    b_f = bias_ref[n_heads + h]
    grow = grow_ref[0]
    gcol = gcol_ref[0]
    li_row = grow[0:1, :] + b_i
    lf_row = _log_sigmoid(grow[1:2, :] + b_f)
    li_col = gcol[:, 0:1] + b_i
    lf_col = _log_sigmoid(gcol[:, 1:2] + b_f)

    rows = lax.broadcasted_iota(_I32, (CHUNK, CHUNK), 0)
    cols = lax.broadcasted_iota(_I32, (CHUNK, CHUNK), 1)
    tril = rows >= cols
    b_col = jnp.sum(jnp.where(tril, lf_row, 0.0), axis=1, keepdims=True)
    b_row = jnp.sum(jnp.where(rows <= cols, lf_col, 0.0), axis=0, keepdims=True)
    b_end = jnp.sum(lf_row, axis=1, keepdims=True)

    m_prev = m_ref[...]
    a = b_col + m_prev
    d = jnp.where(tril, b_col - b_row + li_row, -jnp.inf)
    m_t = jnp.maximum(a, jnp.max(d, axis=1, keepdims=True))
    inter = jnp.exp(a - m_t)

    q = q_ref[...]
    k = k_ref[...]
    v = v_ref[...]
    qf = q.astype(_F32) * dk ** -0.5
    qs = qf.astype(MXU_DTYPE)
    qk = lax.dot_general(qs, k, (((1,), (1,)), ((), ())),
                         preferred_element_type=_F32) * jnp.exp(d - m_t)
    ct = c_ref[...]
    num = inter * jnp.dot(qs, ct.astype(MXU_DTYPE), preferred_element_type=_F32) + jnp.dot(
        qk.astype(MXU_DTYPE), v, preferred_element_type=_F32)
    n_row = n_ref[...]
    den = inter * jnp.sum(qf * n_row, axis=1, keepdims=True) + jnp.sum(qk, axis=1, keepdims=True)
    hs = num / jnp.maximum(jnp.abs(den), jnp.exp(-m_t))

    g_col = b_end - b_col + li_col
    m_new = jnp.maximum(b_end + m_prev, jnp.max(g_col, axis=0, keepdims=True))
    decay = jnp.exp(b_end + m_prev - m_new)
    wk = jnp.exp(g_col - m_new)
    vw = (v.astype(_F32) * wk).astype(MXU_DTYPE)
    c_ref[...] = decay * ct + lax.dot_general(k, vw, (((0,), (0,)), ((), ())),
                                              preferred_element_type=_F32)
    n_ref[...] = decay * n_row + jnp.sum(wk * k.astype(_F32), axis=0, keepdims=True)
    m_ref[...] = m_new

    hs = hs * lax.rsqrt(jnp.mean(hs * hs, axis=1, keepdims=True) + NORM_EPS)
    y_ref[...] = (hs * hn_ref[...] * jax.nn.sigmoid(o_ref[...].astype(_F32))).astype(y_ref.dtype)


def _mlstm_scan(proj, g_row, g_col, gate_bias, head_norm, batch, seq, dk, dv):
    nc = seq // CHUNK
    hq = C_HEADS
    v_blk0 = 2 * hq * dk // dv
    o_blk0 = v_blk0 + hq
    return pl.pallas_call(
        functools.partial(_mlstm_kernel, n_heads=hq, dk=dk),
        grid=(batch, hq, nc),
        in_specs=[pl.BlockSpec(memory_space=pltpu.SMEM),
                  pl.BlockSpec((CHUNK, dk), lambda b, h, c: (b * nc + c, h)),
                  pl.BlockSpec((CHUNK, dk), lambda b, h, c: (b * nc + c, hq + h)),
                  pl.BlockSpec((CHUNK, dv), lambda b, h, c: (b * nc + c, v_blk0 + h)),
                  pl.BlockSpec((CHUNK, dv), lambda b, h, c: (b * nc + c, o_blk0 + h)),
                  pl.BlockSpec((1, 2, CHUNK), lambda b, h, c: ((b * hq + h) * nc + c, 0, 0)),
                  pl.BlockSpec((1, CHUNK, 2), lambda b, h, c: ((b * hq + h) * nc + c, 0, 0)),
                  pl.BlockSpec((1, dv), lambda b, h, c: (0, h))],
        out_specs=pl.BlockSpec((CHUNK, dv), lambda b, h, c: (b * nc + c, h)),
        out_shape=jax.ShapeDtypeStruct((batch * seq, hq * dv), MXU_DTYPE),
        scratch_shapes=[pltpu.VMEM((dk, dv), _F32),
                        pltpu.VMEM((1, dk), _F32),
                        pltpu.VMEM((1, 1), _F32)],
        compiler_params=_params(("parallel", "parallel", "arbitrary")),
        name="mlstm_scan",
    )(gate_bias.astype(_F32), proj, proj, proj, proj, g_row, g_col,
      head_norm.reshape(1, hq * dv).astype(_F32))


def _mlstm_mixer(x, mix_g, w_in, gate_bias, head_norm, w_out, batch, seq):
    d_model = x.shape[1]
    dv = d_model // C_HEADS
    dk = dv // 2
    hq = C_HEADS
    h = _rmsnorm(x, mix_g, MXU_DTYPE)
    n_main = 2 * hq * dk + 2 * hq * dv
    w_gate = jnp.concatenate(
        [w_in[:, n_main:], jnp.zeros((w_in.shape[0], LANES - 2 * hq), w_in.dtype)], axis=1)
    proj = _matmul(h, w_in[:, :n_main].astype(MXU_DTYPE), MXU_DTYPE)
    gates = _matmul(h, w_gate.astype(MXU_DTYPE), _F32, bn=LANES)
    nc = seq // CHUNK
    g = gates[:, :2 * hq].reshape(batch, nc, CHUNK, 2, hq)
    g_row = g.transpose(0, 4, 1, 3, 2).reshape(batch * hq * nc, 2, CHUNK)
    g_col = g.transpose(0, 4, 1, 2, 3).reshape(batch * hq * nc, CHUNK, 2)
    y = _mlstm_scan(proj, g_row, g_col, gate_bias, head_norm, batch, seq, dk, dv)
    return _matmul_residual(y, w_out.astype(MXU_DTYPE), x, 1.0)


def kernel(x, positions, ffn1_norm, ffn1_w_in, ffn1_w_out, mix_norm, ffn2_norm, ffn2_w_in, ffn2_w_out,
           a_w_in, a_q_norm, a_kv_norm, a_w_uq, a_w_ukv, a_w_out,
           b_w_in, b_idx_k_norm, b_w_out,
           c_w_in, c_gate_bias, c_head_norm, c_w_out, final_norm):
    batch, seq, d_model = x.shape
    depth = ffn1_norm.shape[0]
    x = x.reshape(batch * seq, d_model)
    for i in range(depth):
        x = _ffn(x, ffn1_norm[i], ffn1_w_in[i], ffn1_w_out[i])
        kind, j = i % N_MIXERS, i // N_MIXERS
        if kind == 0:
            x = _mla_mixer(x, positions, mix_norm[i], a_w_in[j], a_q_norm[j], a_kv_norm[j],
                           a_w_uq[j], a_w_ukv[j], a_w_out[j], batch, seq)
        elif kind == 1:
            x = _dsa_mixer(x, positions, mix_norm[i], b_w_in[j], b_idx_k_norm[j], b_w_out[j], batch, seq)
        else:
            x = _mlstm_mixer(x, mix_norm[i], c_w_in[j], c_gate_bias[j], c_head_norm[j], c_w_out[j],
                             batch, seq)
        x = _ffn(x, ffn2_norm[i], ffn2_w_in[i], ffn2_w_out[i])
    return _rmsnorm(x, final_norm, _F32).reshape(batch, seq, d_model)
```

```python
import functools

import jax
import jax.numpy as jnp
import numpy as np
from jax import lax
from jax.experimental import pallas as pl
from jax.experimental.pallas import tpu as pltpu

CHUNK = 64
ROPE_THETA = 10000.0
NORM_EPS = 1e-6
D_FF = 6144

A_HEADS = 64
A_Q_LORA = 1024
A_KV_LORA = 512
A_NOPE = 128
A_ROPE = 64
A_V = 128

B_HEADS = 32
B_KV_HEADS = 8
B_HEAD_DIM = 128
B_IDX_HEADS = 32
B_IDX_DIM = 128
B_TOPK_MAX = 256

C_HEADS = 8
N_MIXERS = 3

LANES = 128
SUBLANES = 8
ONES_ROWS = 16
LOG2E = 1.4426950408889634
MXU_DTYPE = jnp.bfloat16
VMEM_LIMIT_MB = 56

ROW_TILE = 256
MM_BM = 512
MM_BN = 1024
MLA_TQ = 512
DSA_TQ = 256
DSA_TK = 256

_F32 = jnp.float32
_I32 = jnp.int32
_NEG_BIG = -1e30
_INT_MIN = -(2 ** 31)
_KEY_OF_NEG_INF = -2139095041


def _params(semantics):
    return pltpu.CompilerParams(dimension_semantics=semantics,
                                vmem_limit_bytes=VMEM_LIMIT_MB << 20)


def _rot_half(x):
    return pltpu.roll(x, LANES // 2, 1)


def _rmsnorm_kernel(x_ref, g_ref, o_ref):
    x = x_ref[...].astype(_F32)
    y = x * lax.rsqrt(jnp.mean(x * x, axis=-1, keepdims=True) + NORM_EPS)
    o_ref[...] = (y * g_ref[...]).astype(o_ref.dtype)


def _rmsnorm(x, g, out_dtype, col_block=0, width=None):
    t = x.shape[0]
    width = x.shape[1] if width is None else width
    bt = min(ROW_TILE, t)
    return pl.pallas_call(
        _rmsnorm_kernel,
        grid=(t // bt,),
        in_specs=[pl.BlockSpec((bt, width), lambda i: (i, col_block)),
                  pl.BlockSpec((1, width), lambda i: (0, 0))],
        out_specs=pl.BlockSpec((bt, width), lambda i: (i, 0)),
        out_shape=jax.ShapeDtypeStruct((t, width), out_dtype),
        compiler_params=_params(("parallel",)),
        name="rmsnorm",
    )(x, g.reshape(1, width).astype(_F32))


def _rope_kernel(x_ref, ra_ref, rb_ref, o_ref, *, n_blocks):
    ra = ra_ref[...]
    rb = rb_ref[...]
    for c in range(n_blocks):
        sl = slice(c * LANES, (c + 1) * LANES)
        x = x_ref[:, sl].astype(_F32)
        o_ref[:, sl] = (x * ra + _rot_half(x) * rb).astype(o_ref.dtype)


def _rope(x, ra, rb, out_dtype, col_block=0, width=None):
    t = x.shape[0]
    width = x.shape[1] if width is None else width
    bt = min(ROW_TILE, t)
    return pl.pallas_call(
        functools.partial(_rope_kernel, n_blocks=width // LANES),
        grid=(t // bt,),
        in_specs=[pl.BlockSpec((bt, width), lambda i: (i, col_block)),
                  pl.BlockSpec((bt, LANES), lambda i: (i, 0)),
                  pl.BlockSpec((bt, LANES), lambda i: (i, 0))],
        out_specs=pl.BlockSpec((bt, width), lambda i: (i, 0)),
        out_shape=jax.ShapeDtypeStruct((t, width), out_dtype),
        compiler_params=_params(("parallel",)),
        name="rope",
    )(x, ra, rb)


def _rmsnorm_rope_kernel(x_ref, g_ref, ra_ref, rb_ref, o_ref):
    x = x_ref[...].astype(_F32)
    y = x * lax.rsqrt(jnp.mean(x * x, axis=-1, keepdims=True) + NORM_EPS) * g_ref[...]
    o_ref[...] = (y * ra_ref[...] + _rot_half(y) * rb_ref[...]).astype(o_ref.dtype)


def _rmsnorm_rope(x, g, ra, rb, out_dtype, col_block=0):
    t = x.shape[0]
    bt = min(ROW_TILE, t)
    return pl.pallas_call(
        _rmsnorm_rope_kernel,
        grid=(t // bt,),
        in_specs=[pl.BlockSpec((bt, LANES), lambda i: (i, col_block)),
                  pl.BlockSpec((1, LANES), lambda i: (0, 0)),
                  pl.BlockSpec((bt, LANES), lambda i: (i, 0)),
                  pl.BlockSpec((bt, LANES), lambda i: (i, 0))],
        out_specs=pl.BlockSpec((bt, LANES), lambda i: (i, 0)),
        out_shape=jax.ShapeDtypeStruct((t, LANES), out_dtype),
        compiler_params=_params(("parallel",)),
        name="rmsnorm_rope",
    )(x, g.reshape(1, LANES).astype(_F32), ra, rb)


def _normed(x_ref, g_ref, h_ref):
    @pl.when(pl.program_id(1) == 0)
    def _normalise():
        x = x_ref[...].astype(_F32)
        y = x * lax.rsqrt(jnp.mean(x * x, axis=-1, keepdims=True) + NORM_EPS)
        h_ref[...] = (y * g_ref[...]).astype(h_ref.dtype)

    return h_ref[...]


def _mm_norm_kernel(x_ref, g_ref, w_ref, o_ref, h_ref):
    o_ref[...] = jnp.dot(_normed(x_ref, g_ref, h_ref), w_ref[...],
                         preferred_element_type=_F32).astype(o_ref.dtype)


def _mm_norm_swiglu_kernel(x_ref, g_ref, wg_ref, wu_ref, o_ref, h_ref):
    h = _normed(x_ref, g_ref, h_ref)
    g = jnp.dot(h, wg_ref[...], preferred_element_type=_F32)
    u = jnp.dot(h, wu_ref[...], preferred_element_type=_F32)
    o_ref[...] = (g * jax.nn.sigmoid(g) * u).astype(o_ref.dtype)


def _mm_norm_t_kernel(w_ref, x_ref, g_ref, o_ref, h_ref):
    o_ref[0] = lax.dot_general(w_ref[...], _normed(x_ref, g_ref, h_ref), (((1,), (1,)), ((), ())),
                               preferred_element_type=_F32).astype(o_ref.dtype)


def _mm_residual_kernel(x_ref, w_ref, r_ref, o_ref, *, scale):
    y = jnp.dot(x_ref[...], w_ref[...], preferred_element_type=_F32)
    o_ref[...] = r_ref[...] + scale * y


def _tile(n, pref):
    return pref if n % pref == 0 else n


def _matmul_norm(x, g, w, out_dtype, bm=None, bn=None, col_block=0):
    m = x.shape[0]
    k, n = w.shape
    bm = _tile(m, bm or MM_BM)
    bn = _tile(n, bn or MM_BN)
    return pl.pallas_call(
        _mm_norm_kernel,
        grid=(m // bm, n // bn),
        in_specs=[pl.BlockSpec((bm, k), lambda i, j: (i, col_block)),
                  pl.BlockSpec((1, k), lambda i, j: (0, 0)),
                  pl.BlockSpec((k, bn), lambda i, j: (0, j))],
        out_specs=pl.BlockSpec((bm, bn), lambda i, j: (i, j)),
        out_shape=jax.ShapeDtypeStruct((m, n), out_dtype),
        scratch_shapes=[pltpu.VMEM((bm, k), MXU_DTYPE)],
        compiler_params=_params(("parallel", "arbitrary")),
        name="matmul_norm",
    )(x, g.reshape(1, k).astype(_F32), w)


def _matmul_norm_t(w_t, x, g, tk, out_dtype, bn=None, col_block=0):
    n, k = w_t.shape
    t = x.shape[0]
    bn = _tile(n, bn or MM_BN)
    return pl.pallas_call(
        _mm_norm_t_kernel,
        grid=(t // tk, n // bn),
        in_specs=[pl.BlockSpec((bn, k), lambda i, j: (j, 0)),
                  pl.BlockSpec((tk, k), lambda i, j: (i, col_block)),
                  pl.BlockSpec((1, k), lambda i, j: (0, 0))],
        out_specs=pl.BlockSpec((1, bn, tk), lambda i, j: (i, j, 0)),
        out_shape=jax.ShapeDtypeStruct((t // tk, n, tk), out_dtype),
        scratch_shapes=[pltpu.VMEM((tk, k), MXU_DTYPE)],
        compiler_params=_params(("parallel", "arbitrary")),
        name="matmul_norm_t",
    )(w_t, x, g.reshape(1, k).astype(_F32))


def _matmul_norm_swiglu(x, g, w_in, bm=None, bn=512):
    m, k = x.shape
    f = w_in.shape[1] // 2
    bm = _tile(m, bm or MM_BM)
    bn = _tile(f, bn)
    nb = f // bn
    return pl.pallas_call(
        _mm_norm_swiglu_kernel,
        grid=(m // bm, nb),
        in_specs=[pl.BlockSpec((bm, k), lambda i, j: (i, 0)),
                  pl.BlockSpec((1, k), lambda i, j: (0, 0)),
                  pl.BlockSpec((k, bn), lambda i, j: (0, j)),
                  pl.BlockSpec((k, bn), lambda i, j: (0, j + nb))],
        out_specs=pl.BlockSpec((bm, bn), lambda i, j: (i, j)),
        out_shape=jax.ShapeDtypeStruct((m, f), MXU_DTYPE),
        scratch_shapes=[pltpu.VMEM((bm, k), MXU_DTYPE)],
        compiler_params=_params(("parallel", "arbitrary")),
        name="matmul_norm_swiglu",
    )(x, g.reshape(1, k).astype(_F32), w_in, w_in)


def _matmul_residual(x, w, res, scale, bm=None, bn=512):
    m, k = x.shape
    n = w.shape[1]
    bm = _tile(m, bm or MM_BM)
    bn = _tile(n, bn)
    return pl.pallas_call(
        functools.partial(_mm_residual_kernel, scale=scale),
        grid=(m // bm, n // bn),
        in_specs=[pl.BlockSpec((bm, k), lambda i, j: (i, 0)),
                  pl.BlockSpec((k, bn), lambda i, j: (0, j)),
                  pl.BlockSpec((bm, bn), lambda i, j: (i, j))],
        out_specs=pl.BlockSpec((bm, bn), lambda i, j: (i, j)),
        out_shape=jax.ShapeDtypeStruct((m, n), _F32),
        compiler_params=_params(("parallel", "arbitrary")),
        name="matmul_residual",
    )(x, w, res)


def _ffn(x, norm_g, w_in, w_out):
    act = _matmul_norm_swiglu(x, norm_g, w_in.astype(MXU_DTYPE))
    return _matmul_residual(act, w_out.astype(MXU_DTYPE), x, 0.5)


def _rope_tables(positions, d, half_width):
    inv = jnp.power(ROPE_THETA, -jnp.arange(0, d, 2, dtype=_F32) / d)
    ang = positions.astype(_F32).reshape(-1, 1) * inv
    pad = jnp.zeros((ang.shape[0], half_width - d // 2), _F32)
    cos, sin = jnp.cos(ang), jnp.sin(ang)
    ra = jnp.concatenate([cos, pad, cos, pad], axis=1)
    rb = jnp.concatenate([-sin, pad, sin, pad], axis=1)
    return ra, rb


def _softmax_step(s, vt, m_ref, acc_ref):
    m_prev = m_ref[...]
    m_new = jnp.maximum(m_prev, jnp.max(s, axis=0, keepdims=True))
    alpha = jnp.exp2(m_prev - m_new)
    p = jnp.exp2(s - m_new).astype(MXU_DTYPE)
    vt1 = jnp.concatenate([vt, jnp.ones((ONES_ROWS, vt.shape[1]), vt.dtype)], axis=0)
    acc_ref[...] = alpha * acc_ref[...] + jnp.dot(vt1, p, preferred_element_type=_F32)
    m_ref[...] = m_new


def _pipelined_sweep(n, score, update, sa_ref, sb_ref):
    sa_ref[...] = score(0, True)

    def pair(t):
        sb_ref[...] = score(t + 1, False)
        update(t, sa_ref[...])
        sa_ref[...] = score(t + 2, False)
        update(t + 1, sb_ref[...])

    def quad_body(p, carry):
        pair(4 * p)
        pair(4 * p + 2)
        return carry

    n_quads = (n - 1) // 4
    lax.fori_loop(0, n_quads, quad_body, 0)

    def pair_body(p, carry):
        pair(4 * n_quads + 2 * p)
        return carry

    n_pairs = (n - 1 - 4 * n_quads) // 2
    lax.fori_loop(0, n_pairs, pair_body, 0)
    t = 4 * n_quads + 2 * n_pairs

    @pl.when(t == n - 1)
    def _odd_tail():
        update(t, sa_ref[...])

    @pl.when(t < n - 1)
    def _even_tail():
        sb_ref[...] = score(t + 1, False)
        update(t, sa_ref[...])
        update(t + 1, sb_ref[...])


def _static_sweep(n, score, update, sa_ref, sb_ref, unroll=4):
    sa_ref[...] = score(0)
    bufs = (sa_ref, sb_ref)

    def run(base, count, last):
        for u in range(count):
            cur, nxt = bufs[u % 2], bufs[(u + 1) % 2]
            if not (last and u == count - 1):
                nxt[...] = score(base + u + 1)
            update(base + u, cur[...])

    n_loop = (n - 1) // unroll * unroll

    def body(p, carry):
        run(p * unroll, unroll, False)
        return carry

    lax.fori_loop(0, n_loop // unroll, body, 0)
    run(n_loop, n - n_loop, True)


def _softmax_finish(acc_ref, dv):
    acc = acc_ref[...]
    return acc[:dv, :] / acc[dv:dv + 1, :]


def _mla_attn_kernel(jt_ref, it_ref, q_ref, kn_ref, kr_ref, vt_ref, cos_ref, sin_ref, o_ref,
                     qs_ref, m_ref, acc_ref, sa_ref, sb_ref, *, nq, tq, n_off):
    half = A_ROPE // 2
    r1 = slice(A_NOPE, A_NOPE + half)
    r2 = slice(A_NOPE + LANES // 2, A_NOPE + LANES // 2 + half)

    def prepare(i, carry):
        qt = q_ref[i]
        x1 = qt[r1, :].astype(_F32)
        x2 = qt[r2, :].astype(_F32)
        cos = cos_ref[i]
        sin = sin_ref[i]
        qs_ref[i] = qt
        qs_ref[i, r1, :] = (x1 * cos - x2 * sin).astype(qs_ref.dtype)
        qs_ref[i, r2, :] = (x1 * sin + x2 * cos).astype(qs_ref.dtype)
        m_ref[i] = jnp.full((1, tq), -jnp.inf, _F32)
        acc_ref[i] = jnp.zeros(acc_ref.shape[1:], _F32)
        return carry

    lax.fori_loop(0, nq, prepare, 0)

    def scores(j, i):
        off = pl.multiple_of(j * tq, tq)
        kc = jnp.concatenate([kn_ref[pl.ds(off, tq), :], kr_ref[pl.ds(off, tq), :]], axis=1)
        return jnp.dot(kc, qs_ref[i], preferred_element_type=_F32)

    def update(j, i, s):
        _softmax_step(s, vt_ref[j], m_ref.at[i], acc_ref.at[i])

    kchunk = lax.broadcasted_iota(_I32, (tq, tq), 0) // CHUNK
    qchunk = lax.broadcasted_iota(_I32, (tq, tq), 1) // CHUNK
    _static_sweep(nq, lambda t: jnp.where(qchunk >= kchunk, scores(t, t), -jnp.inf),
                  lambda t, s: update(t, t, s), sa_ref, sb_ref)
    if n_off:
        _static_sweep(n_off, lambda t: scores(jt_ref[t], it_ref[t]),
                      lambda t, s: update(jt_ref[t], it_ref[t], s), sa_ref, sb_ref)

    def finish(i, carry):
        off = pl.multiple_of(i * tq, tq)
        o_ref[pl.ds(off, tq), :] = _softmax_finish(acc_ref.at[i], A_V).T.astype(o_ref.dtype)
        return carry

    lax.fori_loop(0, nq, finish, 0)


def _mla_attention(qt, kn, kr, vt, cos_t, sin_t, batch, seq, tq):
    nq = seq // tq
    pairs = [(j, i) for j in range(nq) for i in range(j + 1, nq)]
    jt = jnp.asarray([p[0] for p in pairs] or [0], _I32)
    it = jnp.asarray([p[1] for p in pairs] or [0], _I32)
    half = A_ROPE // 2
    smem = pl.BlockSpec(memory_space=pltpu.SMEM)
    return pl.pallas_call(
        functools.partial(_mla_attn_kernel, nq=nq, tq=tq, n_off=len(pairs)),
        grid=(batch, A_HEADS),
        in_specs=[smem, smem,
                  pl.BlockSpec((nq, 2 * LANES, tq), lambda b, h: (b, h, 0)),
                  pl.BlockSpec((seq, LANES), lambda b, h: (b, h)),
                  pl.BlockSpec((seq, LANES), lambda b, h: (b, 0)),
                  pl.BlockSpec((nq, A_V, tq), lambda b, h: (b, h, 0)),
                  pl.BlockSpec((nq, half, tq), lambda b, h: (b, 0, 0)),
                  pl.BlockSpec((nq, half, tq), lambda b, h: (b, 0, 0))],
        out_specs=pl.BlockSpec((seq, A_V), lambda b, h: (b, h)),
        out_shape=jax.ShapeDtypeStruct((batch * seq, A_HEADS * A_V), MXU_DTYPE),
        scratch_shapes=[pltpu.VMEM((nq, 2 * LANES, tq), MXU_DTYPE),
                        pltpu.VMEM((nq, 1, tq), _F32),
                        pltpu.VMEM((nq, A_V + ONES_ROWS, tq), _F32),
                        pltpu.VMEM((tq, tq), _F32),
                        pltpu.VMEM((tq, tq), _F32)],
        compiler_params=_params(("parallel", "parallel")),
        name="mla_attention",
    )(jt, it, qt, kn, kr, vt, cos_t, sin_t)


def _pad_rope_cols(w, n_heads, nope):
    k = w.shape[0]
    w = w.reshape(k, n_heads, nope + A_ROPE)
    half = A_ROPE // 2
    z = jnp.zeros((k, n_heads, LANES // 2 - half), w.dtype)
    w = jnp.concatenate([w[..., :nope], w[..., nope:nope + half], z, w[..., nope + half:], z], axis=-1)
    return w.reshape(k, n_heads * (nope + LANES))


def _mla_mixer(x, positions, mix_g, w_in, q_norm, kv_norm, w_uq, w_ukv, w_out, batch, seq):
    assert A_NOPE == LANES and A_V == LANES and A_ROPE == LANES // 2
    tq = min(MLA_TQ, seq)
    n_lat = A_Q_LORA + A_KV_LORA
    w_in_p = jnp.concatenate([w_in[:, :n_lat], _pad_rope_cols(w_in[:, n_lat:], 1, 0)], axis=1)
    lat = _matmul_norm(x, mix_g, w_in_p.astype(MXU_DTYPE), _F32, bm=MM_BM // 2,
                       bn=w_in_p.shape[1])
    ra, rb = _rope_tables(positions, A_ROPE, LANES // 2)
    kr = _rope(lat, ra, rb, MXU_DTYPE, col_block=n_lat // LANES, width=LANES)
    qscale = (A_NOPE + A_ROPE) ** -0.5 * LOG2E
    w_qt = (_pad_rope_cols(w_uq, A_HEADS, A_NOPE) * qscale).T.astype(MXU_DTYPE)
    qt = _matmul_norm_t(w_qt, lat, q_norm, tq, MXU_DTYPE)
    w_ukv = w_ukv.reshape(A_KV_LORA, A_HEADS, A_NOPE + A_V)
    w_kn = w_ukv[..., :A_NOPE].reshape(A_KV_LORA, A_HEADS * A_NOPE)
    w_vt = w_ukv[..., A_NOPE:].reshape(A_KV_LORA, A_HEADS * A_V).T
    kv_blk = A_Q_LORA // A_KV_LORA
    kn = _matmul_norm(lat, kv_norm, w_kn.astype(MXU_DTYPE), MXU_DTYPE, col_block=kv_blk)
    vt = _matmul_norm_t(w_vt.astype(MXU_DTYPE), lat, kv_norm, tq, MXU_DTYPE,
                        col_block=kv_blk)
    half = A_ROPE // 2
    cos_t = ra[:, :half].reshape(-1, tq, half).transpose(0, 2, 1)
    sin_t = rb[:, LANES // 2:LANES // 2 + half].reshape(-1, tq, half).transpose(0, 2, 1)
    o = _mla_attention(qt, kn, kr, vt, cos_t, sin_t, batch, seq, tq)
    return _matmul_residual(o, w_out.astype(MXU_DTYPE), x, 1.0)


def _dsa_kernel(qi_ref, wit_ref, ki_ref, ra_ref, rb_ref, q_ref, k_ref, vt_ref, o_ref,
                qit_ref, key_ref, thr_ref, qt_ref, m_ref, acc_ref, sa_ref, sb_ref,
                *, tq, tk, topk, group, wi_scale, qscale):
    i = pl.program_id(1)
    h = pl.program_id(2)
    nkb = (i + 1) * (tq // tk)
    ra = ra_ref[...]
    rb = rb_ref[...]

    @pl.when(h == 0)
    def _select():
        for hd in range(B_IDX_HEADS):
            sl = slice(hd * LANES, (hd + 1) * LANES)
            x = qi_ref[:, sl].astype(_F32)
            x = x * ra + _rot_half(x) * rb
            qit_ref[sl, :] = x.T.astype(qit_ref.dtype)
        wit = wit_ref[...] * wi_scale

        def score_tile(j, carry):
            off = pl.multiple_of(j * tk, tk)
            ki = ki_ref[pl.ds(off, tk), :]
            sc = jnp.zeros((tk, tq), _F32)
            for hd in range(B_IDX_HEADS):
                d = jnp.dot(ki, qit_ref[hd * LANES:(hd + 1) * LANES, :],
                            preferred_element_type=_F32)
                sc = sc + jnp.maximum(d, 0.0) * wit[hd:hd + 1, :]
            kchunk = (off + lax.broadcasted_iota(_I32, (tk, tq), 0)) // CHUNK
            qchunk = (i * tq + lax.broadcasted_iota(_I32, (tk, tq), 1)) // CHUNK
            sc = jnp.where(qchunk >= kchunk, sc, -jnp.inf)
            bits = pltpu.bitcast(sc, _I32)
            key_ref[j] = jnp.where(bits >= 0, bits, bits ^ 0x7FFFFFFF)
            return carry

        lax.fori_loop(0, nkb, score_tile, 0)

        def count(pred):
            def tile(j, acc):
                hit = jnp.where(pred(key_ref[j]), 1.0, 0.0)
                return acc + jnp.sum(hit.reshape(tk // SUBLANES, SUBLANES, tq), axis=0)

            acc = lax.fori_loop(0, nkb, tile, jnp.zeros((SUBLANES, tq), _F32))
            return jnp.sum(acc, axis=0, keepdims=True)

        def bisect(b, t):
            cand = t + jnp.left_shift(jnp.int32(1), 31 - b)
            return jnp.where(count(lambda kt: kt >= cand) >= topk, cand, t)

        t = lax.fori_loop(0, 32, bisect, jnp.full((1, tq), _INT_MIN, _I32))

        need = topk - count(lambda kt: kt > t)
        lower = jnp.where(lax.broadcasted_iota(_I32, (tk, tk), 0)
                          >= lax.broadcasted_iota(_I32, (tk, tk), 1), 1.0, 0.0).astype(MXU_DTYPE)

        def demote_tile(j, seen):
            kt = key_ref[j]
            eq = kt == t
            rank = seen + jnp.dot(lower, jnp.where(eq, 1.0, 0.0).astype(MXU_DTYPE),
                                  preferred_element_type=_F32)
            key_ref[j] = jnp.where(eq, jnp.where(rank > need, t - 1, kt), kt)
            return rank[tk - 1:tk, :]

        lax.fori_loop(0, nkb, demote_tile, jnp.zeros((1, tq), _F32))
        thr_ref[...] = jnp.maximum(t, _KEY_OF_NEG_INF + 1)

    q = q_ref[...]
    for g in range(group):
        x = q[:, g * LANES:(g + 1) * LANES].astype(_F32)
        x = (x * ra + _rot_half(x) * rb) * qscale
        qt_ref[:, g * tq:(g + 1) * tq] = x.T.astype(qt_ref.dtype)
    m_ref[...] = jnp.full(m_ref.shape, _NEG_BIG, _F32)
    acc_ref[...] = jnp.zeros(acc_ref.shape, _F32)
    thr = thr_ref[...]

    def score(j, first):
        off = pl.multiple_of(j * tk, tk)
        bias = jnp.where(key_ref[j] >= thr, 0.0, _NEG_BIG)
        bias = jnp.concatenate([bias] * group, axis=1)
        return jnp.dot(k_ref[pl.ds(off, tk), :], qt_ref[...], preferred_element_type=_F32) + bias

    def update(j, s):
        _softmax_step(s, vt_ref[j], m_ref, acc_ref)

    _pipelined_sweep(nkb, score, update, sa_ref, sb_ref)
    out = _softmax_finish(acc_ref, B_HEAD_DIM)
    for g in range(group):
        o_ref[:, g * LANES:(g + 1) * LANES] = out[:, g * tq:(g + 1) * tq].T.astype(o_ref.dtype)


def _dsa_attention(proj, k_roped, vt, ki, wit, ra, rb, batch, seq, tq, tk):
    nq = seq // tq
    group = B_HEADS // B_KV_HEADS
    topk = min(B_TOPK_MAX, seq // 4)
    qi_w = B_IDX_HEADS * B_IDX_DIM
    q_w = group * B_HEAD_DIM
    q_blk0 = qi_w // q_w
    kern = functools.partial(
        _dsa_kernel, tq=tq, tk=tk, topk=float(topk), group=group,
        wi_scale=B_IDX_HEADS ** -0.5 * B_IDX_DIM ** -0.5, qscale=B_HEAD_DIM ** -0.5 * LOG2E)
    return pl.pallas_call(
        kern,
        grid=(batch, nq, B_KV_HEADS),
        in_specs=[pl.BlockSpec((tq, qi_w), lambda b, i, h: (b * nq + i, 0)),
                  pl.BlockSpec((B_IDX_HEADS, tq), lambda b, i, h: (0, b * nq + i)),
                  pl.BlockSpec((seq, LANES), lambda b, i, h: (b, 0)),
                  pl.BlockSpec((tq, LANES), lambda b, i, h: (b * nq + i, 0)),
                  pl.BlockSpec((tq, LANES), lambda b, i, h: (b * nq + i, 0)),
                  pl.BlockSpec((tq, q_w), lambda b, i, h: (b * nq + i, q_blk0 + h)),
                  pl.BlockSpec((seq, B_HEAD_DIM), lambda b, i, h: (b, h)),
                  pl.BlockSpec((seq // tk, B_HEAD_DIM, tk), lambda b, i, h: (b, h, 0))],
        out_specs=pl.BlockSpec((tq, q_w), lambda b, i, h: (b * nq + i, h)),
        out_shape=jax.ShapeDtypeStruct((batch * seq, B_HEADS * B_HEAD_DIM), MXU_DTYPE),
        scratch_shapes=[pltpu.VMEM((qi_w, tq), MXU_DTYPE),
                        pltpu.VMEM((seq // tk, tk, tq), _I32),
                        pltpu.VMEM((1, tq), _I32),
                        pltpu.VMEM((B_HEAD_DIM, group * tq), MXU_DTYPE),
                        pltpu.VMEM((1, group * tq), _F32),
                        pltpu.VMEM((B_HEAD_DIM + ONES_ROWS, group * tq), _F32),
                        pltpu.VMEM((tk, group * tq), _F32),
                        pltpu.VMEM((tk, group * tq), _F32)],
        compiler_params=_params(("parallel", "arbitrary", "arbitrary")),
        name="dsa_attention",
    )(proj, wit, ki, ra, rb, proj, k_roped, vt)


def _dsa_mixer(x, positions, mix_g, w_in, idx_k_norm, w_out, batch, seq):
    assert B_HEAD_DIM == LANES and B_IDX_DIM == LANES and B_IDX_HEADS % SUBLANES == 0
    tq = min(DSA_TQ, seq)
    tk = min(DSA_TK, tq)
    nq, nk = B_HEADS * B_HEAD_DIM, B_KV_HEADS * B_HEAD_DIM
    nqi = B_IDX_HEADS * B_IDX_DIM
    o_q, o_k, o_v, o_qi, o_ki, o_wi = np.cumsum([0, nq, nk, nk, nqi, B_IDX_DIM])
    w_main = jnp.concatenate([w_in[:, o_qi:o_ki], w_in[:, o_q:o_v]], axis=1)
    w_side = jnp.concatenate(
        [w_in[:, o_ki:o_wi], w_in[:, o_wi:],
         jnp.zeros((w_in.shape[0], LANES - B_IDX_HEADS), w_in.dtype)], axis=1)
    proj = _matmul_norm(x, mix_g, w_main.astype(MXU_DTYPE), MXU_DTYPE, bn=512)
    side = _matmul_norm(x, mix_g, w_side.astype(MXU_DTYPE), _F32, bn=2 * LANES)
    vt = _matmul_norm_t(w_in[:, o_v:o_qi].T.astype(MXU_DTYPE), x, mix_g, tk, MXU_DTYPE)
    ra, rb = _rope_tables(positions, B_HEAD_DIM, LANES // 2)
    k_roped = _rope(proj, ra, rb, MXU_DTYPE, col_block=(nqi + nq) // nk, width=nk)
    ki = _rmsnorm_rope(side, idx_k_norm, ra, rb, MXU_DTYPE, col_block=0)
    wit = side[:, LANES:LANES + B_IDX_HEADS].T
    o = _dsa_attention(proj, k_roped, vt, ki, wit, ra, rb, batch, seq, tq, tk)
    return _matmul_residual(o, w_out.astype(MXU_DTYPE), x, 1.0)


def _log_sigmoid(x):
    return jnp.minimum(x, 0.0) - jnp.log(1.0 + jnp.exp(-jnp.abs(x)))


def _mlstm_kernel(bias_ref, q_ref, k_ref, v_ref, o_ref, grow_ref, gcol_ref, hn_ref, y_ref,
                  c_ref, n_ref, m_ref, *, n_heads, dk):
    h = pl.program_id(1)
    c = pl.program_id(2)

    @pl.when(c == 0)
    def _init():
        c_ref[...] = jnp.zeros(c_ref.shape, _F32)
        n_ref[...] = jnp.zeros(n_ref.shape, _F32)
        m_ref[...] = jnp.zeros(m_ref.shape, _F32)

    b_i = bias_ref[h]
    b_f = bias_ref[n_heads + h]
    grow = grow_ref[0]
    gcol = gcol_ref[0]
    li_row = grow[0:1, :] + b_i
    lf_row = _log_sigmoid(grow[1:2, :] + b_f)
    li_col = gcol[:, 0:1] + b_i
    lf_col = _log_sigmoid(gcol[:, 1:2] + b_f)

    rows = lax.broadcasted_iota(_I32, (CHUNK, CHUNK), 0)
    cols = lax.broadcasted_iota(_I32, (CHUNK, CHUNK), 1)
    tril = rows >= cols
    b_col = jnp.sum(jnp.where(tril, lf_row, 0.0), axis=1, keepdims=True)
    b_row = jnp.sum(jnp.where(rows <= cols, lf_col, 0.0), axis=0, keepdims=True)
    b_end = jnp.sum(lf_row, axis=1, keepdims=True)

    m_prev = m_ref[...]
    a = b_col + m_prev
    d = jnp.where(tril, b_col - b_row + li_row, -jnp.inf)
    m_t = jnp.maximum(a, jnp.max(d, axis=1, keepdims=True))
    inter = jnp.exp(a - m_t)

    q = q_ref[...]
    k = k_ref[...]
    v = v_ref[...]
    qf = q.astype(_F32) * dk ** -0.5
    qs = qf.astype(MXU_DTYPE)
    qk = lax.dot_general(qs, k, (((1,), (1,)), ((), ())),
                         preferred_element_type=_F32) * jnp.exp(d - m_t)
    ct = c_ref[...]
    num = inter * jnp.dot(qs, ct.astype(MXU_DTYPE), preferred_element_type=_F32) + jnp.dot(
        qk.astype(MXU_DTYPE), v, preferred_element_type=_F32)
    n_row = n_ref[...]
    den = inter * jnp.sum(qf * n_row, axis=1, keepdims=True) + jnp.sum(qk, axis=1, keepdims=True)
    hs = num / jnp.maximum(jnp.abs(den), jnp.exp(-m_t))

    g_col = b_end - b_col + li_col
    m_new = jnp.maximum(b_end + m_prev, jnp.max(g_col, axis=0, keepdims=True))
    decay = jnp.exp(b_end + m_prev - m_new)
    wk = jnp.exp(g_col - m_new)
    vw = (v.astype(_F32) * wk).astype(MXU_DTYPE)
    c_ref[...] = decay * ct + lax.dot_general(k, vw, (((0,), (0,)), ((), ())),
                                              preferred_element_type=_F32)
    n_ref[...] = decay * n_row + jnp.sum(wk * k.astype(_F32), axis=0, keepdims=True)
    m_ref[...] = m_new

    hs = hs * lax.rsqrt(jnp.mean(hs * hs, axis=1, keepdims=True) + NORM_EPS)
    y_ref[...] = (hs * hn_ref[...] * jax.nn.sigmoid(o_ref[...].astype(_F32))).astype(y_ref.dtype)


def _mlstm_scan(proj, g_row, g_col, gate_bias, head_norm, batch, seq, dk, dv):
    nc = seq // CHUNK
    hq = C_HEADS
    v_blk0 = 2 * hq * dk // dv
    o_blk0 = v_blk0 + hq
    return pl.pallas_call(
        functools.partial(_mlstm_kernel, n_heads=hq, dk=dk),
        grid=(batch, hq, nc),
        in_specs=[pl.BlockSpec(memory_space=pltpu.SMEM),
                  pl.BlockSpec((CHUNK, dk), lambda b, h, c: (b * nc + c, h)),
                  pl.BlockSpec((CHUNK, dk), lambda b, h, c: (b * nc + c, hq + h)),
                  pl.BlockSpec((CHUNK, dv), lambda b, h, c: (b * nc + c, v_blk0 + h)),
                  pl.BlockSpec((CHUNK, dv), lambda b, h, c: (b * nc + c, o_blk0 + h)),
                  pl.BlockSpec((1, 2, CHUNK), lambda b, h, c: ((b * hq + h) * nc + c, 0, 0)),
                  pl.BlockSpec((1, CHUNK, 2), lambda b, h, c: ((b * hq + h) * nc + c, 0, 0)),
                  pl.BlockSpec((1, dv), lambda b, h, c: (0, h))],
        out_specs=pl.BlockSpec((CHUNK, dv), lambda b, h, c: (b * nc + c, h)),
        out_shape=jax.ShapeDtypeStruct((batch * seq, hq * dv), MXU_DTYPE),
        scratch_shapes=[pltpu.VMEM((dk, dv), _F32),
                        pltpu.VMEM((1, dk), _F32),
                        pltpu.VMEM((1, 1), _F32)],
        compiler_params=_params(("parallel", "parallel", "arbitrary")),
        name="mlstm_scan",
    )(gate_bias.astype(_F32), proj, proj, proj, proj, g_row, g_col,
      head_norm.reshape(1, hq * dv).astype(_F32))


def _mlstm_mixer(x, mix_g, w_in, gate_bias, head_norm, w_out, batch, seq):
    d_model = x.shape[1]
    dv = d_model // C_HEADS
    dk = dv // 2
    hq = C_HEADS
    n_main = 2 * hq * dk + 2 * hq * dv
    w_gate = jnp.concatenate(
        [w_in[:, n_main:], jnp.zeros((w_in.shape[0], LANES - 2 * hq), w_in.dtype)], axis=1)
    proj = _matmul_norm(x, mix_g, w_in[:, :n_main].astype(MXU_DTYPE), MXU_DTYPE)
    gates = _matmul_norm(x, mix_g, w_gate.astype(MXU_DTYPE), _F32, bn=LANES)
    nc = seq // CHUNK
    g = gates[:, :2 * hq].reshape(batch, nc, CHUNK, 2, hq)
    g_row = g.transpose(0, 4, 1, 3, 2).reshape(batch * hq * nc, 2, CHUNK)
    g_col = g.transpose(0, 4, 1, 2, 3).reshape(batch * hq * nc, CHUNK, 2)
    y = _mlstm_scan(proj, g_row, g_col, gate_bias, head_norm, batch, seq, dk, dv)
    return _matmul_residual(y, w_out.astype(MXU_DTYPE), x, 1.0)


def kernel(x, positions, ffn1_norm, ffn1_w_in, ffn1_w_out, mix_norm, ffn2_norm, ffn2_w_in, ffn2_w_out,
           a_w_in, a_q_norm, a_kv_norm, a_w_uq, a_w_ukv, a_w_out,
           b_w_in, b_idx_k_norm, b_w_out,
           c_w_in, c_gate_bias, c_head_norm, c_w_out, final_norm):
    batch, seq, d_model = x.shape
    depth = ffn1_norm.shape[0]
    x = x.reshape(batch * seq, d_model)
    for i in range(depth):
        x = _ffn(x, ffn1_norm[i], ffn1_w_in[i], ffn1_w_out[i])
        kind, j = i % N_MIXERS, i // N_MIXERS
        if kind == 0:
            x = _mla_mixer(x, positions, mix_norm[i], a_w_in[j], a_q_norm[j], a_kv_norm[j],
                           a_w_uq[j], a_w_ukv[j], a_w_out[j], batch, seq)
        elif kind == 1:
            x = _dsa_mixer(x, positions, mix_norm[i], b_w_in[j], b_idx_k_norm[j], b_w_out[j], batch, seq)
        else:
            x = _mlstm_mixer(x, mix_norm[i], c_w_in[j], c_gate_bias[j], c_head_norm[j], c_w_out[j],
                             batch, seq)
        x = _ffn(x, ffn2_norm[i], ffn2_w_in[i], ffn2_w_out[i])
    return _rmsnorm(x, final_norm, _F32).reshape(batch, seq, d_model)
```

```python
import functools

import jax
import jax.numpy as jnp
import numpy as np
from jax import lax
from jax.experimental import pallas as pl
from jax.experimental.pallas import tpu as pltpu

CHUNK = 64
ROPE_THETA = 10000.0
NORM_EPS = 1e-6
D_FF = 6144

A_HEADS = 64
A_Q_LORA = 1024
A_KV_LORA = 512
A_NOPE = 128
A_ROPE = 64
A_V = 128

B_HEADS = 32
B_KV_HEADS = 8
B_HEAD_DIM = 128
B_IDX_HEADS = 32
B_IDX_DIM = 128
B_TOPK_MAX = 256

C_HEADS = 8
N_MIXERS = 3

LANES = 128
SUBLANES = 8
ONES_ROWS = 16
LOG2E = 1.4426950408889634
MXU_DTYPE = jnp.bfloat16
VMEM_LIMIT_MB = 56

ROW_TILE = 256
MM_BM = 512
MM_BN = 1024
MM_RESIDUAL_VMEM_MB = 48
MLA_TQ = 512
DSA_TQ = 256
DSA_TK = 256

_F32 = jnp.float32
_I32 = jnp.int32
_NEG_BIG = -1e30
_INT_MIN = -(2 ** 31)
_KEY_OF_NEG_INF = -2139095041


def _params(semantics):
    return pltpu.CompilerParams(dimension_semantics=semantics,
                                vmem_limit_bytes=VMEM_LIMIT_MB << 20)


def _rot_half(x):
    return pltpu.roll(x, LANES // 2, 1)


def _rmsnorm_kernel(x_ref, g_ref, o_ref):
    x = x_ref[...].astype(_F32)
    y = x * lax.rsqrt(jnp.mean(x * x, axis=-1, keepdims=True) + NORM_EPS)
    o_ref[...] = (y * g_ref[...]).astype(o_ref.dtype)


def _rmsnorm(x, g, out_dtype, col_block=0, width=None):
    t = x.shape[0]
    width = x.shape[1] if width is None else width
    bt = min(ROW_TILE, t)
    return pl.pallas_call(
        _rmsnorm_kernel,
        grid=(t // bt,),
        in_specs=[pl.BlockSpec((bt, width), lambda i: (i, col_block)),
                  pl.BlockSpec((1, width), lambda i: (0, 0))],
        out_specs=pl.BlockSpec((bt, width), lambda i: (i, 0)),
        out_shape=jax.ShapeDtypeStruct((t, width), out_dtype),
        compiler_params=_params(("parallel",)),
        name="rmsnorm",
    )(x, g.reshape(1, width).astype(_F32))


def _rope_kernel(x_ref, ra_ref, rb_ref, o_ref, *, n_blocks):
    ra = ra_ref[...]
    rb = rb_ref[...]
    for c in range(n_blocks):
        sl = slice(c * LANES, (c + 1) * LANES)
        x = x_ref[:, sl].astype(_F32)
        o_ref[:, sl] = (x * ra + _rot_half(x) * rb).astype(o_ref.dtype)


def _rope(x, ra, rb, out_dtype, col_block=0, width=None):
    t = x.shape[0]
    width = x.shape[1] if width is None else width
    bt = min(ROW_TILE, t)
    return pl.pallas_call(
        functools.partial(_rope_kernel, n_blocks=width // LANES),
        grid=(t // bt,),
        in_specs=[pl.BlockSpec((bt, width), lambda i: (i, col_block)),
                  pl.BlockSpec((bt, LANES), lambda i: (i, 0)),
                  pl.BlockSpec((bt, LANES), lambda i: (i, 0))],
        out_specs=pl.BlockSpec((bt, width), lambda i: (i, 0)),
        out_shape=jax.ShapeDtypeStruct((t, width), out_dtype),
        compiler_params=_params(("parallel",)),
        name="rope",
    )(x, ra, rb)


def _rmsnorm_rope_kernel(x_ref, g_ref, ra_ref, rb_ref, o_ref):
    x = x_ref[...].astype(_F32)
    y = x * lax.rsqrt(jnp.mean(x * x, axis=-1, keepdims=True) + NORM_EPS) * g_ref[...]
    o_ref[...] = (y * ra_ref[...] + _rot_half(y) * rb_ref[...]).astype(o_ref.dtype)


def _rmsnorm_rope(x, g, ra, rb, out_dtype, col_block=0):
    t = x.shape[0]
    bt = min(ROW_TILE, t)
    return pl.pallas_call(
        _rmsnorm_rope_kernel,
        grid=(t // bt,),
        in_specs=[pl.BlockSpec((bt, LANES), lambda i: (i, col_block)),
                  pl.BlockSpec((1, LANES), lambda i: (0, 0)),
                  pl.BlockSpec((bt, LANES), lambda i: (i, 0)),
                  pl.BlockSpec((bt, LANES), lambda i: (i, 0))],
        out_specs=pl.BlockSpec((bt, LANES), lambda i: (i, 0)),
        out_shape=jax.ShapeDtypeStruct((t, LANES), out_dtype),
        compiler_params=_params(("parallel",)),
        name="rmsnorm_rope",
    )(x, g.reshape(1, LANES).astype(_F32), ra, rb)


def _normed(x_ref, g_ref, h_ref):
    @pl.when(pl.program_id(1) == 0)
    def _normalise():
        x = x_ref[...].astype(_F32)
        y = x * lax.rsqrt(jnp.mean(x * x, axis=-1, keepdims=True) + NORM_EPS)
        h_ref[...] = (y * g_ref[...]).astype(h_ref.dtype)

    return h_ref[...]


def _mm_norm_kernel(x_ref, g_ref, w_ref, o_ref, h_ref):
    o_ref[...] = jnp.dot(_normed(x_ref, g_ref, h_ref), w_ref[...],
                         preferred_element_type=_F32).astype(o_ref.dtype)


def _mm_norm_swiglu_kernel(x_ref, g_ref, wg_ref, wu_ref, o_ref, h_ref):
    h = _normed(x_ref, g_ref, h_ref)
    g = jnp.dot(h, wg_ref[...], preferred_element_type=_F32)
    u = jnp.dot(h, wu_ref[...], preferred_element_type=_F32)
    o_ref[...] = (g * jax.nn.sigmoid(g) * u).astype(o_ref.dtype)


def _mm_norm_t_kernel(w_ref, x_ref, g_ref, o_ref, h_ref):
    y = lax.dot_general(w_ref[...], _normed(x_ref, g_ref, h_ref), (((1,), (1,)), ((), ())),
                        preferred_element_type=_F32).astype(o_ref.dtype)
    tk = o_ref.shape[2]
    for r in range(o_ref.shape[0]):
        o_ref[r] = y[:, r * tk:(r + 1) * tk]


def _mm_residual_kernel(x_ref, w_ref, r_ref, o_ref, *, scale):
    y = jnp.dot(x_ref[...], w_ref[...], preferred_element_type=_F32)
    o_ref[...] = r_ref[...] + scale * y


def _tile(n, pref):
    return pref if n % pref == 0 else n


def _matmul_norm(x, g, w, out_dtype, bm=None, bn=None, col_block=0):
    m = x.shape[0]
    k, n = w.shape
    bm = _tile(m, bm or MM_BM)
    bn = _tile(n, bn or MM_BN)
    return pl.pallas_call(
        _mm_norm_kernel,
        grid=(m // bm, n // bn),
        in_specs=[pl.BlockSpec((bm, k), lambda i, j: (i, col_block)),
                  pl.BlockSpec((1, k), lambda i, j: (0, 0)),
                  pl.BlockSpec((k, bn), lambda i, j: (0, j))],
        out_specs=pl.BlockSpec((bm, bn), lambda i, j: (i, j)),
        out_shape=jax.ShapeDtypeStruct((m, n), out_dtype),
        scratch_shapes=[pltpu.VMEM((bm, k), MXU_DTYPE)],
        compiler_params=_params(("parallel", "arbitrary")),
        name="matmul_norm",
    )(x, g.reshape(1, k).astype(_F32), w)


def _matmul_norm_t(w_t, x, g, tk, out_dtype, bn=None, col_block=0, row_tiles=1):
    n, k = w_t.shape
    t = x.shape[0]
    bn = _tile(n, bn or MM_BN)
    rt = row_tiles if (t // tk) % row_tiles == 0 else 1
    return pl.pallas_call(
        _mm_norm_t_kernel,
        grid=(t // (rt * tk), n // bn),
        in_specs=[pl.BlockSpec((bn, k), lambda i, j: (j, 0)),
                  pl.BlockSpec((rt * tk, k), lambda i, j: (i, col_block)),
                  pl.BlockSpec((1, k), lambda i, j: (0, 0))],
        out_specs=pl.BlockSpec((rt, bn, tk), lambda i, j: (i, j, 0)),
        out_shape=jax.ShapeDtypeStruct((t // tk, n, tk), out_dtype),
        scratch_shapes=[pltpu.VMEM((rt * tk, k), MXU_DTYPE)],
        compiler_params=_params(("parallel", "arbitrary")),
        name="matmul_norm_t",
    )(w_t, x, g.reshape(1, k).astype(_F32))


def _matmul_norm_swiglu(x, g, w_in, bm=None, bn=512):
    m, k = x.shape
    f = w_in.shape[1] // 2
    bm = _tile(m, bm or MM_BM)
    bn = _tile(f, bn)
    nb = f // bn
    return pl.pallas_call(
        _mm_norm_swiglu_kernel,
        grid=(m // bm, nb),
        in_specs=[pl.BlockSpec((bm, k), lambda i, j: (i, 0)),
                  pl.BlockSpec((1, k), lambda i, j: (0, 0)),
                  pl.BlockSpec((k, bn), lambda i, j: (0, j)),
                  pl.BlockSpec((k, bn), lambda i, j: (0, j + nb))],
        out_specs=pl.BlockSpec((bm, bn), lambda i, j: (i, j)),
        out_shape=jax.ShapeDtypeStruct((m, f), MXU_DTYPE),
        scratch_shapes=[pltpu.VMEM((bm, k), MXU_DTYPE)],
        compiler_params=_params(("parallel", "arbitrary")),
        name="matmul_norm_swiglu",
    )(x, g.reshape(1, k).astype(_F32), w_in, w_in)


def _matmul_residual(x, w, res, scale, bn=512):
    m, k = x.shape
    n = w.shape[1]
    bn = _tile(n, bn)
    bm = _tile(m, MM_BM)
    for cand in (2 * MM_BM,):
        windows = 2 * (cand * k * 2 + k * bn * 2 + 2 * cand * bn * 4)
        if m % cand == 0 and windows <= MM_RESIDUAL_VMEM_MB << 20:
            bm = cand
    return pl.pallas_call(
        functools.partial(_mm_residual_kernel, scale=scale),
        grid=(m // bm, n // bn),
        in_specs=[pl.BlockSpec((bm, k), lambda i, j: (i, 0)),
                  pl.BlockSpec((k, bn), lambda i, j: (0, j)),
                  pl.BlockSpec((bm, bn), lambda i, j: (i, j))],
        out_specs=pl.BlockSpec((bm, bn), lambda i, j: (i, j)),
        out_shape=jax.ShapeDtypeStruct((m, n), _F32),
        compiler_params=_params(("parallel", "arbitrary")),
        name="matmul_residual",
    )(x, w, res)


def _ffn(x, norm_g, w_in, w_out):
    act = _matmul_norm_swiglu(x, norm_g, w_in.astype(MXU_DTYPE))
    return _matmul_residual(act, w_out.astype(MXU_DTYPE), x, 0.5)


def _rope_tables(positions, d, half_width):
    inv = jnp.power(ROPE_THETA, -jnp.arange(0, d, 2, dtype=_F32) / d)
    ang = positions.astype(_F32).reshape(-1, 1) * inv
    pad = jnp.zeros((ang.shape[0], half_width - d // 2), _F32)
    cos, sin = jnp.cos(ang), jnp.sin(ang)
    ra = jnp.concatenate([cos, pad, cos, pad], axis=1)
    rb = jnp.concatenate([-sin, pad, sin, pad], axis=1)
    return ra, rb


def _softmax_step(s, vt, m_ref, acc_ref):
    m_prev = m_ref[...]
    m_new = jnp.maximum(m_prev, jnp.max(s, axis=0, keepdims=True))
    alpha = jnp.exp2(m_prev - m_new)
    p = jnp.exp2(s - m_new).astype(MXU_DTYPE)
    vt1 = jnp.concatenate([vt, jnp.ones((ONES_ROWS, vt.shape[1]), vt.dtype)], axis=0)
    acc_ref[...] = alpha * acc_ref[...] + jnp.dot(vt1, p, preferred_element_type=_F32)
    m_ref[...] = m_new


def _pipelined_sweep(n, score, update, sa_ref, sb_ref):
    sa_ref[...] = score(0, True)

    def pair(t):
        sb_ref[...] = score(t + 1, False)
        update(t, sa_ref[...])
        sa_ref[...] = score(t + 2, False)
        update(t + 1, sb_ref[...])

    def quad_body(p, carry):
        pair(4 * p)
        pair(4 * p + 2)
        return carry

    n_quads = (n - 1) // 4
    lax.fori_loop(0, n_quads, quad_body, 0)

    def pair_body(p, carry):
        pair(4 * n_quads + 2 * p)
        return carry

    n_pairs = (n - 1 - 4 * n_quads) // 2
    lax.fori_loop(0, n_pairs, pair_body, 0)
    t = 4 * n_quads + 2 * n_pairs

    @pl.when(t == n - 1)
    def _odd_tail():
        update(t, sa_ref[...])

    @pl.when(t < n - 1)
    def _even_tail():
        sb_ref[...] = score(t + 1, False)
        update(t, sa_ref[...])
        update(t + 1, sb_ref[...])


def _static_sweep(n, score, update, sa_ref, sb_ref, unroll=8):
    sa_ref[...] = score(0)
    bufs = (sa_ref, sb_ref)

    def run(base, count, last):
        for u in range(count):
            cur, nxt = bufs[u % 2], bufs[(u + 1) % 2]
            if not (last and u == count - 1):
                nxt[...] = score(base + u + 1)
            update(base + u, cur[...])

    n_loop = (n - 1) // unroll * unroll

    def body(p, carry):
        run(p * unroll, unroll, False)
        return carry

    lax.fori_loop(0, n_loop // unroll, body, 0)
    run(n_loop, n - n_loop, True)


def _softmax_finish(acc_ref, dv):
    acc = acc_ref[...]
    return acc[:dv, :] / acc[dv:dv + 1, :]


def _mla_attn_kernel(jt_ref, it_ref, q_ref, kn_ref, kr_ref, vt_ref, cos_ref, sin_ref, o_ref,
                     qs_ref, m_ref, acc_ref, sa_ref, sb_ref, *, nq, tq, n_off):
    half = A_ROPE // 2
    r1 = slice(A_NOPE, A_NOPE + half)
    r2 = slice(A_NOPE + LANES // 2, A_NOPE + LANES // 2 + half)

    def prepare(i, carry):
        qt = q_ref[i]
        x1 = qt[r1, :].astype(_F32)
        x2 = qt[r2, :].astype(_F32)
        cos = cos_ref[i]
        sin = sin_ref[i]
        qs_ref[i] = qt
        qs_ref[i, r1, :] = (x1 * cos - x2 * sin).astype(qs_ref.dtype)
        qs_ref[i, r2, :] = (x1 * sin + x2 * cos).astype(qs_ref.dtype)
        m_ref[i] = jnp.full((1, tq), -jnp.inf, _F32)
        acc_ref[i] = jnp.zeros(acc_ref.shape[1:], _F32)
        return carry

    lax.fori_loop(0, nq, prepare, 0)

    def scores(j, i):
        off = pl.multiple_of(j * tq, tq)
        kc = jnp.concatenate([kn_ref[pl.ds(off, tq), :], kr_ref[pl.ds(off, tq), :]], axis=1)
        return jnp.dot(kc, qs_ref[i], preferred_element_type=_F32)

    def update(j, i, s):
        _softmax_step(s, vt_ref[j], m_ref.at[i], acc_ref.at[i])

    kchunk = lax.broadcasted_iota(_I32, (tq, tq), 0) // CHUNK
    qchunk = lax.broadcasted_iota(_I32, (tq, tq), 1) // CHUNK
    _static_sweep(nq, lambda t: jnp.where(qchunk >= kchunk, scores(t, t), -jnp.inf),
                  lambda t, s: update(t, t, s), sa_ref, sb_ref)
    if n_off:
        _static_sweep(n_off, lambda t: scores(jt_ref[t], it_ref[t]),
                      lambda t, s: update(jt_ref[t], it_ref[t], s), sa_ref, sb_ref)

    def finish(i, carry):
        off = pl.multiple_of(i * tq, tq)
        o_ref[pl.ds(off, tq), :] = _softmax_finish(acc_ref.at[i], A_V).T.astype(o_ref.dtype)
        return carry

    lax.fori_loop(0, nq, finish, 0)


def _mla_attention(qt, kn, kr, vt, cos_t, sin_t, batch, seq, tq):
    nq = seq // tq
    pairs = [(j, i) for j in range(nq) for i in range(j + 1, nq)]
    jt = jnp.asarray([p[0] for p in pairs] or [0], _I32)
    it = jnp.asarray([p[1] for p in pairs] or [0], _I32)
    half = A_ROPE // 2
    smem = pl.BlockSpec(memory_space=pltpu.SMEM)
    return pl.pallas_call(
        functools.partial(_mla_attn_kernel, nq=nq, tq=tq, n_off=len(pairs)),
        grid=(batch, A_HEADS),
        in_specs=[smem, smem,
                  pl.BlockSpec((nq, 2 * LANES, tq), lambda b, h: (b, h, 0)),
                  pl.BlockSpec((seq, LANES), lambda b, h: (b, h)),
                  pl.BlockSpec((seq, LANES), lambda b, h: (b, 0)),
                  pl.BlockSpec((nq, A_V, tq), lambda b, h: (b, h, 0)),
                  pl.BlockSpec((nq, half, tq), lambda b, h: (b, 0, 0)),
                  pl.BlockSpec((nq, half, tq), lambda b, h: (b, 0, 0))],
        out_specs=pl.BlockSpec((seq, A_V), lambda b, h: (b, h)),
        out_shape=jax.ShapeDtypeStruct((batch * seq, A_HEADS * A_V), MXU_DTYPE),
        scratch_shapes=[pltpu.VMEM((nq, 2 * LANES, tq), MXU_DTYPE),
                        pltpu.VMEM((nq, 1, tq), _F32),
                        pltpu.VMEM((nq, A_V + ONES_ROWS, tq), _F32),
                        pltpu.VMEM((tq, tq), _F32),
                        pltpu.VMEM((tq, tq), _F32)],
        compiler_params=_params(("parallel", "parallel")),
        name="mla_attention",
    )(jt, it, qt, kn, kr, vt, cos_t, sin_t)


def _pad_rope_cols(w, n_heads, nope):
    k = w.shape[0]
    w = w.reshape(k, n_heads, nope + A_ROPE)
    half = A_ROPE // 2
    z = jnp.zeros((k, n_heads, LANES // 2 - half), w.dtype)
    w = jnp.concatenate([w[..., :nope], w[..., nope:nope + half], z, w[..., nope + half:], z], axis=-1)
    return w.reshape(k, n_heads * (nope + LANES))


def _mla_mixer(x, positions, mix_g, w_in, q_norm, kv_norm, w_uq, w_ukv, w_out, batch, seq):
    assert A_NOPE == LANES and A_V == LANES and A_ROPE == LANES // 2
    tq = min(MLA_TQ, seq)
    n_lat = A_Q_LORA + A_KV_LORA
    w_in_p = jnp.concatenate([w_in[:, :n_lat], _pad_rope_cols(w_in[:, n_lat:], 1, 0)], axis=1)
    lat = _matmul_norm(x, mix_g, w_in_p.astype(MXU_DTYPE), _F32, bm=MM_BM // 2,
                       bn=w_in_p.shape[1])
    ra, rb = _rope_tables(positions, A_ROPE, LANES // 2)
    kr = _rope(lat, ra, rb, MXU_DTYPE, col_block=n_lat // LANES, width=LANES)
    qscale = (A_NOPE + A_ROPE) ** -0.5 * LOG2E
    w_qt = (_pad_rope_cols(w_uq, A_HEADS, A_NOPE) * qscale).T.astype(MXU_DTYPE)
    qt = _matmul_norm_t(w_qt, lat, q_norm, tq, MXU_DTYPE, bn=2 * MM_BN, row_tiles=2)
    w_ukv = w_ukv.reshape(A_KV_LORA, A_HEADS, A_NOPE + A_V)
    w_kn = w_ukv[..., :A_NOPE].reshape(A_KV_LORA, A_HEADS * A_NOPE)
    w_vt = w_ukv[..., A_NOPE:].reshape(A_KV_LORA, A_HEADS * A_V).T
    kv_blk = A_Q_LORA // A_KV_LORA
    kn = _matmul_norm(lat, kv_norm, w_kn.astype(MXU_DTYPE), MXU_DTYPE, bm=2 * MM_BM, bn=2 * MM_BN,
                      col_block=kv_blk)
    vt = _matmul_norm_t(w_vt.astype(MXU_DTYPE), lat, kv_norm, tq, MXU_DTYPE, bn=2 * MM_BN,
                        col_block=kv_blk, row_tiles=2)
    half = A_ROPE // 2
    cos_t = ra[:, :half].reshape(-1, tq, half).transpose(0, 2, 1)
    sin_t = rb[:, LANES // 2:LANES // 2 + half].reshape(-1, tq, half).transpose(0, 2, 1)
    o = _mla_attention(qt, kn, kr, vt, cos_t, sin_t, batch, seq, tq)
    return _matmul_residual(o, w_out.astype(MXU_DTYPE), x, 1.0)


def _dsa_kernel(qi_ref, wit_ref, ki_ref, ra_ref, rb_ref, q_ref, k_ref, vt_ref, o_ref,
                qit_ref, key_ref, thr_ref, qt_ref, m_ref, acc_ref, sa_ref, sb_ref,
                *, tq, tk, topk, group, wi_scale, qscale):
    i = pl.program_id(1)
    h = pl.program_id(2)
    nkb = (i + 1) * (tq // tk)
    ra = ra_ref[...]
    rb = rb_ref[...]

    @pl.when(h == 0)
    def _select():
        for hd in range(B_IDX_HEADS):
            sl = slice(hd * LANES, (hd + 1) * LANES)
            x = qi_ref[:, sl].astype(_F32)
            x = x * ra + _rot_half(x) * rb
            qit_ref[sl, :] = x.T.astype(qit_ref.dtype)
        wit = wit_ref[...] * wi_scale

        def score_tile(j, carry):
            off = pl.multiple_of(j * tk, tk)
            ki = ki_ref[pl.ds(off, tk), :]
            sc = jnp.zeros((tk, tq), _F32)
            for hd in range(B_IDX_HEADS):
                d = jnp.dot(ki, qit_ref[hd * LANES:(hd + 1) * LANES, :],
                            preferred_element_type=_F32)
                sc = sc + jnp.maximum(d, 0.0) * wit[hd:hd + 1, :]
            kchunk = (off + lax.broadcasted_iota(_I32, (tk, tq), 0)) // CHUNK
            qchunk = (i * tq + lax.broadcasted_iota(_I32, (tk, tq), 1)) // CHUNK
            sc = jnp.where(qchunk >= kchunk, sc, -jnp.inf)
            bits = pltpu.bitcast(sc, _I32)
            key_ref[j] = jnp.where(bits >= 0, bits, bits ^ 0x7FFFFFFF)
            return carry

        lax.fori_loop(0, nkb, score_tile, 0)

        def count(pred):
            def tile(j, acc):
                hit = jnp.where(pred(key_ref[j]), 1.0, 0.0)
                return acc + jnp.sum(hit.reshape(tk // SUBLANES, SUBLANES, tq), axis=0)

            acc = lax.fori_loop(0, nkb, tile, jnp.zeros((SUBLANES, tq), _F32))
            return jnp.sum(acc, axis=0, keepdims=True)

        def bisect(b, t):
            cand = t + jnp.left_shift(jnp.int32(1), 31 - b)
            return jnp.where(count(lambda kt: kt >= cand) >= topk, cand, t)

        t = lax.fori_loop(0, 32, bisect, jnp.full((1, tq), _INT_MIN, _I32))

        need = topk - count(lambda kt: kt > t)
        lower = jnp.where(lax.broadcasted_iota(_I32, (tk, tk), 0)
                          >= lax.broadcasted_iota(_I32, (tk, tk), 1), 1.0, 0.0).astype(MXU_DTYPE)

        def demote_tile(j, seen):
            kt = key_ref[j]
            eq = kt == t
            rank = seen + jnp.dot(lower, jnp.where(eq, 1.0, 0.0).astype(MXU_DTYPE),
                                  preferred_element_type=_F32)
            key_ref[j] = jnp.where(eq, jnp.where(rank > need, t - 1, kt), kt)
            return rank[tk - 1:tk, :]

        lax.fori_loop(0, nkb, demote_tile, jnp.zeros((1, tq), _F32))
        thr_ref[...] = jnp.maximum(t, _KEY_OF_NEG_INF + 1)

    q = q_ref[...]
    for g in range(group):
        x = q[:, g * LANES:(g + 1) * LANES].astype(_F32)
        x = (x * ra + _rot_half(x) * rb) * qscale
        qt_ref[:, g * tq:(g + 1) * tq] = x.T.astype(qt_ref.dtype)
    m_ref[...] = jnp.full(m_ref.shape, _NEG_BIG, _F32)
    acc_ref[...] = jnp.zeros(acc_ref.shape, _F32)
    thr = thr_ref[...]

    def score(j, first):
        off = pl.multiple_of(j * tk, tk)
        bias = jnp.where(key_ref[j] >= thr, 0.0, _NEG_BIG)
        bias = jnp.concatenate([bias] * group, axis=1)
        return jnp.dot(k_ref[pl.ds(off, tk), :], qt_ref[...], preferred_element_type=_F32) + bias

    def update(j, s):
        _softmax_step(s, vt_ref[j], m_ref, acc_ref)

    _pipelined_sweep(nkb, score, update, sa_ref, sb_ref)
    out = _softmax_finish(acc_ref, B_HEAD_DIM)
    for g in range(group):
        o_ref[:, g * LANES:(g + 1) * LANES] = out[:, g * tq:(g + 1) * tq].T.astype(o_ref.dtype)


def _dsa_attention(proj, k_roped, vt, ki, wit, ra, rb, batch, seq, tq, tk):
    nq = seq // tq
    group = B_HEADS // B_KV_HEADS
    topk = min(B_TOPK_MAX, seq // 4)
    qi_w = B_IDX_HEADS * B_IDX_DIM
    q_w = group * B_HEAD_DIM
    q_blk0 = qi_w // q_w
    kern = functools.partial(
        _dsa_kernel, tq=tq, tk=tk, topk=float(topk), group=group,
        wi_scale=B_IDX_HEADS ** -0.5 * B_IDX_DIM ** -0.5, qscale=B_HEAD_DIM ** -0.5 * LOG2E)
    return pl.pallas_call(
        kern,
        grid=(batch, nq, B_KV_HEADS),
        in_specs=[pl.BlockSpec((tq, qi_w), lambda b, i, h: (b * nq + i, 0)),
                  pl.BlockSpec((B_IDX_HEADS, tq), lambda b, i, h: (0, b * nq + i)),
                  pl.BlockSpec((seq, LANES), lambda b, i, h: (b, 0)),
                  pl.BlockSpec((tq, LANES), lambda b, i, h: (b * nq + i, 0)),
                  pl.BlockSpec((tq, LANES), lambda b, i, h: (b * nq + i, 0)),
                  pl.BlockSpec((tq, q_w), lambda b, i, h: (b * nq + i, q_blk0 + h)),
                  pl.BlockSpec((seq, B_HEAD_DIM), lambda b, i, h: (b, h)),
                  pl.BlockSpec((seq // tk, B_HEAD_DIM, tk), lambda b, i, h: (b, h, 0))],
        out_specs=pl.BlockSpec((tq, q_w), lambda b, i, h: (b * nq + i, h)),
        out_shape=jax.ShapeDtypeStruct((batch * seq, B_HEADS * B_HEAD_DIM), MXU_DTYPE),
        scratch_shapes=[pltpu.VMEM((qi_w, tq), MXU_DTYPE),
                        pltpu.VMEM((seq // tk, tk, tq), _I32),
                        pltpu.VMEM((1, tq), _I32),
                        pltpu.VMEM((B_HEAD_DIM, group * tq), MXU_DTYPE),
                        pltpu.VMEM((1, group * tq), _F32),
                        pltpu.VMEM((B_HEAD_DIM + ONES_ROWS, group * tq), _F32),
                        pltpu.VMEM((tk, group * tq), _F32),
                        pltpu.VMEM((tk, group * tq), _F32)],
        compiler_params=_params(("parallel", "arbitrary", "arbitrary")),
        name="dsa_attention",
    )(proj, wit, ki, ra, rb, proj, k_roped, vt)


def _dsa_mixer(x, positions, mix_g, w_in, idx_k_norm, w_out, batch, seq):
    assert B_HEAD_DIM == LANES and B_IDX_DIM == LANES and B_IDX_HEADS % SUBLANES == 0
    tq = min(DSA_TQ, seq)
    tk = min(DSA_TK, tq)
    nq, nk = B_HEADS * B_HEAD_DIM, B_KV_HEADS * B_HEAD_DIM
    nqi = B_IDX_HEADS * B_IDX_DIM
    o_q, o_k, o_v, o_qi, o_ki, o_wi = np.cumsum([0, nq, nk, nk, nqi, B_IDX_DIM])
    w_main = jnp.concatenate([w_in[:, o_qi:o_ki], w_in[:, o_q:o_v]], axis=1)
    w_side = jnp.concatenate(
        [w_in[:, o_ki:o_wi], w_in[:, o_wi:],
         jnp.zeros((w_in.shape[0], LANES - B_IDX_HEADS), w_in.dtype)], axis=1)
    proj = _matmul_norm(x, mix_g, w_main.astype(MXU_DTYPE), MXU_DTYPE, bn=512)
    side = _matmul_norm(x, mix_g, w_side.astype(MXU_DTYPE), _F32, bn=2 * LANES)
    vt = _matmul_norm_t(w_in[:, o_v:o_qi].T.astype(MXU_DTYPE), x, mix_g, tk, MXU_DTYPE)
    ra, rb = _rope_tables(positions, B_HEAD_DIM, LANES // 2)
    k_roped = _rope(proj, ra, rb, MXU_DTYPE, col_block=(nqi + nq) // nk, width=nk)
    ki = _rmsnorm_rope(side, idx_k_norm, ra, rb, MXU_DTYPE, col_block=0)
    wit = side[:, LANES:LANES + B_IDX_HEADS].T
    o = _dsa_attention(proj, k_roped, vt, ki, wit, ra, rb, batch, seq, tq, tk)
    return _matmul_residual(o, w_out.astype(MXU_DTYPE), x, 1.0)


def _log_sigmoid(x):
    return jnp.minimum(x, 0.0) - jnp.log(1.0 + jnp.exp(-jnp.abs(x)))


def _mlstm_kernel(bias_ref, q_ref, k_ref, v_ref, o_ref, grow_ref, gcol_ref, hn_ref, y_ref,
                  c_ref, n_ref, m_ref, *, n_heads, dk, dv):
    @pl.when(pl.program_id(1) == 0)
    def _init():
        c_ref[...] = jnp.zeros(c_ref.shape, _F32)
        n_ref[...] = jnp.zeros(n_ref.shape, _F32)
        m_ref[...] = jnp.zeros(m_ref.shape, _F32)

    grow = grow_ref[0]
    gcol = gcol_ref[0]
    rows = lax.broadcasted_iota(_I32, (CHUNK, CHUNK), 0)
    cols = lax.broadcasted_iota(_I32, (CHUNK, CHUNK), 1)
    tril = rows >= cols
    triu = rows <= cols

    for h in range(n_heads):
        b_i = bias_ref[h]
        b_f = bias_ref[n_heads + h]
        li_row = grow[h:h + 1, :] + b_i
        lf_row = _log_sigmoid(grow[n_heads + h:n_heads + h + 1, :] + b_f)
        li_col = gcol[:, h:h + 1] + b_i
        lf_col = _log_sigmoid(gcol[:, n_heads + h:n_heads + h + 1] + b_f)

        b_col = jnp.sum(jnp.where(tril, lf_row, 0.0), axis=1, keepdims=True)
        b_row = jnp.sum(jnp.where(triu, lf_col, 0.0), axis=0, keepdims=True)
        b_end = jnp.sum(lf_row, axis=1, keepdims=True)

        m_prev = m_ref[h]
        a = b_col + m_prev
        d = jnp.where(tril, b_col - b_row + li_row, -jnp.inf)
        m_t = jnp.maximum(a, jnp.max(d, axis=1, keepdims=True))
        inter = jnp.exp(a - m_t)

        q = q_ref[:, h * dk:(h + 1) * dk]
        k = k_ref[:, h * dk:(h + 1) * dk]
        v = v_ref[:, h * dv:(h + 1) * dv]
        qf = q.astype(_F32) * dk ** -0.5
        qs = qf.astype(MXU_DTYPE)
        qk = lax.dot_general(qs, k, (((1,), (1,)), ((), ())),
                             preferred_element_type=_F32) * jnp.exp(d - m_t)
        ct = c_ref[h]
        num = inter * jnp.dot(qs, ct.astype(MXU_DTYPE), preferred_element_type=_F32) + jnp.dot(
            qk.astype(MXU_DTYPE), v, preferred_element_type=_F32)
        n_row = n_ref[h]
        den = inter * jnp.sum(qf * n_row, axis=1, keepdims=True) + jnp.sum(qk, axis=1, keepdims=True)
        hs = num / jnp.maximum(jnp.abs(den), jnp.exp(-m_t))

        g_col = b_end - b_col + li_col
        m_new = jnp.maximum(b_end + m_prev, jnp.max(g_col, axis=0, keepdims=True))
        decay = jnp.exp(b_end + m_prev - m_new)
        wk = jnp.exp(g_col - m_new)
        vw = (v.astype(_F32) * wk).astype(MXU_DTYPE)
        c_ref[h] = decay * ct + lax.dot_general(k, vw, (((0,), (0,)), ((), ())),
                                                preferred_element_type=_F32)
        n_ref[h] = decay * n_row + jnp.sum(wk * k.astype(_F32), axis=0, keepdims=True)
        m_ref[h] = m_new

        hs = hs * lax.rsqrt(jnp.mean(hs * hs, axis=1, keepdims=True) + NORM_EPS)
        gate = jax.nn.sigmoid(o_ref[:, h * dv:(h + 1) * dv].astype(_F32))
        y_ref[:, h * dv:(h + 1) * dv] = (hs * hn_ref[:, h * dv:(h + 1) * dv] * gate).astype(y_ref.dtype)


def _mlstm_scan(proj, g_row, g_col, gate_bias, head_norm, batch, seq, dk, dv):
    nc = seq // CHUNK
    hq = C_HEADS
    qk_w, v_w = hq * dk, hq * dv
    v_blk0 = 2 * qk_w // v_w
    return pl.pallas_call(
        functools.partial(_mlstm_kernel, n_heads=hq, dk=dk, dv=dv),
        grid=(batch, nc),
        in_specs=[pl.BlockSpec(memory_space=pltpu.SMEM),
                  pl.BlockSpec((CHUNK, qk_w), lambda b, c: (b * nc + c, 0)),
                  pl.BlockSpec((CHUNK, qk_w), lambda b, c: (b * nc + c, 1)),
                  pl.BlockSpec((CHUNK, v_w), lambda b, c: (b * nc + c, v_blk0)),
                  pl.BlockSpec((CHUNK, v_w), lambda b, c: (b * nc + c, v_blk0 + 1)),
                  pl.BlockSpec((1, 2 * hq, CHUNK), lambda b, c: (b * nc + c, 0, 0)),
                  pl.BlockSpec((1, CHUNK, 2 * hq), lambda b, c: (b * nc + c, 0, 0)),
                  pl.BlockSpec((1, v_w), lambda b, c: (0, 0))],
        out_specs=pl.BlockSpec((CHUNK, v_w), lambda b, c: (b * nc + c, 0)),
        out_shape=jax.ShapeDtypeStruct((batch * seq, v_w), MXU_DTYPE),
        scratch_shapes=[pltpu.VMEM((hq, dk, dv), _F32),
                        pltpu.VMEM((hq, 1, dk), _F32),
                        pltpu.VMEM((hq, 1, 1), _F32)],
        compiler_params=_params(("parallel", "arbitrary")),
        name="mlstm_scan",
    )(gate_bias.astype(_F32), proj, proj, proj, proj, g_row, g_col,
      head_norm.reshape(1, v_w).astype(_F32))


def _mlstm_mixer(x, mix_g, w_in, gate_bias, head_norm, w_out, batch, seq):
    d_model = x.shape[1]
    dv = d_model // C_HEADS
    dk = dv // 2
    hq = C_HEADS
    n_main = 2 * hq * dk + 2 * hq * dv
    w_gate = jnp.concatenate(
        [w_in[:, n_main:], jnp.zeros((w_in.shape[0], LANES - 2 * hq), w_in.dtype)], axis=1)
    proj = _matmul_norm(x, mix_g, w_in[:, :n_main].astype(MXU_DTYPE), MXU_DTYPE)
    gates = _matmul_norm(x, mix_g, w_gate.astype(MXU_DTYPE), _F32, bn=LANES)
    nc = seq // CHUNK
    g_col = gates[:, :2 * hq].reshape(batch * nc, CHUNK, 2 * hq)
    g_row = g_col.transpose(0, 2, 1)
    y = _mlstm_scan(proj, g_row, g_col, gate_bias, head_norm, batch, seq, dk, dv)
    return _matmul_residual(y, w_out.astype(MXU_DTYPE), x, 1.0)


def kernel(x, positions, ffn1_norm, ffn1_w_in, ffn1_w_out, mix_norm, ffn2_norm, ffn2_w_in, ffn2_w_out,
           a_w_in, a_q_norm, a_kv_norm, a_w_uq, a_w_ukv, a_w_out,
           b_w_in, b_idx_k_norm, b_w_out,
           c_w_in, c_gate_bias, c_head_norm, c_w_out, final_norm):
    batch, seq, d_model = x.shape
    depth = ffn1_norm.shape[0]
    x = x.reshape(batch * seq, d_model)
    for i in range(depth):
        x = _ffn(x, ffn1_norm[i], ffn1_w_in[i], ffn1_w_out[i])
        kind, j = i % N_MIXERS, i // N_MIXERS
        if kind == 0:
            x = _mla_mixer(x, positions, mix_norm[i], a_w_in[j], a_q_norm[j], a_kv_norm[j],
                           a_w_uq[j], a_w_ukv[j], a_w_out[j], batch, seq)
        elif kind == 1:
            x = _dsa_mixer(x, positions, mix_norm[i], b_w_in[j], b_idx_k_norm[j], b_w_out[j], batch, seq)
        else:
            x = _mlstm_mixer(x, mix_norm[i], c_w_in[j], c_gate_bias[j], c_head_norm[j], c_w_out[j],
                             batch, seq)
        x = _ffn(x, ffn2_norm[i], ffn2_w_in[i], ffn2_w_out[i])
    return _rmsnorm(x, final_norm, _F32).reshape(batch, seq, d_model)
```

```python
import functools

import jax
import jax.numpy as jnp
import numpy as np
from jax import lax
from jax.experimental import pallas as pl
from jax.experimental.pallas import tpu as pltpu

CHUNK = 64
ROPE_THETA = 10000.0
NORM_EPS = 1e-6
D_FF = 6144

A_HEADS = 64
A_Q_LORA = 1024
A_KV_LORA = 512
A_NOPE = 128
A_ROPE = 64
A_V = 128

B_HEADS = 32
B_KV_HEADS = 8
B_HEAD_DIM = 128
B_IDX_HEADS = 32
B_IDX_DIM = 128
B_TOPK_MAX = 256

C_HEADS = 8
N_MIXERS = 3

LANES = 128
SUBLANES = 8
ONES_ROWS = 16
LOG2E = 1.4426950408889634
MXU_DTYPE = jnp.bfloat16
VMEM_LIMIT_MB = 56

ROW_TILE = 256
MM_BM = 512
MM_BN = 1024
MM_RESIDUAL_VMEM_MB = 48
MLA_TQ = 512
DSA_TQ = 256
DSA_TK = 256

_F32 = jnp.float32
_I32 = jnp.int32
_NEG_BIG = -1e30
_INT_MIN = -(2 ** 31)
_KEY_OF_NEG_INF = -2139095041


def _params(semantics):
    return pltpu.CompilerParams(dimension_semantics=semantics,
                                vmem_limit_bytes=VMEM_LIMIT_MB << 20)


def _rot_half(x):
    return pltpu.roll(x, LANES // 2, 1)


def _rmsnorm_kernel(x_ref, g_ref, o_ref):
    x = x_ref[...].astype(_F32)
    y = x * lax.rsqrt(jnp.mean(x * x, axis=-1, keepdims=True) + NORM_EPS)
    o_ref[...] = (y * g_ref[...]).astype(o_ref.dtype)


def _rmsnorm(x, g, out_dtype, col_block=0, width=None):
    t = x.shape[0]
    width = x.shape[1] if width is None else width
    bt = min(ROW_TILE, t)
    return pl.pallas_call(
        _rmsnorm_kernel,
        grid=(t // bt,),
        in_specs=[pl.BlockSpec((bt, width), lambda i: (i, col_block)),
                  pl.BlockSpec((1, width), lambda i: (0, 0))],
        out_specs=pl.BlockSpec((bt, width), lambda i: (i, 0)),
        out_shape=jax.ShapeDtypeStruct((t, width), out_dtype),
        compiler_params=_params(("parallel",)),
        name="rmsnorm",
    )(x, g.reshape(1, width).astype(_F32))


def _rope_kernel(x_ref, ra_ref, rb_ref, o_ref, *, n_blocks):
    ra = ra_ref[...]
    rb = rb_ref[...]
    for c in range(n_blocks):
        sl = slice(c * LANES, (c + 1) * LANES)
        x = x_ref[:, sl].astype(_F32)
        o_ref[:, sl] = (x * ra + _rot_half(x) * rb).astype(o_ref.dtype)


def _rope(x, ra, rb, out_dtype, col_block=0, width=None):
    t = x.shape[0]
    width = x.shape[1] if width is None else width
    bt = min(ROW_TILE, t)
    return pl.pallas_call(
        functools.partial(_rope_kernel, n_blocks=width // LANES),
        grid=(t // bt,),
        in_specs=[pl.BlockSpec((bt, width), lambda i: (i, col_block)),
                  pl.BlockSpec((bt, LANES), lambda i: (i, 0)),
                  pl.BlockSpec((bt, LANES), lambda i: (i, 0))],
        out_specs=pl.BlockSpec((bt, width), lambda i: (i, 0)),
        out_shape=jax.ShapeDtypeStruct((t, width), out_dtype),
        compiler_params=_params(("parallel",)),
        name="rope",
    )(x, ra, rb)


def _rmsnorm_rope_kernel(x_ref, g_ref, ra_ref, rb_ref, o_ref):
    x = x_ref[...].astype(_F32)
    y = x * lax.rsqrt(jnp.mean(x * x, axis=-1, keepdims=True) + NORM_EPS) * g_ref[...]
    o_ref[...] = (y * ra_ref[...] + _rot_half(y) * rb_ref[...]).astype(o_ref.dtype)


def _rmsnorm_rope(x, g, ra, rb, out_dtype, col_block=0):
    t = x.shape[0]
    bt = min(ROW_TILE, t)
    return pl.pallas_call(
        _rmsnorm_rope_kernel,
        grid=(t // bt,),
        in_specs=[pl.BlockSpec((bt, LANES), lambda i: (i, col_block)),
                  pl.BlockSpec((1, LANES), lambda i: (0, 0)),
                  pl.BlockSpec((bt, LANES), lambda i: (i, 0)),
                  pl.BlockSpec((bt, LANES), lambda i: (i, 0))],
        out_specs=pl.BlockSpec((bt, LANES), lambda i: (i, 0)),
        out_shape=jax.ShapeDtypeStruct((t, LANES), out_dtype),
        compiler_params=_params(("parallel",)),
        name="rmsnorm_rope",
    )(x, g.reshape(1, LANES).astype(_F32), ra, rb)


def _normed(x_ref, g_ref, h_ref):
    @pl.when(pl.program_id(1) == 0)
    def _normalise():
        x = x_ref[...].astype(_F32)
        y = x * lax.rsqrt(jnp.mean(x * x, axis=-1, keepdims=True) + NORM_EPS)
        h_ref[...] = (y * g_ref[...]).astype(h_ref.dtype)

    return h_ref[...]


def _mm_norm_kernel(x_ref, g_ref, w_ref, o_ref, h_ref):
    o_ref[...] = jnp.dot(_normed(x_ref, g_ref, h_ref), w_ref[...],
                         preferred_element_type=_F32).astype(o_ref.dtype)


def _mm_norm_swiglu_kernel(x_ref, g_ref, wg_ref, wu_ref, o_ref, h_ref):
    h = _normed(x_ref, g_ref, h_ref)
    g = jnp.dot(h, wg_ref[...], preferred_element_type=_F32)
    u = jnp.dot(h, wu_ref[...], preferred_element_type=_F32)
    o_ref[...] = (g * jax.nn.sigmoid(g) * u).astype(o_ref.dtype)


def _mm_norm_t_kernel(w_ref, x_ref, g_ref, o_ref, h_ref):
    y = lax.dot_general(w_ref[...], _normed(x_ref, g_ref, h_ref), (((1,), (1,)), ((), ())),
                        preferred_element_type=_F32).astype(o_ref.dtype)
    tk = o_ref.shape[2]
    for r in range(o_ref.shape[0]):
        o_ref[r] = y[:, r * tk:(r + 1) * tk]


def _mm_residual_kernel(x_ref, w_ref, r_ref, o_ref, *, scale):
    y = jnp.dot(x_ref[...], w_ref[...], preferred_element_type=_F32)
    o_ref[...] = r_ref[...] + scale * y


def _tile(n, pref):
    return pref if n % pref == 0 else n


def _matmul_norm(x, g, w, out_dtype, bm=None, bn=None, col_block=0):
    m = x.shape[0]
    k, n = w.shape
    bm = _tile(m, bm or MM_BM)
    bn = _tile(n, bn or MM_BN)
    return pl.pallas_call(
        _mm_norm_kernel,
        grid=(m // bm, n // bn),
        in_specs=[pl.BlockSpec((bm, k), lambda i, j: (i, col_block)),
                  pl.BlockSpec((1, k), lambda i, j: (0, 0)),
                  pl.BlockSpec((k, bn), lambda i, j: (0, j))],
        out_specs=pl.BlockSpec((bm, bn), lambda i, j: (i, j)),
        out_shape=jax.ShapeDtypeStruct((m, n), out_dtype),
        scratch_shapes=[pltpu.VMEM((bm, k), MXU_DTYPE)],
        compiler_params=_params(("parallel", "arbitrary")),
        name="matmul_norm",
    )(x, g.reshape(1, k).astype(_F32), w)


def _matmul_norm_t(w_t, x, g, tk, out_dtype, bn=None, col_block=0, row_tiles=1):
    n, k = w_t.shape
    t = x.shape[0]
    bn = _tile(n, bn or MM_BN)
    rt = row_tiles if (t // tk) % row_tiles == 0 else 1
    return pl.pallas_call(
        _mm_norm_t_kernel,
        grid=(t // (rt * tk), n // bn),
        in_specs=[pl.BlockSpec((bn, k), lambda i, j: (j, 0)),
                  pl.BlockSpec((rt * tk, k), lambda i, j: (i, col_block)),
                  pl.BlockSpec((1, k), lambda i, j: (0, 0))],
        out_specs=pl.BlockSpec((rt, bn, tk), lambda i, j: (i, j, 0)),
        out_shape=jax.ShapeDtypeStruct((t // tk, n, tk), out_dtype),
        scratch_shapes=[pltpu.VMEM((rt * tk, k), MXU_DTYPE)],
        compiler_params=_params(("parallel", "arbitrary")),
        name="matmul_norm_t",
    )(w_t, x, g.reshape(1, k).astype(_F32))


def _matmul_norm_swiglu(x, g, w_in, layer=None, bm=None, bn=512):
    m, k = x.shape
    f = w_in.shape[-1] // 2
    bm = _tile(m, bm or MM_BM)
    bn = _tile(f, bn)
    nb = f // bn
    if layer is None:
        w_specs = [pl.BlockSpec((k, bn), lambda i, j: (0, j)),
                   pl.BlockSpec((k, bn), lambda i, j: (0, j + nb))]
    else:
        w_specs = [pl.BlockSpec((None, k, bn), lambda i, j: (layer, 0, j)),
                   pl.BlockSpec((None, k, bn), lambda i, j: (layer, 0, j + nb))]
    return pl.pallas_call(
        _mm_norm_swiglu_kernel,
        grid=(m // bm, nb),
        in_specs=[pl.BlockSpec((bm, k), lambda i, j: (i, 0)),
                  pl.BlockSpec((1, k), lambda i, j: (0, 0))] + w_specs,
        out_specs=pl.BlockSpec((bm, bn), lambda i, j: (i, j)),
        out_shape=jax.ShapeDtypeStruct((m, f), MXU_DTYPE),
        scratch_shapes=[pltpu.VMEM((bm, k), MXU_DTYPE)],
        compiler_params=_params(("parallel", "arbitrary")),
        name="matmul_norm_swiglu",
    )(x, g.reshape(1, k).astype(_F32), w_in, w_in)


def _matmul_residual(x, w, res, scale, layer=None, bn=512):
    m, k = x.shape
    n = w.shape[-1]
    bn = _tile(n, bn)
    bm = _tile(m, MM_BM)
    for cand in (2 * MM_BM,):
        windows = 2 * (cand * k * 2 + k * bn * 2 + 2 * cand * bn * 4)
        if m % cand == 0 and windows <= MM_RESIDUAL_VMEM_MB << 20:
            bm = cand
    if layer is None:
        w_spec = pl.BlockSpec((k, bn), lambda i, j: (0, j))
    else:
        w_spec = pl.BlockSpec((None, k, bn), lambda i, j: (layer, 0, j))
    return pl.pallas_call(
        functools.partial(_mm_residual_kernel, scale=scale),
        grid=(m // bm, n // bn),
        in_specs=[pl.BlockSpec((bm, k), lambda i, j: (i, 0)),
                  w_spec,
                  pl.BlockSpec((bm, bn), lambda i, j: (i, j))],
        out_specs=pl.BlockSpec((bm, bn), lambda i, j: (i, j)),
        out_shape=jax.ShapeDtypeStruct((m, n), _F32),
        compiler_params=_params(("parallel", "arbitrary")),
        name="matmul_residual",
    )(x, w, res)


def _ffn(x, norm_g, w_in_all, w_out_all, layer):
    act = _matmul_norm_swiglu(x, norm_g, w_in_all, layer)
    return _matmul_residual(act, w_out_all, x, 0.5, layer)


def _rope_tables(positions, d, half_width):
    inv = jnp.power(ROPE_THETA, -jnp.arange(0, d, 2, dtype=_F32) / d)
    ang = positions.astype(_F32).reshape(-1, 1) * inv
    pad = jnp.zeros((ang.shape[0], half_width - d // 2), _F32)
    cos, sin = jnp.cos(ang), jnp.sin(ang)
    ra = jnp.concatenate([cos, pad, cos, pad], axis=1)
    rb = jnp.concatenate([-sin, pad, sin, pad], axis=1)
    return ra, rb


def _softmax_step(s, vt, m_ref, acc_ref):
    m_prev = m_ref[...]
    m_new = jnp.maximum(m_prev, jnp.max(s, axis=0, keepdims=True))
    alpha = jnp.exp2(m_prev - m_new)
    p = jnp.exp2(s - m_new).astype(MXU_DTYPE)
    vt1 = jnp.concatenate([vt, jnp.ones((ONES_ROWS, vt.shape[1]), vt.dtype)], axis=0)
    acc_ref[...] = alpha * acc_ref[...] + jnp.dot(vt1, p, preferred_element_type=_F32)
    m_ref[...] = m_new


def _pipelined_sweep(n, score, update, sa_ref, sb_ref):
    sa_ref[...] = score(0, True)

    def pair(t):
        sb_ref[...] = score(t + 1, False)
        update(t, sa_ref[...])
        sa_ref[...] = score(t + 2, False)
        update(t + 1, sb_ref[...])

    def quad_body(p, carry):
        pair(4 * p)
        pair(4 * p + 2)
        return carry

    n_quads = (n - 1) // 4
    lax.fori_loop(0, n_quads, quad_body, 0)

    def pair_body(p, carry):
        pair(4 * n_quads + 2 * p)
        return carry

    n_pairs = (n - 1 - 4 * n_quads) // 2
    lax.fori_loop(0, n_pairs, pair_body, 0)
    t = 4 * n_quads + 2 * n_pairs

    @pl.when(t == n - 1)
    def _odd_tail():
        update(t, sa_ref[...])

    @pl.when(t < n - 1)
    def _even_tail():
        sb_ref[...] = score(t + 1, False)
        update(t, sa_ref[...])
        update(t + 1, sb_ref[...])


def _static_sweep(n, score, update, sa_ref, sb_ref, unroll=8):
    sa_ref[...] = score(0)
    bufs = (sa_ref, sb_ref)

    def run(base, count, last):
        for u in range(count):
            cur, nxt = bufs[u % 2], bufs[(u + 1) % 2]
            if not (last and u == count - 1):
                nxt[...] = score(base + u + 1)
            update(base + u, cur[...])

    n_loop = (n - 1) // unroll * unroll

    def body(p, carry):
        run(p * unroll, unroll, False)
        return carry

    lax.fori_loop(0, n_loop // unroll, body, 0)
    run(n_loop, n - n_loop, True)


def _softmax_finish(acc_ref, dv):
    acc = acc_ref[...]
    return acc[:dv, :] / acc[dv:dv + 1, :]


def _mla_attn_kernel(jt_ref, it_ref, q_ref, kn_ref, kr_ref, vt_ref, cos_ref, sin_ref, o_ref,
                     qs_ref, m_ref, acc_ref, sa_ref, sb_ref, *, nq, tq, n_off):
    half = A_ROPE // 2
    r1 = slice(A_NOPE, A_NOPE + half)
    r2 = slice(A_NOPE + LANES // 2, A_NOPE + LANES // 2 + half)

    def prepare(i, carry):
        qt = q_ref[i]
        x1 = qt[r1, :].astype(_F32)
        x2 = qt[r2, :].astype(_F32)
        cos = cos_ref[i]
        sin = sin_ref[i]
        qs_ref[i] = qt
        qs_ref[i, r1, :] = (x1 * cos - x2 * sin).astype(qs_ref.dtype)
        qs_ref[i, r2, :] = (x1 * sin + x2 * cos).astype(qs_ref.dtype)
        m_ref[i] = jnp.full((1, tq), -jnp.inf, _F32)
        acc_ref[i] = jnp.zeros(acc_ref.shape[1:], _F32)
        return carry

    lax.fori_loop(0, nq, prepare, 0)

    def scores(j, i):
        off = pl.multiple_of(j * tq, tq)
        kc = jnp.concatenate([kn_ref[pl.ds(off, tq), :], kr_ref[pl.ds(off, tq), :]], axis=1)
        return jnp.dot(kc, qs_ref[i], preferred_element_type=_F32)

    def update(j, i, s):
        _softmax_step(s, vt_ref[j], m_ref.at[i], acc_ref.at[i])

    kchunk = lax.broadcasted_iota(_I32, (tq, tq), 0) // CHUNK
    qchunk = lax.broadcasted_iota(_I32, (tq, tq), 1) // CHUNK
    _static_sweep(nq, lambda t: jnp.where(qchunk >= kchunk, scores(t, t), -jnp.inf),
                  lambda t, s: update(t, t, s), sa_ref, sb_ref)
    if n_off:
        _static_sweep(n_off, lambda t: scores(jt_ref[t], it_ref[t]),
                      lambda t, s: update(jt_ref[t], it_ref[t], s), sa_ref, sb_ref)

    def finish(i, carry):
        off = pl.multiple_of(i * tq, tq)
        o_ref[pl.ds(off, tq), :] = _softmax_finish(acc_ref.at[i], A_V).T.astype(o_ref.dtype)
        return carry

    lax.fori_loop(0, nq, finish, 0)


def _mla_attention(qt, kn, kr, vt, cos_t, sin_t, batch, seq, tq):
    nq = seq // tq
    pairs = [(j, i) for j in range(nq) for i in range(j + 1, nq)]
    jt = jnp.asarray([p[0] for p in pairs] or [0], _I32)
    it = jnp.asarray([p[1] for p in pairs] or [0], _I32)
    half = A_ROPE // 2
    smem = pl.BlockSpec(memory_space=pltpu.SMEM)
    return pl.pallas_call(
        functools.partial(_mla_attn_kernel, nq=nq, tq=tq, n_off=len(pairs)),
        grid=(batch, A_HEADS),
        in_specs=[smem, smem,
                  pl.BlockSpec((nq, 2 * LANES, tq), lambda b, h: (b, h, 0)),
                  pl.BlockSpec((seq, LANES), lambda b, h: (b, h)),
                  pl.BlockSpec((seq, LANES), lambda b, h: (b, 0)),
                  pl.BlockSpec((nq, A_V, tq), lambda b, h: (b, h, 0)),
                  pl.BlockSpec((nq, half, tq), lambda b, h: (b, 0, 0)),
                  pl.BlockSpec((nq, half, tq), lambda b, h: (b, 0, 0))],
        out_specs=pl.BlockSpec((seq, A_V), lambda b, h: (b, h)),
        out_shape=jax.ShapeDtypeStruct((batch * seq, A_HEADS * A_V), MXU_DTYPE),
        scratch_shapes=[pltpu.VMEM((nq, 2 * LANES, tq), MXU_DTYPE),
                        pltpu.VMEM((nq, 1, tq), _F32),
                        pltpu.VMEM((nq, A_V + ONES_ROWS, tq), _F32),
                        pltpu.VMEM((tq, tq), _F32),
                        pltpu.VMEM((tq, tq), _F32)],
        compiler_params=_params(("parallel", "parallel")),
        name="mla_attention",
    )(jt, it, qt, kn, kr, vt, cos_t, sin_t)


def _pad_rope_cols(w, n_heads, nope):
    k = w.shape[0]
    w = w.reshape(k, n_heads, nope + A_ROPE)
    half = A_ROPE // 2
    z = jnp.zeros((k, n_heads, LANES // 2 - half), w.dtype)
    w = jnp.concatenate([w[..., :nope], w[..., nope:nope + half], z, w[..., nope + half:], z], axis=-1)
    return w.reshape(k, n_heads * (nope + LANES))


def _mla_mixer(x, positions, mix_g, w_in, q_norm, kv_norm, w_uq, w_ukv, w_out, batch, seq):
    assert A_NOPE == LANES and A_V == LANES and A_ROPE == LANES // 2
    tq = min(MLA_TQ, seq)
    n_lat = A_Q_LORA + A_KV_LORA
    w_in_p = jnp.concatenate([w_in[:, :n_lat], _pad_rope_cols(w_in[:, n_lat:], 1, 0)], axis=1)
    lat = _matmul_norm(x, mix_g, w_in_p.astype(MXU_DTYPE), _F32, bm=MM_BM // 2,
                       bn=w_in_p.shape[1])
    ra, rb = _rope_tables(positions, A_ROPE, LANES // 2)
    kr = _rope(lat, ra, rb, MXU_DTYPE, col_block=n_lat // LANES, width=LANES)
    qscale = (A_NOPE + A_ROPE) ** -0.5 * LOG2E
    w_qt = (_pad_rope_cols(w_uq, A_HEADS, A_NOPE) * qscale).T.astype(MXU_DTYPE)
    qt = _matmul_norm_t(w_qt, lat, q_norm, tq, MXU_DTYPE, bn=2 * MM_BN, row_tiles=2)
    w_ukv = w_ukv.reshape(A_KV_LORA, A_HEADS, A_NOPE + A_V)
    w_kn = w_ukv[..., :A_NOPE].reshape(A_KV_LORA, A_HEADS * A_NOPE)
    w_vt = w_ukv[..., A_NOPE:].reshape(A_KV_LORA, A_HEADS * A_V).T
    kv_blk = A_Q_LORA // A_KV_LORA
    kn = _matmul_norm(lat, kv_norm, w_kn.astype(MXU_DTYPE), MXU_DTYPE, bm=2 * MM_BM, bn=2 * MM_BN,
                      col_block=kv_blk)
    vt = _matmul_norm_t(w_vt.astype(MXU_DTYPE), lat, kv_norm, tq, MXU_DTYPE, bn=2 * MM_BN,
                        col_block=kv_blk, row_tiles=2)
    half = A_ROPE // 2
    cos_t = ra[:, :half].reshape(-1, tq, half).transpose(0, 2, 1)
    sin_t = rb[:, LANES // 2:LANES // 2 + half].reshape(-1, tq, half).transpose(0, 2, 1)
    o = _mla_attention(qt, kn, kr, vt, cos_t, sin_t, batch, seq, tq)
    return _matmul_residual(o, w_out.astype(MXU_DTYPE), x, 1.0)


def _dsa_kernel(qi_ref, wit_ref, ki_ref, ra_ref, rb_ref, q_ref, k_ref, vt_ref, o_ref,
                qit_ref, key_ref, thr_ref, qt_ref, m_ref, acc_ref, sa_ref, sb_ref,
                *, tq, tk, topk, group, wi_scale, qscale):
    i = pl.program_id(1)
    h = pl.program_id(2)
    nkb = (i + 1) * (tq // tk)
    ra = ra_ref[...]
    rb = rb_ref[...]

    @pl.when(h == 0)
    def _select():
        for hd in range(B_IDX_HEADS):
            sl = slice(hd * LANES, (hd + 1) * LANES)
            x = qi_ref[:, sl].astype(_F32)
            x = x * ra + _rot_half(x) * rb
            qit_ref[sl, :] = x.T.astype(qit_ref.dtype)
        wit = wit_ref[...] * wi_scale

        def score_tile(j, carry):
            off = pl.multiple_of(j * tk, tk)
            ki = ki_ref[pl.ds(off, tk), :]
            sc = jnp.zeros((tk, tq), _F32)
            for hd in range(B_IDX_HEADS):
                d = jnp.dot(ki, qit_ref[hd * LANES:(hd + 1) * LANES, :],
                            preferred_element_type=_F32)
                sc = sc + jnp.maximum(d, 0.0) * wit[hd:hd + 1, :]
            kchunk = (off + lax.broadcasted_iota(_I32, (tk, tq), 0)) // CHUNK
            qchunk = (i * tq + lax.broadcasted_iota(_I32, (tk, tq), 1)) // CHUNK
            sc = jnp.where(qchunk >= kchunk, sc, -jnp.inf)
            bits = pltpu.bitcast(sc, _I32)
            key_ref[j] = jnp.where(bits >= 0, bits, bits ^ 0x7FFFFFFF)
            return carry

        lax.fori_loop(0, nkb, score_tile, 0)

        def count(pred):
            def tile(j, acc):
                hit = jnp.where(pred(key_ref[j]), 1.0, 0.0)
                return acc + jnp.sum(hit.reshape(tk // SUBLANES, SUBLANES, tq), axis=0)

            def four_tiles(p, acc):
                for u in range(4):
                    acc = tile(4 * p + u, acc)
                return acc

            acc = lax.fori_loop(0, nkb // 4, four_tiles, jnp.zeros((SUBLANES, tq), _F32))
            acc = lax.fori_loop(nkb // 4 * 4, nkb, tile, acc)
            return jnp.sum(acc, axis=0, keepdims=True)

        def bisect(b, t):
            cand = t + jnp.left_shift(jnp.int32(1), 31 - b)
            return jnp.where(count(lambda kt: kt >= cand) >= topk, cand, t)

        t = lax.fori_loop(0, 32, bisect, jnp.full((1, tq), _INT_MIN, _I32))

        need = topk - count(lambda kt: kt > t)
        lower = jnp.where(lax.broadcasted_iota(_I32, (tk, tk), 0)
                          >= lax.broadcasted_iota(_I32, (tk, tk), 1), 1.0, 0.0).astype(MXU_DTYPE)

        def demote_tile(j, seen):
            kt = key_ref[j]
            eq = kt == t
            rank = seen + jnp.dot(lower, jnp.where(eq, 1.0, 0.0).astype(MXU_DTYPE),
                                  preferred_element_type=_F32)
            key_ref[j] = jnp.where(eq, jnp.where(rank > need, t - 1, kt), kt)
            return rank[tk - 1:tk, :]

        lax.fori_loop(0, nkb, demote_tile, jnp.zeros((1, tq), _F32))
        thr_ref[...] = jnp.maximum(t, _KEY_OF_NEG_INF + 1)

    q = q_ref[...]
    for g in range(group):
        x = q[:, g * LANES:(g + 1) * LANES].astype(_F32)
        x = (x * ra + _rot_half(x) * rb) * qscale
        qt_ref[:, g * tq:(g + 1) * tq] = x.T.astype(qt_ref.dtype)
    m_ref[...] = jnp.full(m_ref.shape, _NEG_BIG, _F32)
    acc_ref[...] = jnp.zeros(acc_ref.shape, _F32)
    thr = thr_ref[...]

    def score(j, first):
        off = pl.multiple_of(j * tk, tk)
        bias = jnp.where(key_ref[j] >= thr, 0.0, _NEG_BIG)
        bias = jnp.concatenate([bias] * group, axis=1)
        return jnp.dot(k_ref[pl.ds(off, tk), :], qt_ref[...], preferred_element_type=_F32) + bias

    def update(j, s):
        _softmax_step(s, vt_ref[j], m_ref, acc_ref)

    _pipelined_sweep(nkb, score, update, sa_ref, sb_ref)
    out = _softmax_finish(acc_ref, B_HEAD_DIM)
    for g in range(group):
        o_ref[:, g * LANES:(g + 1) * LANES] = out[:, g * tq:(g + 1) * tq].T.astype(o_ref.dtype)


def _dsa_attention(proj, k_roped, vt, ki, wit, ra, rb, batch, seq, tq, tk):
    nq = seq // tq
    group = B_HEADS // B_KV_HEADS
    topk = min(B_TOPK_MAX, seq // 4)
    qi_w = B_IDX_HEADS * B_IDX_DIM
    q_w = group * B_HEAD_DIM
    q_blk0 = qi_w // q_w
    kern = functools.partial(
        _dsa_kernel, tq=tq, tk=tk, topk=float(topk), group=group,
        wi_scale=B_IDX_HEADS ** -0.5 * B_IDX_DIM ** -0.5, qscale=B_HEAD_DIM ** -0.5 * LOG2E)
    return pl.pallas_call(
        kern,
        grid=(batch, nq, B_KV_HEADS),
        in_specs=[pl.BlockSpec((tq, qi_w), lambda b, i, h: (b * nq + i, 0)),
                  pl.BlockSpec((B_IDX_HEADS, tq), lambda b, i, h: (0, b * nq + i)),
                  pl.BlockSpec((seq, LANES), lambda b, i, h: (b, 0)),
                  pl.BlockSpec((tq, LANES), lambda b, i, h: (b * nq + i, 0)),
                  pl.BlockSpec((tq, LANES), lambda b, i, h: (b * nq + i, 0)),
                  pl.BlockSpec((tq, q_w), lambda b, i, h: (b * nq + i, q_blk0 + h)),
                  pl.BlockSpec((seq, B_HEAD_DIM), lambda b, i, h: (b, h)),
                  pl.BlockSpec((seq // tk, B_HEAD_DIM, tk), lambda b, i, h: (b, h, 0))],
        out_specs=pl.BlockSpec((tq, q_w), lambda b, i, h: (b * nq + i, h)),
        out_shape=jax.ShapeDtypeStruct((batch * seq, B_HEADS * B_HEAD_DIM), MXU_DTYPE),
        scratch_shapes=[pltpu.VMEM((qi_w, tq), MXU_DTYPE),
                        pltpu.VMEM((seq // tk, tk, tq), _I32),
                        pltpu.VMEM((1, tq), _I32),
                        pltpu.VMEM((B_HEAD_DIM, group * tq), MXU_DTYPE),
                        pltpu.VMEM((1, group * tq), _F32),
                        pltpu.VMEM((B_HEAD_DIM + ONES_ROWS, group * tq), _F32),
                        pltpu.VMEM((tk, group * tq), _F32),
                        pltpu.VMEM((tk, group * tq), _F32)],
        compiler_params=_params(("parallel", "arbitrary", "arbitrary")),
        name="dsa_attention",
    )(proj, wit, ki, ra, rb, proj, k_roped, vt)


def _dsa_mixer(x, positions, mix_g, w_in, idx_k_norm, w_out, batch, seq):
    assert B_HEAD_DIM == LANES and B_IDX_DIM == LANES and B_IDX_HEADS % SUBLANES == 0
    tq = min(DSA_TQ, seq)
    tk = min(DSA_TK, tq)
    nq, nk = B_HEADS * B_HEAD_DIM, B_KV_HEADS * B_HEAD_DIM
    nqi = B_IDX_HEADS * B_IDX_DIM
    o_q, o_k, o_v, o_qi, o_ki, o_wi = np.cumsum([0, nq, nk, nk, nqi, B_IDX_DIM])
    w_main = jnp.concatenate([w_in[:, o_qi:o_ki], w_in[:, o_q:o_v]], axis=1)
    w_side = jnp.concatenate(
        [w_in[:, o_ki:o_wi], w_in[:, o_wi:],
         jnp.zeros((w_in.shape[0], LANES - B_IDX_HEADS), w_in.dtype)], axis=1)
    proj = _matmul_norm(x, mix_g, w_main.astype(MXU_DTYPE), MXU_DTYPE)
    side = _matmul_norm(x, mix_g, w_side.astype(MXU_DTYPE), _F32, bn=2 * LANES)
    vt = _matmul_norm_t(w_in[:, o_v:o_qi].T.astype(MXU_DTYPE), x, mix_g, tk, MXU_DTYPE)
    ra, rb = _rope_tables(positions, B_HEAD_DIM, LANES // 2)
    k_roped = _rope(proj, ra, rb, MXU_DTYPE, col_block=(nqi + nq) // nk, width=nk)
    ki = _rmsnorm_rope(side, idx_k_norm, ra, rb, MXU_DTYPE, col_block=0)
    wit = side[:, LANES:LANES + B_IDX_HEADS].T
    o = _dsa_attention(proj, k_roped, vt, ki, wit, ra, rb, batch, seq, tq, tk)
    return _matmul_residual(o, w_out.astype(MXU_DTYPE), x, 1.0)


def _log_sigmoid(x):
    return jnp.minimum(x, 0.0) - jnp.log(1.0 + jnp.exp(-jnp.abs(x)))


def _mlstm_kernel(bias_ref, q_ref, k_ref, v_ref, o_ref, grow_ref, gcol_ref, hn_ref, y_ref,
                  c_ref, n_ref, m_ref, *, n_heads, dk, dv):
    @pl.when(pl.program_id(1) == 0)
    def _init():
        c_ref[...] = jnp.zeros(c_ref.shape, _F32)
        n_ref[...] = jnp.zeros(n_ref.shape, _F32)
        m_ref[...] = jnp.zeros(m_ref.shape, _F32)

    grow = grow_ref[0]
    gcol = gcol_ref[0]
    rows = lax.broadcasted_iota(_I32, (CHUNK, CHUNK), 0)
    cols = lax.broadcasted_iota(_I32, (CHUNK, CHUNK), 1)
    tril = rows >= cols
    triu = rows <= cols

    for h in range(n_heads):
        b_i = bias_ref[h]
        b_f = bias_ref[n_heads + h]
        li_row = grow[h:h + 1, :] + b_i
        lf_row = _log_sigmoid(grow[n_heads + h:n_heads + h + 1, :] + b_f)
        li_col = gcol[:, h:h + 1] + b_i
        lf_col = _log_sigmoid(gcol[:, n_heads + h:n_heads + h + 1] + b_f)

        b_col = jnp.sum(jnp.where(tril, lf_row, 0.0), axis=1, keepdims=True)
        b_row = jnp.sum(jnp.where(triu, lf_col, 0.0), axis=0, keepdims=True)
        b_end = jnp.sum(lf_row, axis=1, keepdims=True)

        m_prev = m_ref[h]
        a = b_col + m_prev
        d = jnp.where(tril, b_col - b_row + li_row, -jnp.inf)
        m_t = jnp.maximum(a, jnp.max(d, axis=1, keepdims=True))
        inter = jnp.exp(a - m_t)

        q = q_ref[:, h * dk:(h + 1) * dk]
        k = k_ref[:, h * dk:(h + 1) * dk]
        v = v_ref[:, h * dv:(h + 1) * dv]
        qf = q.astype(_F32) * dk ** -0.5
        qs = qf.astype(MXU_DTYPE)
        qk = lax.dot_general(qs, k, (((1,), (1,)), ((), ())),
                             preferred_element_type=_F32) * jnp.exp(d - m_t)
        ct = c_ref[h]
        num = inter * jnp.dot(qs, ct.astype(MXU_DTYPE), preferred_element_type=_F32) + jnp.dot(
            qk.astype(MXU_DTYPE), v, preferred_element_type=_F32)
        n_row = n_ref[h]
        den = inter * jnp.sum(qf * n_row, axis=1, keepdims=True) + jnp.sum(qk, axis=1, keepdims=True)
        hs = num / jnp.maximum(jnp.abs(den), jnp.exp(-m_t))

        g_col = b_end - b_col + li_col
        m_new = jnp.maximum(b_end + m_prev, jnp.max(g_col, axis=0, keepdims=True))
        decay = jnp.exp(b_end + m_prev - m_new)
        wk = jnp.exp(g_col - m_new)
        vw = (v.astype(_F32) * wk).astype(MXU_DTYPE)
        c_ref[h] = decay * ct + lax.dot_general(k, vw, (((0,), (0,)), ((), ())),
                                                preferred_element_type=_F32)
        n_ref[h] = decay * n_row + jnp.sum(wk * k.astype(_F32), axis=0, keepdims=True)
        m_ref[h] = m_new

        hs = hs * lax.rsqrt(jnp.mean(hs * hs, axis=1, keepdims=True) + NORM_EPS)
        gate = jax.nn.sigmoid(o_ref[:, h * dv:(h + 1) * dv].astype(_F32))
        y_ref[:, h * dv:(h + 1) * dv] = (hs * hn_ref[:, h * dv:(h + 1) * dv] * gate).astype(y_ref.dtype)


def _mlstm_scan(proj, g_row, g_col, gate_bias, head_norm, batch, seq, dk, dv):
    nc = seq // CHUNK
    hq = C_HEADS
    qk_w, v_w = hq * dk, hq * dv
    v_blk0 = 2 * qk_w // v_w
    return pl.pallas_call(
        functools.partial(_mlstm_kernel, n_heads=hq, dk=dk, dv=dv),
        grid=(batch, nc),
        in_specs=[pl.BlockSpec(memory_space=pltpu.SMEM),
                  pl.BlockSpec((CHUNK, qk_w), lambda b, c: (b * nc + c, 0)),
                  pl.BlockSpec((CHUNK, qk_w), lambda b, c: (b * nc + c, 1)),
                  pl.BlockSpec((CHUNK, v_w), lambda b, c: (b * nc + c, v_blk0)),
                  pl.BlockSpec((CHUNK, v_w), lambda b, c: (b * nc + c, v_blk0 + 1)),
                  pl.BlockSpec((1, 2 * hq, CHUNK), lambda b, c: (b * nc + c, 0, 0)),
                  pl.BlockSpec((1, CHUNK, 2 * hq), lambda b, c: (b * nc + c, 0, 0)),
                  pl.BlockSpec((1, v_w), lambda b, c: (0, 0))],
        out_specs=pl.BlockSpec((CHUNK, v_w), lambda b, c: (b * nc + c, 0)),
        out_shape=jax.ShapeDtypeStruct((batch * seq, v_w), MXU_DTYPE),
        scratch_shapes=[pltpu.VMEM((hq, dk, dv), _F32),
                        pltpu.VMEM((hq, 1, dk), _F32),
                        pltpu.VMEM((hq, 1, 1), _F32)],
        compiler_params=_params(("parallel", "arbitrary")),
        name="mlstm_scan",
    )(gate_bias.astype(_F32), proj, proj, proj, proj, g_row, g_col,
      head_norm.reshape(1, v_w).astype(_F32))


def _mlstm_mixer(x, mix_g, w_in, gate_bias, head_norm, w_out, batch, seq):
    d_model = x.shape[1]
    dv = d_model // C_HEADS
    dk = dv // 2
    hq = C_HEADS
    n_main = 2 * hq * dk + 2 * hq * dv
    w_gate = jnp.concatenate(
        [w_in[:, n_main:], jnp.zeros((w_in.shape[0], LANES - 2 * hq), w_in.dtype)], axis=1)
    proj = _matmul_norm(x, mix_g, w_in[:, :n_main].astype(MXU_DTYPE), MXU_DTYPE)
    gates = _matmul_norm(x, mix_g, w_gate.astype(MXU_DTYPE), _F32, bn=LANES)
    nc = seq // CHUNK
    g_col = gates[:, :2 * hq].reshape(batch * nc, CHUNK, 2 * hq)
    g_row = g_col.transpose(0, 2, 1)
    y = _mlstm_scan(proj, g_row, g_col, gate_bias, head_norm, batch, seq, dk, dv)
    return _matmul_residual(y, w_out.astype(MXU_DTYPE), x, 1.0)


def kernel(x, positions, ffn1_norm, ffn1_w_in, ffn1_w_out, mix_norm, ffn2_norm, ffn2_w_in, ffn2_w_out,
           a_w_in, a_q_norm, a_kv_norm, a_w_uq, a_w_ukv, a_w_out,
           b_w_in, b_idx_k_norm, b_w_out,
           c_w_in, c_gate_bias, c_head_norm, c_w_out, final_norm):
    batch, seq, d_model = x.shape
    depth = ffn1_norm.shape[0]
    x = x.reshape(batch * seq, d_model)
    ffn1_w_in, ffn1_w_out, ffn2_w_in, ffn2_w_out = (
        w.astype(MXU_DTYPE) for w in (ffn1_w_in, ffn1_w_out, ffn2_w_in, ffn2_w_out))
    for i in range(depth):
        x = _ffn(x, ffn1_norm[i], ffn1_w_in, ffn1_w_out, i)
        kind, j = i % N_MIXERS, i // N_MIXERS
        if kind == 0:
            x = _mla_mixer(x, positions, mix_norm[i], a_w_in[j], a_q_norm[j], a_kv_norm[j],
                           a_w_uq[j], a_w_ukv[j], a_w_out[j], batch, seq)
        elif kind == 1:
            x = _dsa_mixer(x, positions, mix_norm[i], b_w_in[j], b_idx_k_norm[j], b_w_out[j], batch, seq)
        else:
            x = _mlstm_mixer(x, mix_norm[i], c_w_in[j], c_gate_bias[j], c_head_norm[j], c_w_out[j],
                             batch, seq)
        x = _ffn(x, ffn2_norm[i], ffn2_w_in, ffn2_w_out, i)
    return _rmsnorm(x, final_norm, _F32).reshape(batch, seq, d_model)
```

```python
import functools

import jax
import jax.numpy as jnp
import numpy as np
from jax import lax
from jax.experimental import pallas as pl
from jax.experimental.pallas import tpu as pltpu

CHUNK = 64
ROPE_THETA = 10000.0
NORM_EPS = 1e-6
D_FF = 6144

A_HEADS = 64
A_Q_LORA = 1024
A_KV_LORA = 512
A_NOPE = 128
A_ROPE = 64
A_V = 128

B_HEADS = 32
B_KV_HEADS = 8
B_HEAD_DIM = 128
B_IDX_HEADS = 32
B_IDX_DIM = 128
B_TOPK_MAX = 256

C_HEADS = 8
N_MIXERS = 3

LANES = 128
SUBLANES = 8
ONES_ROWS = 16
LOG2E = 1.4426950408889634
MXU_DTYPE = jnp.bfloat16
VMEM_LIMIT_MB = 56

ROW_TILE = 256
MM_BM = 512
MM_BN = 1024
MM_RESIDUAL_VMEM_MB = 48
MLA_TQ = 512
DSA_TQ = 256
DSA_TK = 256

_F32 = jnp.float32
_I32 = jnp.int32
_NEG_BIG = -1e30
_INT_MIN = -(2 ** 31)
_KEY_OF_NEG_INF = -2139095041


def _params(semantics):
    return pltpu.CompilerParams(dimension_semantics=semantics,
                                vmem_limit_bytes=VMEM_LIMIT_MB << 20)


def _rot_half(x):
    return pltpu.roll(x, LANES // 2, 1)


def _rmsnorm_kernel(x_ref, g_ref, o_ref):
    x = x_ref[...].astype(_F32)
    y = x * lax.rsqrt(jnp.mean(x * x, axis=-1, keepdims=True) + NORM_EPS)
    o_ref[...] = (y * g_ref[...]).astype(o_ref.dtype)


def _rmsnorm(x, g, out_dtype, col_block=0, width=None):
    t = x.shape[0]
    width = x.shape[1] if width is None else width
    bt = min(ROW_TILE, t)
    return pl.pallas_call(
        _rmsnorm_kernel,
        grid=(t // bt,),
        in_specs=[pl.BlockSpec((bt, width), lambda i: (i, col_block)),
                  pl.BlockSpec((1, width), lambda i: (0, 0))],
        out_specs=pl.BlockSpec((bt, width), lambda i: (i, 0)),
        out_shape=jax.ShapeDtypeStruct((t, width), out_dtype),
        compiler_params=_params(("parallel",)),
        name="rmsnorm",
    )(x, g.reshape(1, width).astype(_F32))


def _rope_kernel(x_ref, ra_ref, rb_ref, o_ref, *, n_blocks):
    ra = ra_ref[...]
    rb = rb_ref[...]
    for c in range(n_blocks):
        sl = slice(c * LANES, (c + 1) * LANES)
        x = x_ref[:, sl].astype(_F32)
        o_ref[:, sl] = (x * ra + _rot_half(x) * rb).astype(o_ref.dtype)


def _rope(x, ra, rb, out_dtype, col_block=0, width=None):
    t = x.shape[0]
    width = x.shape[1] if width is None else width
    bt = min(ROW_TILE, t)
    return pl.pallas_call(
        functools.partial(_rope_kernel, n_blocks=width // LANES),
        grid=(t // bt,),
        in_specs=[pl.BlockSpec((bt, width), lambda i: (i, col_block)),
                  pl.BlockSpec((bt, LANES), lambda i: (i, 0)),
                  pl.BlockSpec((bt, LANES), lambda i: (i, 0))],
        out_specs=pl.BlockSpec((bt, width), lambda i: (i, 0)),
        out_shape=jax.ShapeDtypeStruct((t, width), out_dtype),
        compiler_params=_params(("parallel",)),
        name="rope",
    )(x, ra, rb)


def _rmsnorm_rope_kernel(x_ref, g_ref, ra_ref, rb_ref, o_ref):
    x = x_ref[...].astype(_F32)
    y = x * lax.rsqrt(jnp.mean(x * x, axis=-1, keepdims=True) + NORM_EPS) * g_ref[...]
    o_ref[...] = (y * ra_ref[...] + _rot_half(y) * rb_ref[...]).astype(o_ref.dtype)


def _rmsnorm_rope(x, g, ra, rb, out_dtype, col_block=0):
    t = x.shape[0]
    bt = min(ROW_TILE, t)
    return pl.pallas_call(
        _rmsnorm_rope_kernel,
        grid=(t // bt,),
        in_specs=[pl.BlockSpec((bt, LANES), lambda i: (i, col_block)),
                  pl.BlockSpec((1, LANES), lambda i: (0, 0)),
                  pl.BlockSpec((bt, LANES), lambda i: (i, 0)),
                  pl.BlockSpec((bt, LANES), lambda i: (i, 0))],
        out_specs=pl.BlockSpec((bt, LANES), lambda i: (i, 0)),
        out_shape=jax.ShapeDtypeStruct((t, LANES), out_dtype),
        compiler_params=_params(("parallel",)),
        name="rmsnorm_rope",
    )(x, g.reshape(1, LANES).astype(_F32), ra, rb)


def _normed(x_ref, g_ref, h_ref):
    @pl.when(pl.program_id(1) == 0)
    def _normalise():
        x = x_ref[...].astype(_F32)
        y = x * lax.rsqrt(jnp.mean(x * x, axis=-1, keepdims=True) + NORM_EPS)
        h_ref[...] = (y * g_ref[...]).astype(h_ref.dtype)

    return h_ref[...]


def _mm_norm_kernel(x_ref, g_ref, w_ref, o_ref, h_ref):
    o_ref[...] = jnp.dot(_normed(x_ref, g_ref, h_ref), w_ref[...],
                         preferred_element_type=_F32).astype(o_ref.dtype)


def _mm_norm_swiglu_kernel(x_ref, g_ref, wg_ref, wu_ref, o_ref, h_ref):
    h = _normed(x_ref, g_ref, h_ref)
    g = jnp.dot(h, wg_ref[...], preferred_element_type=_F32)
    u = jnp.dot(h, wu_ref[...], preferred_element_type=_F32)
    o_ref[...] = (g * jax.nn.sigmoid(g) * u).astype(o_ref.dtype)


def _mm_norm_t_kernel(w_ref, x_ref, g_ref, o_ref, h_ref):
    y = lax.dot_general(w_ref[...], _normed(x_ref, g_ref, h_ref), (((1,), (1,)), ((), ())),
                        preferred_element_type=_F32).astype(o_ref.dtype)
    tk = o_ref.shape[2]
    for r in range(o_ref.shape[0]):
        o_ref[r] = y[:, r * tk:(r + 1) * tk]


def _mm_residual_kernel(x_ref, w_ref, r_ref, o_ref, *, scale):
    y = jnp.dot(x_ref[...], w_ref[...], preferred_element_type=_F32)
    o_ref[...] = r_ref[...] + scale * y


def _tile(n, pref):
    return pref if n % pref == 0 else n


def _matmul_norm(x, g, w, out_dtype, bm=None, bn=None, col_block=0):
    m = x.shape[0]
    k, n = w.shape
    bm = _tile(m, bm or MM_BM)
    bn = _tile(n, bn or MM_BN)
    return pl.pallas_call(
        _mm_norm_kernel,
        grid=(m // bm, n // bn),
        in_specs=[pl.BlockSpec((bm, k), lambda i, j: (i, col_block)),
                  pl.BlockSpec((1, k), lambda i, j: (0, 0)),
                  pl.BlockSpec((k, bn), lambda i, j: (0, j))],
        out_specs=pl.BlockSpec((bm, bn), lambda i, j: (i, j)),
        out_shape=jax.ShapeDtypeStruct((m, n), out_dtype),
        scratch_shapes=[pltpu.VMEM((bm, k), MXU_DTYPE)],
        compiler_params=_params(("parallel", "arbitrary")),
        name="matmul_norm",
    )(x, g.reshape(1, k).astype(_F32), w)


def _matmul_norm_t(w_t, x, g, tk, out_dtype, bn=None, col_block=0, row_tiles=1):
    n, k = w_t.shape
    t = x.shape[0]
    bn = _tile(n, bn or MM_BN)
    rt = row_tiles if (t // tk) % row_tiles == 0 else 1
    return pl.pallas_call(
        _mm_norm_t_kernel,
        grid=(t // (rt * tk), n // bn),
        in_specs=[pl.BlockSpec((bn, k), lambda i, j: (j, 0)),
                  pl.BlockSpec((rt * tk, k), lambda i, j: (i, col_block)),
                  pl.BlockSpec((1, k), lambda i, j: (0, 0))],
        out_specs=pl.BlockSpec((rt, bn, tk), lambda i, j: (i, j, 0)),
        out_shape=jax.ShapeDtypeStruct((t // tk, n, tk), out_dtype),
        scratch_shapes=[pltpu.VMEM((rt * tk, k), MXU_DTYPE)],
        compiler_params=_params(("parallel", "arbitrary")),
        name="matmul_norm_t",
    )(w_t, x, g.reshape(1, k).astype(_F32))


def _matmul_norm_swiglu(x, g, w_in, layer=None, bm=None, bn=512):
    m, k = x.shape
    f = w_in.shape[-1] // 2
    bm = _tile(m, bm or MM_BM)
    bn = _tile(f, bn)
    nb = f // bn
    if layer is None:
        w_specs = [pl.BlockSpec((k, bn), lambda i, j: (0, j)),
                   pl.BlockSpec((k, bn), lambda i, j: (0, j + nb))]
    else:
        w_specs = [pl.BlockSpec((None, k, bn), lambda i, j: (layer, 0, j)),
                   pl.BlockSpec((None, k, bn), lambda i, j: (layer, 0, j + nb))]
    return pl.pallas_call(
        _mm_norm_swiglu_kernel,
        grid=(m // bm, nb),
        in_specs=[pl.BlockSpec((bm, k), lambda i, j: (i, 0)),
                  pl.BlockSpec((1, k), lambda i, j: (0, 0))] + w_specs,
        out_specs=pl.BlockSpec((bm, bn), lambda i, j: (i, j)),
        out_shape=jax.ShapeDtypeStruct((m, f), MXU_DTYPE),
        scratch_shapes=[pltpu.VMEM((bm, k), MXU_DTYPE)],
        compiler_params=_params(("parallel", "arbitrary")),
        name="matmul_norm_swiglu",
    )(x, g.reshape(1, k).astype(_F32), w_in, w_in)


def _matmul_residual(x, w, res, scale, layer=None, bn=512):
    m, k = x.shape
    n = w.shape[-1]
    bn = _tile(n, bn)
    bm = _tile(m, MM_BM)
    for cand in (2 * MM_BM,):
        windows = 2 * (cand * k * 2 + k * bn * 2 + 2 * cand * bn * 4)
        if m % cand == 0 and windows <= MM_RESIDUAL_VMEM_MB << 20:
            bm = cand
    if layer is None:
        w_spec = pl.BlockSpec((k, bn), lambda i, j: (0, j))
    else:
        w_spec = pl.BlockSpec((None, k, bn), lambda i, j: (layer, 0, j))
    return pl.pallas_call(
        functools.partial(_mm_residual_kernel, scale=scale),
        grid=(m // bm, n // bn),
        in_specs=[pl.BlockSpec((bm, k), lambda i, j: (i, 0)),
                  w_spec,
                  pl.BlockSpec((bm, bn), lambda i, j: (i, j))],
        out_specs=pl.BlockSpec((bm, bn), lambda i, j: (i, j)),
        out_shape=jax.ShapeDtypeStruct((m, n), _F32),
        compiler_params=_params(("parallel", "arbitrary")),
        name="matmul_residual",
    )(x, w, res)


def _ffn(x, norm_g, w_in_all, w_out_all, layer):
    act = _matmul_norm_swiglu(x, norm_g, w_in_all, layer)
    return _matmul_residual(act, w_out_all, x, 0.5, layer)


def _rope_tables(positions, d, half_width):
    inv = jnp.power(ROPE_THETA, -jnp.arange(0, d, 2, dtype=_F32) / d)
    ang = positions.astype(_F32).reshape(-1, 1) * inv
    pad = jnp.zeros((ang.shape[0], half_width - d // 2), _F32)
    cos, sin = jnp.cos(ang), jnp.sin(ang)
    ra = jnp.concatenate([cos, pad, cos, pad], axis=1)
    rb = jnp.concatenate([-sin, pad, sin, pad], axis=1)
    return ra, rb


def _softmax_step(s, vt, m_ref, acc_ref):
    m_prev = m_ref[...]
    m_new = jnp.maximum(m_prev, jnp.max(s, axis=0, keepdims=True))
    alpha = jnp.exp2(m_prev - m_new)
    p = jnp.exp2(s - m_new).astype(MXU_DTYPE)
    vt1 = jnp.concatenate([vt, jnp.ones((ONES_ROWS, vt.shape[1]), vt.dtype)], axis=0)
    acc_ref[...] = alpha * acc_ref[...] + jnp.dot(vt1, p, preferred_element_type=_F32)
    m_ref[...] = m_new


def _pipelined_sweep(n, score, update, sa_ref, sb_ref):
    sa_ref[...] = score(0, True)

    def pair(t):
        sb_ref[...] = score(t + 1, False)
        update(t, sa_ref[...])
        sa_ref[...] = score(t + 2, False)
        update(t + 1, sb_ref[...])

    done = 0
    for size in (8, 4, 2):
        def body(p, carry, size=size, done=done):
            for u in range(size // 2):
                pair(done + size * p + 2 * u)
            return carry

        trips = (n - 1 - done) // size
        lax.fori_loop(0, trips, body, 0)
        done = done + size * trips
    t = done

    @pl.when(t == n - 1)
    def _odd_tail():
        update(t, sa_ref[...])

    @pl.when(t < n - 1)
    def _even_tail():
        sb_ref[...] = score(t + 1, False)
        update(t, sa_ref[...])
        update(t + 1, sb_ref[...])


def _static_sweep(n, score, update, sa_ref, sb_ref, unroll=8):
    sa_ref[...] = score(0)
    bufs = (sa_ref, sb_ref)

    def run(base, count, last):
        for u in range(count):
            cur, nxt = bufs[u % 2], bufs[(u + 1) % 2]
            if not (last and u == count - 1):
                nxt[...] = score(base + u + 1)
            update(base + u, cur[...])

    n_loop = (n - 1) // unroll * unroll

    def body(p, carry):
        run(p * unroll, unroll, False)
        return carry

    lax.fori_loop(0, n_loop // unroll, body, 0)
    run(n_loop, n - n_loop, True)


def _softmax_finish(acc_ref, dv):
    acc = acc_ref[...]
    return acc[:dv, :] / acc[dv:dv + 1, :]


def _mla_attn_kernel(jt_ref, it_ref, q_ref, kn_ref, kr_ref, vt_ref, cos_ref, sin_ref, o_ref,
                     qs_ref, m_ref, acc_ref, sa_ref, sb_ref, *, nq, tq, n_off):
    half = A_ROPE // 2
    r1 = slice(A_NOPE, A_NOPE + half)
    r2 = slice(A_NOPE + LANES // 2, A_NOPE + LANES // 2 + half)

    def prepare(i, carry):
        qt = q_ref[i]
        x1 = qt[r1, :].astype(_F32)
        x2 = qt[r2, :].astype(_F32)
        cos = cos_ref[i]
        sin = sin_ref[i]
        qs_ref[i] = qt
        qs_ref[i, r1, :] = (x1 * cos - x2 * sin).astype(qs_ref.dtype)
        qs_ref[i, r2, :] = (x1 * sin + x2 * cos).astype(qs_ref.dtype)
        m_ref[i] = jnp.full((1, tq), -jnp.inf, _F32)
        acc_ref[i] = jnp.zeros(acc_ref.shape[1:], _F32)
        return carry

    lax.fori_loop(0, nq, prepare, 0)

    def scores(j, i):
        off = pl.multiple_of(j * tq, tq)
        kc = jnp.concatenate([kn_ref[pl.ds(off, tq), :], kr_ref[pl.ds(off, tq), :]], axis=1)
        return jnp.dot(kc, qs_ref[i], preferred_element_type=_F32)

    def update(j, i, s):
        _softmax_step(s, vt_ref[j], m_ref.at[i], acc_ref.at[i])

    kchunk = lax.broadcasted_iota(_I32, (tq, tq), 0) // CHUNK
    qchunk = lax.broadcasted_iota(_I32, (tq, tq), 1) // CHUNK
    _static_sweep(nq, lambda t: jnp.where(qchunk >= kchunk, scores(t, t), -jnp.inf),
                  lambda t, s: update(t, t, s), sa_ref, sb_ref)
    if n_off:
        _static_sweep(n_off, lambda t: scores(jt_ref[t], it_ref[t]),
                      lambda t, s: update(jt_ref[t], it_ref[t], s), sa_ref, sb_ref)

    def finish(i, carry):
        off = pl.multiple_of(i * tq, tq)
        o_ref[pl.ds(off, tq), :] = _softmax_finish(acc_ref.at[i], A_V).T.astype(o_ref.dtype)
        return carry

    lax.fori_loop(0, nq, finish, 0)


def _mla_attention(qt, kn, kr, vt, cos_t, sin_t, batch, seq, tq):
    nq = seq // tq
    pairs = [(j, i) for j in range(nq) for i in range(j + 1, nq)]
    jt = jnp.asarray([p[0] for p in pairs] or [0], _I32)
    it = jnp.asarray([p[1] for p in pairs] or [0], _I32)
    half = A_ROPE // 2
    smem = pl.BlockSpec(memory_space=pltpu.SMEM)
    return pl.pallas_call(
        functools.partial(_mla_attn_kernel, nq=nq, tq=tq, n_off=len(pairs)),
        grid=(batch, A_HEADS),
        in_specs=[smem, smem,
                  pl.BlockSpec((nq, 2 * LANES, tq), lambda b, h: (b, h, 0)),
                  pl.BlockSpec((seq, LANES), lambda b, h: (b, h)),
                  pl.BlockSpec((seq, LANES), lambda b, h: (b, 0)),
                  pl.BlockSpec((nq, A_V, tq), lambda b, h: (b, h, 0)),
                  pl.BlockSpec((nq, half, tq), lambda b, h: (b, 0, 0)),
                  pl.BlockSpec((nq, half, tq), lambda b, h: (b, 0, 0))],
        out_specs=pl.BlockSpec((seq, A_V), lambda b, h: (b, h)),
        out_shape=jax.ShapeDtypeStruct((batch * seq, A_HEADS * A_V), MXU_DTYPE),
        scratch_shapes=[pltpu.VMEM((nq, 2 * LANES, tq), MXU_DTYPE),
                        pltpu.VMEM((nq, 1, tq), _F32),
                        pltpu.VMEM((nq, A_V + ONES_ROWS, tq), _F32),
                        pltpu.VMEM((tq, tq), _F32),
                        pltpu.VMEM((tq, tq), _F32)],
        compiler_params=_params(("parallel", "parallel")),
        name="mla_attention",
    )(jt, it, qt, kn, kr, vt, cos_t, sin_t)


def _pad_rope_cols(w, n_heads, nope):
    k = w.shape[0]
    w = w.reshape(k, n_heads, nope + A_ROPE)
    half = A_ROPE // 2
    z = jnp.zeros((k, n_heads, LANES // 2 - half), w.dtype)
    w = jnp.concatenate([w[..., :nope], w[..., nope:nope + half], z, w[..., nope + half:], z], axis=-1)
    return w.reshape(k, n_heads * (nope + LANES))


def _mla_mixer(x, positions, mix_g, w_in, q_norm, kv_norm, w_uq, w_ukv, w_out, batch, seq):
    assert A_NOPE == LANES and A_V == LANES and A_ROPE == LANES // 2
    tq = min(MLA_TQ, seq)
    n_lat = A_Q_LORA + A_KV_LORA
    w_in_p = jnp.concatenate([w_in[:, :n_lat], _pad_rope_cols(w_in[:, n_lat:], 1, 0)], axis=1)
    lat = _matmul_norm(x, mix_g, w_in_p.astype(MXU_DTYPE), _F32, bm=MM_BM // 2,
                       bn=w_in_p.shape[1])
    ra, rb = _rope_tables(positions, A_ROPE, LANES // 2)
    kr = _rope(lat, ra, rb, MXU_DTYPE, col_block=n_lat // LANES, width=LANES)
    qscale = (A_NOPE + A_ROPE) ** -0.5 * LOG2E
    w_qt = (_pad_rope_cols(w_uq, A_HEADS, A_NOPE) * qscale).T.astype(MXU_DTYPE)
    qt = _matmul_norm_t(w_qt, lat, q_norm, tq, MXU_DTYPE, bn=2 * MM_BN, row_tiles=2)
    w_ukv = w_ukv.reshape(A_KV_LORA, A_HEADS, A_NOPE + A_V)
    w_kn = w_ukv[..., :A_NOPE].reshape(A_KV_LORA, A_HEADS * A_NOPE)
    w_vt = w_ukv[..., A_NOPE:].reshape(A_KV_LORA, A_HEADS * A_V).T
    kv_blk = A_Q_LORA // A_KV_LORA
    kn = _matmul_norm(lat, kv_norm, w_kn.astype(MXU_DTYPE), MXU_DTYPE, bm=2 * MM_BM, bn=2 * MM_BN,
                      col_block=kv_blk)
    vt = _matmul_norm_t(w_vt.astype(MXU_DTYPE), lat, kv_norm, tq, MXU_DTYPE, bn=2 * MM_BN,
                        col_block=kv_blk, row_tiles=2)
    half = A_ROPE // 2
    cos_t = ra[:, :half].reshape(-1, tq, half).transpose(0, 2, 1)
    sin_t = rb[:, LANES // 2:LANES // 2 + half].reshape(-1, tq, half).transpose(0, 2, 1)
    o = _mla_attention(qt, kn, kr, vt, cos_t, sin_t, batch, seq, tq)
    return _matmul_residual(o, w_out.astype(MXU_DTYPE), x, 1.0)


def _dsa_kernel(qi_ref, wit_ref, ki_ref, ra_ref, rb_ref, q_ref, k_ref, vt_ref, o_ref,
                qit_ref, key_ref, qt_ref, m_ref, acc_ref, sa_ref, sb_ref,
                *, tq, tk, topk, group, wi_scale, qscale):
    i = pl.program_id(1)
    h = pl.program_id(2)
    nkb = (i + 1) * (tq // tk)
    ra = ra_ref[...]
    rb = rb_ref[...]

    @pl.when(h == 0)
    def _select():
        for hd in range(B_IDX_HEADS):
            sl = slice(hd * LANES, (hd + 1) * LANES)
            x = qi_ref[:, sl].astype(_F32)
            x = x * ra + _rot_half(x) * rb
            qit_ref[sl, :] = x.T.astype(qit_ref.dtype)
        wit = wit_ref[...] * wi_scale

        def score_tile(j, carry):
            off = pl.multiple_of(j * tk, tk)
            ki = ki_ref[pl.ds(off, tk), :]
            sc = jnp.zeros((tk, tq), _F32)
            for hd in range(B_IDX_HEADS):
                d = jnp.dot(ki, qit_ref[hd * LANES:(hd + 1) * LANES, :],
                            preferred_element_type=_F32)
                sc = sc + jnp.maximum(d, 0.0) * wit[hd:hd + 1, :]
            kchunk = (off + lax.broadcasted_iota(_I32, (tk, tq), 0)) // CHUNK
            qchunk = (i * tq + lax.broadcasted_iota(_I32, (tk, tq), 1)) // CHUNK
            sc = jnp.where(qchunk >= kchunk, sc, -jnp.inf)
            bits = pltpu.bitcast(sc, _I32)
            key_ref[j] = jnp.where(bits >= 0, bits, bits ^ 0x7FFFFFFF)
            return carry

        lax.fori_loop(0, nkb, score_tile, 0)

        def count(pred):
            def tile(j, acc):
                hit = jnp.where(pred(key_ref[j]), 1.0, 0.0)
                return acc + jnp.sum(hit.reshape(tk // SUBLANES, SUBLANES, tq), axis=0)

            def four_tiles(p, acc):
                for u in range(4):
                    acc = tile(4 * p + u, acc)
                return acc

            acc = lax.fori_loop(0, nkb // 4, four_tiles, jnp.zeros((SUBLANES, tq), _F32))
            acc = lax.fori_loop(nkb // 4 * 4, nkb, tile, acc)
            return jnp.sum(acc, axis=0, keepdims=True)

        def bisect(b, t):
            cand = t + jnp.left_shift(jnp.int32(1), 31 - b)
            return jnp.where(count(lambda kt: kt >= cand) >= topk, cand, t)

        t = lax.fori_loop(0, 32, bisect, jnp.full((1, tq), _INT_MIN, _I32))

        need = topk - count(lambda kt: kt > t)
        lower = jnp.where(lax.broadcasted_iota(_I32, (tk, tk), 0)
                          >= lax.broadcasted_iota(_I32, (tk, tk), 1), 1.0, 0.0).astype(MXU_DTYPE)

        def demote_tile(j, seen):
            kt = key_ref[j]
            eq = kt == t
            rank = seen + jnp.dot(lower, jnp.where(eq, 1.0, 0.0).astype(MXU_DTYPE),
                                  preferred_element_type=_F32)
            key_ref[j] = jnp.where(eq, jnp.where(rank > need, t - 1, kt), kt)
            return rank[tk - 1:tk, :]

        lax.fori_loop(0, nkb, demote_tile, jnp.zeros((1, tq), _F32))
        thr = jnp.maximum(t, _KEY_OF_NEG_INF + 1)

        def bias_tile(j, carry):
            bias = jnp.where(key_ref[j] >= thr, 0.0, _NEG_BIG)
            key_ref[j] = pltpu.bitcast(bias, _I32)
            return carry

        lax.fori_loop(0, nkb, bias_tile, 0)

    q = q_ref[...]
    for g in range(group):
        x = q[:, g * LANES:(g + 1) * LANES].astype(_F32)
        x = (x * ra + _rot_half(x) * rb) * qscale
        qt_ref[:, g * tq:(g + 1) * tq] = x.T.astype(qt_ref.dtype)
    m_ref[...] = jnp.full(m_ref.shape, _NEG_BIG, _F32)
    acc_ref[...] = jnp.zeros(acc_ref.shape, _F32)

    def score(j, first):
        off = pl.multiple_of(j * tk, tk)
        bias = pltpu.bitcast(key_ref[j], _F32)
        bias = jnp.concatenate([bias] * group, axis=1)
        return jnp.dot(k_ref[pl.ds(off, tk), :], qt_ref[...], preferred_element_type=_F32) + bias

    def update(j, s):
        _softmax_step(s, vt_ref[j], m_ref, acc_ref)

    _pipelined_sweep(nkb, score, update, sa_ref, sb_ref)
    out = _softmax_finish(acc_ref, B_HEAD_DIM)
    for g in range(group):
        o_ref[:, g * LANES:(g + 1) * LANES] = out[:, g * tq:(g + 1) * tq].T.astype(o_ref.dtype)


def _dsa_attention(proj, k_roped, vt, ki, wit, ra, rb, batch, seq, tq, tk):
    nq = seq // tq
    group = B_HEADS // B_KV_HEADS
    topk = min(B_TOPK_MAX, seq // 4)
    qi_w = B_IDX_HEADS * B_IDX_DIM
    q_w = group * B_HEAD_DIM
    q_blk0 = qi_w // q_w
    kern = functools.partial(
        _dsa_kernel, tq=tq, tk=tk, topk=float(topk), group=group,
        wi_scale=B_IDX_HEADS ** -0.5 * B_IDX_DIM ** -0.5, qscale=B_HEAD_DIM ** -0.5 * LOG2E)
    return pl.pallas_call(
        kern,
        grid=(batch, nq, B_KV_HEADS),
        in_specs=[pl.BlockSpec((tq, qi_w), lambda b, i, h: (b * nq + i, 0)),
                  pl.BlockSpec((B_IDX_HEADS, tq), lambda b, i, h: (0, b * nq + i)),
                  pl.BlockSpec((seq, LANES), lambda b, i, h: (b, 0)),
                  pl.BlockSpec((tq, LANES), lambda b, i, h: (b * nq + i, 0)),
                  pl.BlockSpec((tq, LANES), lambda b, i, h: (b * nq + i, 0)),
                  pl.BlockSpec((tq, q_w), lambda b, i, h: (b * nq + i, q_blk0 + h)),
                  pl.BlockSpec((seq, B_HEAD_DIM), lambda b, i, h: (b, h)),
                  pl.BlockSpec((seq // tk, B_HEAD_DIM, tk), lambda b, i, h: (b, h, 0))],
        out_specs=pl.BlockSpec((tq, q_w), lambda b, i, h: (b * nq + i, h)),
        out_shape=jax.ShapeDtypeStruct((batch * seq, B_HEADS * B_HEAD_DIM), MXU_DTYPE),
        scratch_shapes=[pltpu.VMEM((qi_w, tq), MXU_DTYPE),
                        pltpu.VMEM((seq // tk, tk, tq), _I32),
                        pltpu.VMEM((B_HEAD_DIM, group * tq), MXU_DTYPE),
                        pltpu.VMEM((1, group * tq), _F32),
                        pltpu.VMEM((B_HEAD_DIM + ONES_ROWS, group * tq), _F32),
                        pltpu.VMEM((tk, group * tq), _F32),
                        pltpu.VMEM((tk, group * tq), _F32)],
        compiler_params=_params(("parallel", "arbitrary", "arbitrary")),
        name="dsa_attention",
    )(proj, wit, ki, ra, rb, proj, k_roped, vt)


def _dsa_mixer(x, positions, mix_g, w_in, idx_k_norm, w_out, batch, seq):
    assert B_HEAD_DIM == LANES and B_IDX_DIM == LANES and B_IDX_HEADS % SUBLANES == 0
    tq = min(DSA_TQ, seq)
    tk = min(DSA_TK, tq)
    nq, nk = B_HEADS * B_HEAD_DIM, B_KV_HEADS * B_HEAD_DIM
    nqi = B_IDX_HEADS * B_IDX_DIM
    o_q, o_k, o_v, o_qi, o_ki, o_wi = np.cumsum([0, nq, nk, nk, nqi, B_IDX_DIM])
    w_main = jnp.concatenate([w_in[:, o_qi:o_ki], w_in[:, o_q:o_v]], axis=1)
    w_side = jnp.concatenate(
        [w_in[:, o_ki:o_wi], w_in[:, o_wi:],
         jnp.zeros((w_in.shape[0], LANES - B_IDX_HEADS), w_in.dtype)], axis=1)
    proj = _matmul_norm(x, mix_g, w_main.astype(MXU_DTYPE), MXU_DTYPE)
    side = _matmul_norm(x, mix_g, w_side.astype(MXU_DTYPE), _F32, bn=2 * LANES)
    vt = _matmul_norm_t(w_in[:, o_v:o_qi].T.astype(MXU_DTYPE), x, mix_g, tk, MXU_DTYPE)
    ra, rb = _rope_tables(positions, B_HEAD_DIM, LANES // 2)
    k_roped = _rope(proj, ra, rb, MXU_DTYPE, col_block=(nqi + nq) // nk, width=nk)
    ki = _rmsnorm_rope(side, idx_k_norm, ra, rb, MXU_DTYPE, col_block=0)
    wit = side[:, LANES:LANES + B_IDX_HEADS].T
    o = _dsa_attention(proj, k_roped, vt, ki, wit, ra, rb, batch, seq, tq, tk)
    return _matmul_residual(o, w_out.astype(MXU_DTYPE), x, 1.0)


def _log_sigmoid(x):
    return jnp.minimum(x, 0.0) - jnp.log(1.0 + jnp.exp(-jnp.abs(x)))


def _mlstm_kernel(bias_ref, q_ref, k_ref, v_ref, o_ref, grow_ref, gcol_ref, hn_ref, y_ref,
                  c_ref, n_ref, m_ref, *, n_heads, dk, dv):
    @pl.when(pl.program_id(1) == 0)
    def _init():
        c_ref[...] = jnp.zeros(c_ref.shape, _F32)
        n_ref[...] = jnp.zeros(n_ref.shape, _F32)
        m_ref[...] = jnp.zeros(m_ref.shape, _F32)

    grow = grow_ref[0]
    gcol = gcol_ref[0]
    rows = lax.broadcasted_iota(_I32, (CHUNK, CHUNK), 0)
    cols = lax.broadcasted_iota(_I32, (CHUNK, CHUNK), 1)
    tril = rows >= cols
    triu = rows <= cols

    for h in range(n_heads):
        b_i = bias_ref[h]
        b_f = bias_ref[n_heads + h]
        li_row = grow[h:h + 1, :] + b_i
        lf_row = _log_sigmoid(grow[n_heads + h:n_heads + h + 1, :] + b_f)
        li_col = gcol[:, h:h + 1] + b_i
        lf_col = _log_sigmoid(gcol[:, n_heads + h:n_heads + h + 1] + b_f)

        b_col = jnp.sum(jnp.where(tril, lf_row, 0.0), axis=1, keepdims=True)
        b_row = jnp.sum(jnp.where(triu, lf_col, 0.0), axis=0, keepdims=True)
        b_end = jnp.sum(lf_row, axis=1, keepdims=True)

        m_prev = m_ref[h]
        a = b_col + m_prev
        d = jnp.where(tril, b_col - b_row + li_row, -jnp.inf)
        m_t = jnp.maximum(a, jnp.max(d, axis=1, keepdims=True))
        inter = jnp.exp(a - m_t)

        q = q_ref[:, h * dk:(h + 1) * dk]
        k = k_ref[:, h * dk:(h + 1) * dk]
        v = v_ref[:, h * dv:(h + 1) * dv]
        qf = q.astype(_F32) * dk ** -0.5
        qs = qf.astype(MXU_DTYPE)
        qk = lax.dot_general(qs, k, (((1,), (1,)), ((), ())),
                             preferred_element_type=_F32) * jnp.exp(d - m_t)
        ct = c_ref[h]
        num = inter * jnp.dot(qs, ct.astype(MXU_DTYPE), preferred_element_type=_F32) + jnp.dot(
            qk.astype(MXU_DTYPE), v, preferred_element_type=_F32)
        n_row = n_ref[h]
        den = inter * jnp.sum(qf * n_row, axis=1, keepdims=True) + jnp.sum(qk, axis=1, keepdims=True)
        hs = num / jnp.maximum(jnp.abs(den), jnp.exp(-m_t))

        g_col = b_end - b_col + li_col
        m_new = jnp.maximum(b_end + m_prev, jnp.max(g_col, axis=0, keepdims=True))
        decay = jnp.exp(b_end + m_prev - m_new)
        wk = jnp.exp(g_col - m_new)
        vw = (v.astype(_F32) * wk).astype(MXU_DTYPE)
        c_ref[h] = decay * ct + lax.dot_general(k, vw, (((0,), (0,)), ((), ())),
                                                preferred_element_type=_F32)
        n_ref[h] = decay * n_row + jnp.sum(wk * k.astype(_F32), axis=0, keepdims=True)
        m_ref[h] = m_new

        hs = hs * lax.rsqrt(jnp.mean(hs * hs, axis=1, keepdims=True) + NORM_EPS)
        gate = jax.nn.sigmoid(o_ref[:, h * dv:(h + 1) * dv].astype(_F32))
        y_ref[:, h * dv:(h + 1) * dv] = (hs * hn_ref[:, h * dv:(h + 1) * dv] * gate).astype(y_ref.dtype)


def _mlstm_scan(proj, g_row, g_col, gate_bias, head_norm, batch, seq, dk, dv):
    nc = seq // CHUNK
    hq = C_HEADS
    qk_w, v_w = hq * dk, hq * dv
    v_blk0 = 2 * qk_w // v_w
    return pl.pallas_call(
        functools.partial(_mlstm_kernel, n_heads=hq, dk=dk, dv=dv),
        grid=(batch, nc),
        in_specs=[pl.BlockSpec(memory_space=pltpu.SMEM),
                  pl.BlockSpec((CHUNK, qk_w), lambda b, c: (b * nc + c, 0)),
                  pl.BlockSpec((CHUNK, qk_w), lambda b, c: (b * nc + c, 1)),
                  pl.BlockSpec((CHUNK, v_w), lambda b, c: (b * nc + c, v_blk0)),
                  pl.BlockSpec((CHUNK, v_w), lambda b, c: (b * nc + c, v_blk0 + 1)),
                  pl.BlockSpec((1, 2 * hq, CHUNK), lambda b, c: (b * nc + c, 0, 0)),
                  pl.BlockSpec((1, CHUNK, 2 * hq), lambda b, c: (b * nc + c, 0, 0)),
                  pl.BlockSpec((1, v_w), lambda b, c: (0, 0))],
        out_specs=pl.BlockSpec((CHUNK, v_w), lambda b, c: (b * nc + c, 0)),
        out_shape=jax.ShapeDtypeStruct((batch * seq, v_w), MXU_DTYPE),
        scratch_shapes=[pltpu.VMEM((hq, dk, dv), _F32),
                        pltpu.VMEM((hq, 1, dk), _F32),
                        pltpu.VMEM((hq, 1, 1), _F32)],
        compiler_params=_params(("parallel", "arbitrary")),
        name="mlstm_scan",
    )(gate_bias.astype(_F32), proj, proj, proj, proj, g_row, g_col,
      head_norm.reshape(1, v_w).astype(_F32))


def _mlstm_mixer(x, mix_g, w_in, gate_bias, head_norm, w_out, batch, seq):
    d_model = x.shape[1]
    dv = d_model // C_HEADS
    dk = dv // 2
    hq = C_HEADS
    n_main = 2 * hq * dk + 2 * hq * dv
    w_gate = jnp.concatenate(
        [w_in[:, n_main:], jnp.zeros((w_in.shape[0], LANES - 2 * hq), w_in.dtype)], axis=1)
    proj = _matmul_norm(x, mix_g, w_in[:, :n_main].astype(MXU_DTYPE), MXU_DTYPE)
    gates = _matmul_norm(x, mix_g, w_gate.astype(MXU_DTYPE), _F32, bn=LANES)
    nc = seq // CHUNK
    g_col = gates[:, :2 * hq].reshape(batch * nc, CHUNK, 2 * hq)
    g_row = g_col.transpose(0, 2, 1)
    y = _mlstm_scan(proj, g_row, g_col, gate_bias, head_norm, batch, seq, dk, dv)
    return _matmul_residual(y, w_out.astype(MXU_DTYPE), x, 1.0)


def kernel(x, positions, ffn1_norm, ffn1_w_in, ffn1_w_out, mix_norm, ffn2_norm, ffn2_w_in, ffn2_w_out,
           a_w_in, a_q_norm, a_kv_norm, a_w_uq, a_w_ukv, a_w_out,
           b_w_in, b_idx_k_norm, b_w_out,
           c_w_in, c_gate_bias, c_head_norm, c_w_out, final_norm):
    batch, seq, d_model = x.shape
    depth = ffn1_norm.shape[0]
    x = x.reshape(batch * seq, d_model)
    ffn1_w_in, ffn1_w_out, ffn2_w_in, ffn2_w_out = (
        w.astype(MXU_DTYPE) for w in (ffn1_w_in, ffn1_w_out, ffn2_w_in, ffn2_w_out))
    for i in range(depth):
        x = _ffn(x, ffn1_norm[i], ffn1_w_in, ffn1_w_out, i)
        kind, j = i % N_MIXERS, i // N_MIXERS
        if kind == 0:
            x = _mla_mixer(x, positions, mix_norm[i], a_w_in[j], a_q_norm[j], a_kv_norm[j],
                           a_w_uq[j], a_w_ukv[j], a_w_out[j], batch, seq)
        elif kind == 1:
            x = _dsa_mixer(x, positions, mix_norm[i], b_w_in[j], b_idx_k_norm[j], b_w_out[j], batch, seq)
        else:
            x = _mlstm_mixer(x, mix_norm[i], c_w_in[j], c_gate_bias[j], c_head_norm[j], c_w_out[j],
                             batch, seq)
        x = _ffn(x, ffn2_norm[i], ffn2_w_in, ffn2_w_out, i)
    return _rmsnorm(x, final_norm, _F32).reshape(batch, seq, d_model)
```

```python
import functools

import jax
import jax.numpy as jnp
import numpy as np
from jax import lax
from jax.experimental import pallas as pl
from jax.experimental.pallas import tpu as pltpu

CHUNK = 64
ROPE_THETA = 10000.0
NORM_EPS = 1e-6
D_FF = 6144

A_HEADS = 64
A_Q_LORA = 1024
A_KV_LORA = 512
A_NOPE = 128
A_ROPE = 64
A_V = 128

B_HEADS = 32
B_KV_HEADS = 8
B_HEAD_DIM = 128
B_IDX_HEADS = 32
B_IDX_DIM = 128
B_TOPK_MAX = 256

C_HEADS = 8
N_MIXERS = 3

LANES = 128
SUBLANES = 8
ONES_ROWS = 16
LOG2E = 1.4426950408889634
MXU_DTYPE = jnp.bfloat16
VMEM_LIMIT_MB = 56

ROW_TILE = 256
NORM_ROWS = 64
MM_BM = 512
MM_BN = 1024
MM_RESIDUAL_VMEM_MB = 48
MLA_TQ = 512
DSA_TQ = 256
DSA_TK = 256

_F32 = jnp.float32
_I32 = jnp.int32
_NEG_BIG = -1e30
_INT_MIN = -(2 ** 31)
_KEY_OF_NEG_INF = -2139095041


def _params(semantics):
    return pltpu.CompilerParams(dimension_semantics=semantics,
                                vmem_limit_bytes=VMEM_LIMIT_MB << 20)


def _rot_half(x):
    return pltpu.roll(x, LANES // 2, 1)


def _rmsnorm_kernel(x_ref, g_ref, o_ref):
    x = x_ref[...].astype(_F32)
    y = x * lax.rsqrt(jnp.mean(x * x, axis=-1, keepdims=True) + NORM_EPS)
    o_ref[...] = (y * g_ref[...]).astype(o_ref.dtype)


def _rmsnorm(x, g, out_dtype, col_block=0, width=None):
    t = x.shape[0]
    width = x.shape[1] if width is None else width
    bt = min(ROW_TILE, t)
    return pl.pallas_call(
        _rmsnorm_kernel,
        grid=(t // bt,),
        in_specs=[pl.BlockSpec((bt, width), lambda i: (i, col_block)),
                  pl.BlockSpec((1, width), lambda i: (0, 0))],
        out_specs=pl.BlockSpec((bt, width), lambda i: (i, 0)),
        out_shape=jax.ShapeDtypeStruct((t, width), out_dtype),
        compiler_params=_params(("parallel",)),
        name="rmsnorm",
    )(x, g.reshape(1, width).astype(_F32))


def _rope_kernel(x_ref, ra_ref, rb_ref, o_ref, *, n_blocks):
    ra = ra_ref[...]
    rb = rb_ref[...]
    for c in range(n_blocks):
        sl = slice(c * LANES, (c + 1) * LANES)
        x = x_ref[:, sl].astype(_F32)
        o_ref[:, sl] = (x * ra + _rot_half(x) * rb).astype(o_ref.dtype)


def _rope(x, ra, rb, out_dtype, col_block=0, width=None):
    t = x.shape[0]
    width = x.shape[1] if width is None else width
    bt = min(ROW_TILE, t)
    return pl.pallas_call(
        functools.partial(_rope_kernel, n_blocks=width // LANES),
        grid=(t // bt,),
        in_specs=[pl.BlockSpec((bt, width), lambda i: (i, col_block)),
                  pl.BlockSpec((bt, LANES), lambda i: (i, 0)),
                  pl.BlockSpec((bt, LANES), lambda i: (i, 0))],
        out_specs=pl.BlockSpec((bt, width), lambda i: (i, 0)),
        out_shape=jax.ShapeDtypeStruct((t, width), out_dtype),
        compiler_params=_params(("parallel",)),
        name="rope",
    )(x, ra, rb)


def _rmsnorm_rope_kernel(x_ref, g_ref, ra_ref, rb_ref, o_ref):
    x = x_ref[...].astype(_F32)
    y = x * lax.rsqrt(jnp.mean(x * x, axis=-1, keepdims=True) + NORM_EPS) * g_ref[...]
    o_ref[...] = (y * ra_ref[...] + _rot_half(y) * rb_ref[...]).astype(o_ref.dtype)


def _rmsnorm_rope(x, g, ra, rb, out_dtype, col_block=0):
    t = x.shape[0]
    bt = min(ROW_TILE, t)
    return pl.pallas_call(
        _rmsnorm_rope_kernel,
        grid=(t // bt,),
        in_specs=[pl.BlockSpec((bt, LANES), lambda i: (i, col_block)),
                  pl.BlockSpec((1, LANES), lambda i: (0, 0)),
                  pl.BlockSpec((bt, LANES), lambda i: (i, 0)),
                  pl.BlockSpec((bt, LANES), lambda i: (i, 0))],
        out_specs=pl.BlockSpec((bt, LANES), lambda i: (i, 0)),
        out_shape=jax.ShapeDtypeStruct((t, LANES), out_dtype),
        compiler_params=_params(("parallel",)),
        name="rmsnorm_rope",
    )(x, g.reshape(1, LANES).astype(_F32), ra, rb)


def _normed(x_ref, g_ref, h_ref):
    @pl.when(pl.program_id(1) == 0)
    def _normalise():
        g = g_ref[...]
        rows = min(NORM_ROWS, x_ref.shape[0])

        def chunk(c, carry):
            sl = pl.ds(pl.multiple_of(c * rows, rows), rows)
            x = x_ref[sl, :].astype(_F32)
            y = x * lax.rsqrt(jnp.mean(x * x, axis=-1, keepdims=True) + NORM_EPS)
            h_ref[sl, :] = (y * g).astype(h_ref.dtype)
            return carry

        lax.fori_loop(0, x_ref.shape[0] // rows, chunk, 0)

    return h_ref[...]


def _mm_norm_kernel(x_ref, g_ref, w_ref, o_ref, h_ref):
    o_ref[...] = jnp.dot(_normed(x_ref, g_ref, h_ref), w_ref[...],
                         preferred_element_type=_F32).astype(o_ref.dtype)


def _mm_norm_swiglu_kernel(x_ref, g_ref, wg_ref, wu_ref, o_ref, h_ref):
    h = _normed(x_ref, g_ref, h_ref)
    g = jnp.dot(h, wg_ref[...], preferred_element_type=_F32)
    u = jnp.dot(h, wu_ref[...], preferred_element_type=_F32)
    o_ref[...] = (g * jax.nn.sigmoid(g) * u).astype(o_ref.dtype)


def _mm_norm_t_kernel(w_ref, x_ref, g_ref, o_ref, h_ref):
    y = lax.dot_general(w_ref[...], _normed(x_ref, g_ref, h_ref), (((1,), (1,)), ((), ())),
                        preferred_element_type=_F32).astype(o_ref.dtype)
    tk = o_ref.shape[2]
    for r in range(o_ref.shape[0]):
        o_ref[r] = y[:, r * tk:(r + 1) * tk]


def _mm_residual_kernel(x_ref, w_ref, r_ref, o_ref, *, scale):
    y = jnp.dot(x_ref[...], w_ref[...], preferred_element_type=_F32)
    o_ref[...] = r_ref[...] + scale * y


def _tile(n, pref):
    return pref if n % pref == 0 else n


def _matmul_norm(x, g, w, out_dtype, bm=None, bn=None, col_block=0):
    m = x.shape[0]
    k, n = w.shape
    bm = _tile(m, bm or MM_BM)
    bn = _tile(n, bn or MM_BN)
    return pl.pallas_call(
        _mm_norm_kernel,
        grid=(m // bm, n // bn),
        in_specs=[pl.BlockSpec((bm, k), lambda i, j: (i, col_block)),
                  pl.BlockSpec((1, k), lambda i, j: (0, 0)),
                  pl.BlockSpec((k, bn), lambda i, j: (0, j))],
        out_specs=pl.BlockSpec((bm, bn), lambda i, j: (i, j)),
        out_shape=jax.ShapeDtypeStruct((m, n), out_dtype),
        scratch_shapes=[pltpu.VMEM((bm, k), MXU_DTYPE)],
        compiler_params=_params(("parallel", "arbitrary")),
        name="matmul_norm",
    )(x, g.reshape(1, k).astype(_F32), w)


def _matmul_norm_t(w_t, x, g, tk, out_dtype, bn=None, col_block=0, row_tiles=1):
    n, k = w_t.shape
    t = x.shape[0]
    bn = _tile(n, bn or MM_BN)
    rt = row_tiles if (t // tk) % row_tiles == 0 else 1
    return pl.pallas_call(
        _mm_norm_t_kernel,
        grid=(t // (rt * tk), n // bn),
        in_specs=[pl.BlockSpec((bn, k), lambda i, j: (j, 0)),
                  pl.BlockSpec((rt * tk, k), lambda i, j: (i, col_block)),
                  pl.BlockSpec((1, k), lambda i, j: (0, 0))],
        out_specs=pl.BlockSpec((rt, bn, tk), lambda i, j: (i, j, 0)),
        out_shape=jax.ShapeDtypeStruct((t // tk, n, tk), out_dtype),
        scratch_shapes=[pltpu.VMEM((rt * tk, k), MXU_DTYPE)],
        compiler_params=_params(("parallel", "arbitrary")),
        name="matmul_norm_t",
    )(w_t, x, g.reshape(1, k).astype(_F32))


def _matmul_norm_swiglu(x, g, w_in, layer=None, bm=None, bn=512):
    m, k = x.shape
    f = w_in.shape[-1] // 2
    bm = _tile(m, bm or MM_BM)
    bn = _tile(f, bn)
    nb = f // bn
    if layer is None:
        w_specs = [pl.BlockSpec((k, bn), lambda i, j: (0, j)),
                   pl.BlockSpec((k, bn), lambda i, j: (0, j + nb))]
    else:
        w_specs = [pl.BlockSpec((None, k, bn), lambda i, j: (layer, 0, j)),
                   pl.BlockSpec((None, k, bn), lambda i, j: (layer, 0, j + nb))]
    return pl.pallas_call(
        _mm_norm_swiglu_kernel,
        grid=(m // bm, nb),
        in_specs=[pl.BlockSpec((bm, k), lambda i, j: (i, 0)),
                  pl.BlockSpec((1, k), lambda i, j: (0, 0))] + w_specs,
        out_specs=pl.BlockSpec((bm, bn), lambda i, j: (i, j)),
        out_shape=jax.ShapeDtypeStruct((m, f), MXU_DTYPE),
        scratch_shapes=[pltpu.VMEM((bm, k), MXU_DTYPE)],
        compiler_params=_params(("parallel", "arbitrary")),
        name="matmul_norm_swiglu",
    )(x, g.reshape(1, k).astype(_F32), w_in, w_in)


def _matmul_residual(x, w, res, scale, layer=None, bn=512):
    m, k = x.shape
    n = w.shape[-1]
    bn = _tile(n, bn)
    bm = _tile(m, MM_BM)
    for cand in (2 * MM_BM,):
        windows = 2 * (cand * k * 2 + k * bn * 2 + 2 * cand * bn * 4)
        if m % cand == 0 and windows <= MM_RESIDUAL_VMEM_MB << 20:
            bm = cand
    if layer is None:
        w_spec = pl.BlockSpec((k, bn), lambda i, j: (0, j))
    else:
        w_spec = pl.BlockSpec((None, k, bn), lambda i, j: (layer, 0, j))
    return pl.pallas_call(
        functools.partial(_mm_residual_kernel, scale=scale),
        grid=(m // bm, n // bn),
        in_specs=[pl.BlockSpec((bm, k), lambda i, j: (i, 0)),
                  w_spec,
                  pl.BlockSpec((bm, bn), lambda i, j: (i, j))],
        out_specs=pl.BlockSpec((bm, bn), lambda i, j: (i, j)),
        out_shape=jax.ShapeDtypeStruct((m, n), _F32),
        compiler_params=_params(("parallel", "arbitrary")),
        name="matmul_residual",
    )(x, w, res)


def _ffn(x, norm_g, w_in_all, w_out_all, layer):
    act = _matmul_norm_swiglu(x, norm_g, w_in_all, layer, bm=2 * MM_BM, bn=MM_BN // 4)
    return _matmul_residual(act, w_out_all, x, 0.5, layer)


def _rope_tables(positions, d, half_width):
    inv = jnp.power(ROPE_THETA, -jnp.arange(0, d, 2, dtype=_F32) / d)
    ang = positions.astype(_F32).reshape(-1, 1) * inv
    pad = jnp.zeros((ang.shape[0], half_width - d // 2), _F32)
    cos, sin = jnp.cos(ang), jnp.sin(ang)
    ra = jnp.concatenate([cos, pad, cos, pad], axis=1)
    rb = jnp.concatenate([-sin, pad, sin, pad], axis=1)
    return ra, rb


def _softmax_step(s, vt, m_ref, acc_ref):
    m_prev = m_ref[...]
    m_new = jnp.maximum(m_prev, jnp.max(s, axis=0, keepdims=True))
    alpha = jnp.exp2(m_prev - m_new)
    p = jnp.exp2(s - m_new).astype(MXU_DTYPE)
    vt1 = jnp.concatenate([vt, jnp.ones((ONES_ROWS, vt.shape[1]), vt.dtype)], axis=0)
    acc_ref[...] = alpha * acc_ref[...] + jnp.dot(vt1, p, preferred_element_type=_F32)
    m_ref[...] = m_new


def _pipelined_sweep(n, score, update, sa_ref, sb_ref):
    sa_ref[...] = score(0, True)

    def pair(t):
        sb_ref[...] = score(t + 1, False)
        update(t, sa_ref[...])
        sa_ref[...] = score(t + 2, False)
        update(t + 1, sb_ref[...])

    done = 0
    for size in (8, 4, 2):
        def body(p, carry, size=size, done=done):
            for u in range(size // 2):
                pair(done + size * p + 2 * u)
            return carry

        trips = (n - 1 - done) // size
        lax.fori_loop(0, trips, body, 0)
        done = done + size * trips
    t = done

    @pl.when(t == n - 1)
    def _odd_tail():
        update(t, sa_ref[...])

    @pl.when(t < n - 1)
    def _even_tail():
        sb_ref[...] = score(t + 1, False)
        update(t, sa_ref[...])
        update(t + 1, sb_ref[...])


def _static_sweep(n, score, update, sa_ref, sb_ref, unroll=8):
    sa_ref[...] = score(0)
    bufs = (sa_ref, sb_ref)

    def run(base, count, last):
        for u in range(count):
            cur, nxt = bufs[u % 2], bufs[(u + 1) % 2]
            if not (last and u == count - 1):
                nxt[...] = score(base + u + 1)
            update(base + u, cur[...])

    n_loop = (n - 1) // unroll * unroll

    def body(p, carry):
        run(p * unroll, unroll, False)
        return carry

    lax.fori_loop(0, n_loop // unroll, body, 0)
    run(n_loop, n - n_loop, True)


def _softmax_finish(acc_ref, dv):
    acc = acc_ref[...]
    return acc[:dv, :] / acc[dv:dv + 1, :]


def _mla_attn_kernel(jt_ref, it_ref, q_ref, kn_ref, kr_ref, vt_ref, cos_ref, sin_ref, o_ref,
                     qs_ref, m_ref, acc_ref, sa_ref, sb_ref, *, nq, tq, n_off):
    half = A_ROPE // 2
    r1 = slice(A_NOPE, A_NOPE + half)
    r2 = slice(A_NOPE + LANES // 2, A_NOPE + LANES // 2 + half)

    def prepare(i, carry):
        qt = q_ref[i]
        x1 = qt[r1, :].astype(_F32)
        x2 = qt[r2, :].astype(_F32)
        cos = cos_ref[i]
        sin = sin_ref[i]
        qs_ref[i] = qt
        qs_ref[i, r1, :] = (x1 * cos - x2 * sin).astype(qs_ref.dtype)
        qs_ref[i, r2, :] = (x1 * sin + x2 * cos).astype(qs_ref.dtype)
        m_ref[i] = jnp.full((1, tq), -jnp.inf, _F32)
        acc_ref[i] = jnp.zeros(acc_ref.shape[1:], _F32)
        return carry

    lax.fori_loop(0, nq, prepare, 0)

    def scores(j, i):
        off = pl.multiple_of(j * tq, tq)
        kc = jnp.concatenate([kn_ref[pl.ds(off, tq), :], kr_ref[pl.ds(off, tq), :]], axis=1)
        return jnp.dot(kc, qs_ref[i], preferred_element_type=_F32)

    def update(j, i, s):
        _softmax_step(s, vt_ref[j], m_ref.at[i], acc_ref.at[i])

    kchunk = lax.broadcasted_iota(_I32, (tq, tq), 0) // CHUNK
    qchunk = lax.broadcasted_iota(_I32, (tq, tq), 1) // CHUNK
    _static_sweep(nq, lambda t: jnp.where(qchunk >= kchunk, scores(t, t), -jnp.inf),
                  lambda t, s: update(t, t, s), sa_ref, sb_ref)
    if n_off:
        _static_sweep(n_off, lambda t: scores(jt_ref[t], it_ref[t]),
                      lambda t, s: update(jt_ref[t], it_ref[t], s), sa_ref, sb_ref)

    def finish(i, carry):
        off = pl.multiple_of(i * tq, tq)
        o_ref[pl.ds(off, tq), :] = _softmax_finish(acc_ref.at[i], A_V).T.astype(o_ref.dtype)
        return carry

    lax.fori_loop(0, nq, finish, 0)


def _mla_attention(qt, kn, kr, vt, cos_t, sin_t, batch, seq, tq):
    nq = seq // tq
    pairs = [(j, i) for j in range(nq) for i in range(j + 1, nq)]
    jt = jnp.asarray([p[0] for p in pairs] or [0], _I32)
    it = jnp.asarray([p[1] for p in pairs] or [0], _I32)
    half = A_ROPE // 2
    smem = pl.BlockSpec(memory_space=pltpu.SMEM)
    return pl.pallas_call(
        functools.partial(_mla_attn_kernel, nq=nq, tq=tq, n_off=len(pairs)),
        grid=(batch, A_HEADS),
        in_specs=[smem, smem,
                  pl.BlockSpec((nq, 2 * LANES, tq), lambda b, h: (b, h, 0)),
                  pl.BlockSpec((seq, LANES), lambda b, h: (b, h)),
                  pl.BlockSpec((seq, LANES), lambda b, h: (b, 0)),
                  pl.BlockSpec((nq, A_V, tq), lambda b, h: (b, h, 0)),
                  pl.BlockSpec((nq, half, tq), lambda b, h: (b, 0, 0)),
                  pl.BlockSpec((nq, half, tq), lambda b, h: (b, 0, 0))],
        out_specs=pl.BlockSpec((seq, A_V), lambda b, h: (b, h)),
        out_shape=jax.ShapeDtypeStruct((batch * seq, A_HEADS * A_V), MXU_DTYPE),
        scratch_shapes=[pltpu.VMEM((nq, 2 * LANES, tq), MXU_DTYPE),
                        pltpu.VMEM((nq, 1, tq), _F32),
                        pltpu.VMEM((nq, A_V + ONES_ROWS, tq), _F32),
                        pltpu.VMEM((tq, tq), _F32),
                        pltpu.VMEM((tq, tq), _F32)],
        compiler_params=_params(("parallel", "parallel")),
        name="mla_attention",
    )(jt, it, qt, kn, kr, vt, cos_t, sin_t)


def _pad_rope_cols(w, n_heads, nope):
    k = w.shape[0]
    w = w.reshape(k, n_heads, nope + A_ROPE)
    half = A_ROPE // 2
    z = jnp.zeros((k, n_heads, LANES // 2 - half), w.dtype)
    w = jnp.concatenate([w[..., :nope], w[..., nope:nope + half], z, w[..., nope + half:], z], axis=-1)
    return w.reshape(k, n_heads * (nope + LANES))


def _mla_mixer(x, positions, mix_g, w_in, q_norm, kv_norm, w_uq, w_ukv, w_out, batch, seq):
    assert A_NOPE == LANES and A_V == LANES and A_ROPE == LANES // 2
    tq = min(MLA_TQ, seq)
    n_lat = A_Q_LORA + A_KV_LORA
    w_in_p = jnp.concatenate([w_in[:, :n_lat], _pad_rope_cols(w_in[:, n_lat:], 1, 0)], axis=1)
    lat = _matmul_norm(x, mix_g, w_in_p.astype(MXU_DTYPE), _F32, bm=MM_BM // 2,
                       bn=w_in_p.shape[1])
    ra, rb = _rope_tables(positions, A_ROPE, LANES // 2)
    kr = _rope(lat, ra, rb, MXU_DTYPE, col_block=n_lat // LANES, width=LANES)
    qscale = (A_NOPE + A_ROPE) ** -0.5 * LOG2E
    w_qt = (_pad_rope_cols(w_uq, A_HEADS, A_NOPE) * qscale).T.astype(MXU_DTYPE)
    qt = _matmul_norm_t(w_qt, lat, q_norm, tq, MXU_DTYPE, bn=2 * MM_BN, row_tiles=2)
    w_ukv = w_ukv.reshape(A_KV_LORA, A_HEADS, A_NOPE + A_V)
    w_kn = w_ukv[..., :A_NOPE].reshape(A_KV_LORA, A_HEADS * A_NOPE)
    w_vt = w_ukv[..., A_NOPE:].reshape(A_KV_LORA, A_HEADS * A_V).T
    kv_blk = A_Q_LORA // A_KV_LORA
    kn = _matmul_norm(lat, kv_norm, w_kn.astype(MXU_DTYPE), MXU_DTYPE, bm=2 * MM_BM, bn=2 * MM_BN,
                      col_block=kv_blk)
    vt = _matmul_norm_t(w_vt.astype(MXU_DTYPE), lat, kv_norm, tq, MXU_DTYPE, bn=2 * MM_BN,
                        col_block=kv_blk, row_tiles=2)
    half = A_ROPE // 2
    cos_t = ra[:, :half].reshape(-1, tq, half).transpose(0, 2, 1)
    sin_t = rb[:, LANES // 2:LANES // 2 + half].reshape(-1, tq, half).transpose(0, 2, 1)
    o = _mla_attention(qt, kn, kr, vt, cos_t, sin_t, batch, seq, tq)
    return _matmul_residual(o, w_out.astype(MXU_DTYPE), x, 1.0)


def _dsa_kernel(qi_ref, wit_ref, ki_ref, ra_ref, rb_ref, q_ref, k_ref, vt_ref, o_ref,
                qit_ref, key_ref, qt_ref, m_ref, acc_ref, sa_ref, sb_ref,
                *, tq, tk, topk, group, wi_scale, qscale):
    i = pl.program_id(1)
    h = pl.program_id(2)
    nkb = (i + 1) * (tq // tk)
    ra = ra_ref[...]
    rb = rb_ref[...]

    @pl.when(h == 0)
    def _select():
        for hd in range(B_IDX_HEADS):
            sl = slice(hd * LANES, (hd + 1) * LANES)
            x = qi_ref[:, sl].astype(_F32)
            x = x * ra + _rot_half(x) * rb
            qit_ref[sl, :] = x.T.astype(qit_ref.dtype)
        wit = wit_ref[...] * wi_scale

        def score_tile(j, carry):
            off = pl.multiple_of(j * tk, tk)
            ki = ki_ref[pl.ds(off, tk), :]
            sc = jnp.zeros((tk, tq), _F32)
            for hd in range(B_IDX_HEADS):
                d = jnp.dot(ki, qit_ref[hd * LANES:(hd + 1) * LANES, :],
                            preferred_element_type=_F32)
                sc = sc + jnp.maximum(d, 0.0) * wit[hd:hd + 1, :]
            kchunk = (off + lax.broadcasted_iota(_I32, (tk, tq), 0)) // CHUNK
            qchunk = (i * tq + lax.broadcasted_iota(_I32, (tk, tq), 1)) // CHUNK
            sc = jnp.where(qchunk >= kchunk, sc, -jnp.inf)
            bits = pltpu.bitcast(sc, _I32)
            key_ref[j] = jnp.where(bits >= 0, bits, bits ^ 0x7FFFFFFF)
            return carry

        lax.fori_loop(0, nkb, score_tile, 0)

        def count(pred):
            def tile(j, acc):
                hit = jnp.where(pred(key_ref[j]), 1.0, 0.0)
                return acc + jnp.sum(hit.reshape(tk // SUBLANES, SUBLANES, tq), axis=0)

            def four_tiles(p, acc):
                for u in range(4):
                    acc = tile(4 * p + u, acc)
                return acc

            acc = lax.fori_loop(0, nkb // 4, four_tiles, jnp.zeros((SUBLANES, tq), _F32))
            acc = lax.fori_loop(nkb // 4 * 4, nkb, tile, acc)
            return jnp.sum(acc, axis=0, keepdims=True)

        def bisect(b, t):
            cand = t + jnp.left_shift(jnp.int32(1), 31 - b)
            return jnp.where(count(lambda kt: kt >= cand) >= topk, cand, t)

        t = lax.fori_loop(0, 32, bisect, jnp.full((1, tq), _INT_MIN, _I32))

        need = topk - count(lambda kt: kt > t)
        lower = jnp.where(lax.broadcasted_iota(_I32, (tk, tk), 0)
                          >= lax.broadcasted_iota(_I32, (tk, tk), 1), 1.0, 0.0).astype(MXU_DTYPE)

        def demote_tile(j, seen):
            kt = key_ref[j]
            eq = kt == t
            rank = seen + jnp.dot(lower, jnp.where(eq, 1.0, 0.0).astype(MXU_DTYPE),
                                  preferred_element_type=_F32)
            key_ref[j] = jnp.where(eq, jnp.where(rank > need, t - 1, kt), kt)
            return rank[tk - 1:tk, :]

        lax.fori_loop(0, nkb, demote_tile, jnp.zeros((1, tq), _F32))
        thr = jnp.maximum(t, _KEY_OF_NEG_INF + 1)

        def bias_tile(j, carry):
            bias = jnp.where(key_ref[j] >= thr, 0.0, _NEG_BIG)
            key_ref[j] = pltpu.bitcast(bias, _I32)
            return carry

        lax.fori_loop(0, nkb, bias_tile, 0)

    q = q_ref[...]
    for g in range(group):
        x = q[:, g * LANES:(g + 1) * LANES].astype(_F32)
        x = (x * ra + _rot_half(x) * rb) * qscale
        qt_ref[:, g * tq:(g + 1) * tq] = x.T.astype(qt_ref.dtype)
    m_ref[...] = jnp.full(m_ref.shape, _NEG_BIG, _F32)
    acc_ref[...] = jnp.zeros(acc_ref.shape, _F32)

    def score(j, first):
        off = pl.multiple_of(j * tk, tk)
        bias = pltpu.bitcast(key_ref[j], _F32)
        bias = jnp.concatenate([bias] * group, axis=1)
        return jnp.dot(k_ref[pl.ds(off, tk), :], qt_ref[...], preferred_element_type=_F32) + bias

    def update(j, s):
        _softmax_step(s, vt_ref[j], m_ref, acc_ref)

    _pipelined_sweep(nkb, score, update, sa_ref, sb_ref)
    out = _softmax_finish(acc_ref, B_HEAD_DIM)
    for g in range(group):
        o_ref[:, g * LANES:(g + 1) * LANES] = out[:, g * tq:(g + 1) * tq].T.astype(o_ref.dtype)


def _dsa_attention(proj, k_roped, vt, ki, wit, ra, rb, batch, seq, tq, tk):
    nq = seq // tq
    group = B_HEADS // B_KV_HEADS
    topk = min(B_TOPK_MAX, seq // 4)
    qi_w = B_IDX_HEADS * B_IDX_DIM
    q_w = group * B_HEAD_DIM
    q_blk0 = qi_w // q_w
    kern = functools.partial(
        _dsa_kernel, tq=tq, tk=tk, topk=float(topk), group=group,
        wi_scale=B_IDX_HEADS ** -0.5 * B_IDX_DIM ** -0.5, qscale=B_HEAD_DIM ** -0.5 * LOG2E)
    return pl.pallas_call(
        kern,
        grid=(batch, nq, B_KV_HEADS),
        in_specs=[pl.BlockSpec((tq, qi_w), lambda b, i, h: (b * nq + i, 0)),
                  pl.BlockSpec((B_IDX_HEADS, tq), lambda b, i, h: (0, b * nq + i)),
                  pl.BlockSpec((seq, LANES), lambda b, i, h: (b, 0)),
                  pl.BlockSpec((tq, LANES), lambda b, i, h: (b * nq + i, 0)),
                  pl.BlockSpec((tq, LANES), lambda b, i, h: (b * nq + i, 0)),
                  pl.BlockSpec((tq, q_w), lambda b, i, h: (b * nq + i, q_blk0 + h)),
                  pl.BlockSpec((seq, B_HEAD_DIM), lambda b, i, h: (b, h)),
                  pl.BlockSpec((seq // tk, B_HEAD_DIM, tk), lambda b, i, h: (b, h, 0))],
        out_specs=pl.BlockSpec((tq, q_w), lambda b, i, h: (b * nq + i, h)),
        out_shape=jax.ShapeDtypeStruct((batch * seq, B_HEADS * B_HEAD_DIM), MXU_DTYPE),
        scratch_shapes=[pltpu.VMEM((qi_w, tq), MXU_DTYPE),
                        pltpu.VMEM((seq // tk, tk, tq), _I32),
                        pltpu.VMEM((B_HEAD_DIM, group * tq), MXU_DTYPE),
                        pltpu.VMEM((1, group * tq), _F32),
                        pltpu.VMEM((B_HEAD_DIM + ONES_ROWS, group * tq), _F32),
                        pltpu.VMEM((tk, group * tq), _F32),
                        pltpu.VMEM((tk, group * tq), _F32)],
        compiler_params=_params(("parallel", "arbitrary", "arbitrary")),
        name="dsa_attention",
    )(proj, wit, ki, ra, rb, proj, k_roped, vt)


def _dsa_mixer(x, positions, mix_g, w_in, idx_k_norm, w_out, batch, seq):
    assert B_HEAD_DIM == LANES and B_IDX_DIM == LANES and B_IDX_HEADS % SUBLANES == 0
    tq = min(DSA_TQ, seq)
    tk = min(DSA_TK, tq)
    nq, nk = B_HEADS * B_HEAD_DIM, B_KV_HEADS * B_HEAD_DIM
    nqi = B_IDX_HEADS * B_IDX_DIM
    o_q, o_k, o_v, o_qi, o_ki, o_wi = np.cumsum([0, nq, nk, nk, nqi, B_IDX_DIM])
    w_main = jnp.concatenate([w_in[:, o_qi:o_ki], w_in[:, o_q:o_v]], axis=1)
    w_side = jnp.concatenate(
        [w_in[:, o_ki:o_wi], w_in[:, o_wi:],
         jnp.zeros((w_in.shape[0], LANES - B_IDX_HEADS), w_in.dtype)], axis=1)
    proj = _matmul_norm(x, mix_g, w_main.astype(MXU_DTYPE), MXU_DTYPE, bm=2 * MM_BM, bn=MM_BN // 2)
    side = _matmul_norm(x, mix_g, w_side.astype(MXU_DTYPE), _F32, bn=2 * LANES)
    vt = _matmul_norm_t(w_in[:, o_v:o_qi].T.astype(MXU_DTYPE), x, mix_g, tk, MXU_DTYPE)
    ra, rb = _rope_tables(positions, B_HEAD_DIM, LANES // 2)
    k_roped = _rope(proj, ra, rb, MXU_DTYPE, col_block=(nqi + nq) // nk, width=nk)
    ki = _rmsnorm_rope(side, idx_k_norm, ra, rb, MXU_DTYPE, col_block=0)
    wit = side[:, LANES:LANES + B_IDX_HEADS].T
    o = _dsa_attention(proj, k_roped, vt, ki, wit, ra, rb, batch, seq, tq, tk)
    return _matmul_residual(o, w_out.astype(MXU_DTYPE), x, 1.0)


def _log_sigmoid(x):
    return jnp.minimum(x, 0.0) - jnp.log(1.0 + jnp.exp(-jnp.abs(x)))


def _mlstm_kernel(bias_ref, q_ref, k_ref, v_ref, o_ref, grow_ref, gcol_ref, hn_ref, y_ref,
                  c_ref, n_ref, m_ref, *, n_heads, dk, dv):
    @pl.when(pl.program_id(1) == 0)
    def _init():
        c_ref[...] = jnp.zeros(c_ref.shape, _F32)
        n_ref[...] = jnp.zeros(n_ref.shape, _F32)
        m_ref[...] = jnp.zeros(m_ref.shape, _F32)

    grow = grow_ref[0]
    gcol = gcol_ref[0]
    rows = lax.broadcasted_iota(_I32, (CHUNK, CHUNK), 0)
    cols = lax.broadcasted_iota(_I32, (CHUNK, CHUNK), 1)
    tril = rows >= cols
    triu = rows <= cols

    for h in range(n_heads):
        b_i = bias_ref[h]
        b_f = bias_ref[n_heads + h]
        li_row = grow[h:h + 1, :] + b_i
        lf_row = _log_sigmoid(grow[n_heads + h:n_heads + h + 1, :] + b_f)
        li_col = gcol[:, h:h + 1] + b_i
        lf_col = _log_sigmoid(gcol[:, n_heads + h:n_heads + h + 1] + b_f)

        b_col = jnp.sum(jnp.where(tril, lf_row, 0.0), axis=1, keepdims=True)
        b_row = jnp.sum(jnp.where(triu, lf_col, 0.0), axis=0, keepdims=True)
        b_end = jnp.sum(lf_row, axis=1, keepdims=True)

        m_prev = m_ref[h]
        a = b_col + m_prev
        d = jnp.where(tril, b_col - b_row + li_row, -jnp.inf)
        m_t = jnp.maximum(a, jnp.max(d, axis=1, keepdims=True))
        inter = jnp.exp(a - m_t)

        q = q_ref[:, h * dk:(h + 1) * dk]
        k = k_ref[:, h * dk:(h + 1) * dk]
        v = v_ref[:, h * dv:(h + 1) * dv]
        qf = q.astype(_F32) * dk ** -0.5
        qs = qf.astype(MXU_DTYPE)
        qk = lax.dot_general(qs, k, (((1,), (1,)), ((), ())),
                             preferred_element_type=_F32) * jnp.exp(d - m_t)
        ct = c_ref[h]
        num = inter * jnp.dot(qs, ct.astype(MXU_DTYPE), preferred_element_type=_F32) + jnp.dot(
            qk.astype(MXU_DTYPE), v, preferred_element_type=_F32)
        n_row = n_ref[h]
        den = inter * jnp.sum(qf * n_row, axis=1, keepdims=True) + jnp.sum(qk, axis=1, keepdims=True)
        hs = num / jnp.maximum(jnp.abs(den), jnp.exp(-m_t))

        g_col = b_end - b_col + li_col
        m_new = jnp.maximum(b_end + m_prev, jnp.max(g_col, axis=0, keepdims=True))
        decay = jnp.exp(b_end + m_prev - m_new)
        wk = jnp.exp(g_col - m_new)
        vw = (v.astype(_F32) * wk).astype(MXU_DTYPE)
        c_ref[h] = decay * ct + lax.dot_general(k, vw, (((0,), (0,)), ((), ())),
                                                preferred_element_type=_F32)
        n_ref[h] = decay * n_row + jnp.sum(wk * k.astype(_F32), axis=0, keepdims=True)
        m_ref[h] = m_new

        hs = hs * lax.rsqrt(jnp.mean(hs * hs, axis=1, keepdims=True) + NORM_EPS)
        gate = jax.nn.sigmoid(o_ref[:, h * dv:(h + 1) * dv].astype(_F32))
        y_ref[:, h * dv:(h + 1) * dv] = (hs * hn_ref[:, h * dv:(h + 1) * dv] * gate).astype(y_ref.dtype)


def _mlstm_scan(proj, g_row, g_col, gate_bias, head_norm, batch, seq, dk, dv):
    nc = seq // CHUNK
    hq = C_HEADS
    qk_w, v_w = hq * dk, hq * dv
    v_blk0 = 2 * qk_w // v_w
    return pl.pallas_call(
        functools.partial(_mlstm_kernel, n_heads=hq, dk=dk, dv=dv),
        grid=(batch, nc),
        in_specs=[pl.BlockSpec(memory_space=pltpu.SMEM),
                  pl.BlockSpec((CHUNK, qk_w), lambda b, c: (b * nc + c, 0)),
                  pl.BlockSpec((CHUNK, qk_w), lambda b, c: (b * nc + c, 1)),
                  pl.BlockSpec((CHUNK, v_w), lambda b, c: (b * nc + c, v_blk0)),
                  pl.BlockSpec((CHUNK, v_w), lambda b, c: (b * nc + c, v_blk0 + 1)),
                  pl.BlockSpec((1, 2 * hq, CHUNK), lambda b, c: (b * nc + c, 0, 0)),
                  pl.BlockSpec((1, CHUNK, 2 * hq), lambda b, c: (b * nc + c, 0, 0)),
                  pl.BlockSpec((1, v_w), lambda b, c: (0, 0))],
        out_specs=pl.BlockSpec((CHUNK, v_w), lambda b, c: (b * nc + c, 0)),
        out_shape=jax.ShapeDtypeStruct((batch * seq, v_w), MXU_DTYPE),
        scratch_shapes=[pltpu.VMEM((hq, dk, dv), _F32),
                        pltpu.VMEM((hq, 1, dk), _F32),
                        pltpu.VMEM((hq, 1, 1), _F32)],
        compiler_params=_params(("parallel", "arbitrary")),
        name="mlstm_scan",
    )(gate_bias.astype(_F32), proj, proj, proj, proj, g_row, g_col,
      head_norm.reshape(1, v_w).astype(_F32))


def _mlstm_mixer(x, mix_g, w_in, gate_bias, head_norm, w_out, batch, seq):
    d_model = x.shape[1]
    dv = d_model // C_HEADS
    dk = dv // 2
    hq = C_HEADS
    n_main = 2 * hq * dk + 2 * hq * dv
    w_gate = jnp.concatenate(
        [w_in[:, n_main:], jnp.zeros((w_in.shape[0], LANES - 2 * hq), w_in.dtype)], axis=1)
    proj = _matmul_norm(x, mix_g, w_in[:, :n_main].astype(MXU_DTYPE), MXU_DTYPE,
                        bm=2 * MM_BM, bn=MM_BN // 2)
    gates = _matmul_norm(x, mix_g, w_gate.astype(MXU_DTYPE), _F32, bn=LANES)
    nc = seq // CHUNK
    g_col = gates[:, :2 * hq].reshape(batch * nc, CHUNK, 2 * hq)
    g_row = g_col.transpose(0, 2, 1)
    y = _mlstm_scan(proj, g_row, g_col, gate_bias, head_norm, batch, seq, dk, dv)
    return _matmul_residual(y, w_out.astype(MXU_DTYPE), x, 1.0)


def kernel(x, positions, ffn1_norm, ffn1_w_in, ffn1_w_out, mix_norm, ffn2_norm, ffn2_w_in, ffn2_w_out,
           a_w_in, a_q_norm, a_kv_norm, a_w_uq, a_w_ukv, a_w_out,
           b_w_in, b_idx_k_norm, b_w_out,
           c_w_in, c_gate_bias, c_head_norm, c_w_out, final_norm):
    batch, seq, d_model = x.shape
    depth = ffn1_norm.shape[0]
    x = x.reshape(batch * seq, d_model)
    ffn1_w_in, ffn1_w_out, ffn2_w_in, ffn2_w_out = (
        w.astype(MXU_DTYPE) for w in (ffn1_w_in, ffn1_w_out, ffn2_w_in, ffn2_w_out))
    for i in range(depth):
        x = _ffn(x, ffn1_norm[i], ffn1_w_in, ffn1_w_out, i)
        kind, j = i % N_MIXERS, i // N_MIXERS
        if kind == 0:
            x = _mla_mixer(x, positions, mix_norm[i], a_w_in[j], a_q_norm[j], a_kv_norm[j],
                           a_w_uq[j], a_w_ukv[j], a_w_out[j], batch, seq)
        elif kind == 1:
            x = _dsa_mixer(x, positions, mix_norm[i], b_w_in[j], b_idx_k_norm[j], b_w_out[j], batch, seq)
        else:
            x = _mlstm_mixer(x, mix_norm[i], c_w_in[j], c_gate_bias[j], c_head_norm[j], c_w_out[j],
                             batch, seq)
        x = _ffn(x, ffn2_norm[i], ffn2_w_in, ffn2_w_out, i)
    return _rmsnorm(x, final_norm, _F32).reshape(batch, seq, d_model)
```

```python
import functools

import jax
import jax.numpy as jnp
import numpy as np
from jax import lax
from jax.experimental import pallas as pl
from jax.experimental.pallas import tpu as pltpu

CHUNK = 64
ROPE_THETA = 10000.0
NORM_EPS = 1e-6
D_FF = 6144

A_HEADS = 64
A_Q_LORA = 1024
A_KV_LORA = 512
A_NOPE = 128
A_ROPE = 64
A_V = 128

B_HEADS = 32
B_KV_HEADS = 8
B_HEAD_DIM = 128
B_IDX_HEADS = 32
B_IDX_DIM = 128
B_TOPK_MAX = 256

C_HEADS = 8
N_MIXERS = 3

LANES = 128
SUBLANES = 8
ONES_ROWS = 16
LOG2E = 1.4426950408889634
MXU_DTYPE = jnp.bfloat16
VMEM_LIMIT_MB = 60

ROW_TILE = 256
MM_BM = 512
MM_BN = 1024
MM_RESIDUAL_VMEM_MB = 48
MLA_TQ = 512
DSA_TQ = 256
DSA_TK = 256

_F32 = jnp.float32
_I32 = jnp.int32
_NEG_BIG = -1e30
_INT_MIN = -(2 ** 31)
_KEY_OF_NEG_INF = -2139095041


def _params(semantics):
    return pltpu.CompilerParams(dimension_semantics=semantics,
                                vmem_limit_bytes=VMEM_LIMIT_MB << 20)


def _rot_half(x):
    return pltpu.roll(x, LANES // 2, 1)


def _rmsnorm_kernel(x_ref, g_ref, o_ref):
    x = x_ref[...].astype(_F32)
    y = x * lax.rsqrt(jnp.mean(x * x, axis=-1, keepdims=True) + NORM_EPS)
    o_ref[...] = (y * g_ref[...]).astype(o_ref.dtype)


def _rmsnorm(x, g, out_dtype, col_block=0, width=None):
    t = x.shape[0]
    width = x.shape[1] if width is None else width
    bt = min(ROW_TILE, t)
    return pl.pallas_call(
        _rmsnorm_kernel,
        grid=(t // bt,),
        in_specs=[pl.BlockSpec((bt, width), lambda i: (i, col_block)),
                  pl.BlockSpec((1, width), lambda i: (0, 0))],
        out_specs=pl.BlockSpec((bt, width), lambda i: (i, 0)),
        out_shape=jax.ShapeDtypeStruct((t, width), out_dtype),
        compiler_params=_params(("parallel",)),
        name="rmsnorm",
    )(x, g.reshape(1, width).astype(_F32))


def _rope_kernel(x_ref, ra_ref, rb_ref, o_ref, *, n_blocks):
    ra = ra_ref[...]
    rb = rb_ref[...]
    for c in range(n_blocks):
        sl = slice(c * LANES, (c + 1) * LANES)
        x = x_ref[:, sl].astype(_F32)
        o_ref[:, sl] = (x * ra + _rot_half(x) * rb).astype(o_ref.dtype)


def _rope(x, ra, rb, out_dtype, col_block=0, width=None):
    t = x.shape[0]
    width = x.shape[1] if width is None else width
    bt = min(ROW_TILE, t)
    return pl.pallas_call(
        functools.partial(_rope_kernel, n_blocks=width // LANES),
        grid=(t // bt,),
        in_specs=[pl.BlockSpec((bt, width), lambda i: (i, col_block)),
                  pl.BlockSpec((bt, LANES), lambda i: (i, 0)),
                  pl.BlockSpec((bt, LANES), lambda i: (i, 0))],
        out_specs=pl.BlockSpec((bt, width), lambda i: (i, 0)),
        out_shape=jax.ShapeDtypeStruct((t, width), out_dtype),
        compiler_params=_params(("parallel",)),
        name="rope",
    )(x, ra, rb)


def _rmsnorm_rope_kernel(x_ref, g_ref, ra_ref, rb_ref, o_ref):
    x = x_ref[...].astype(_F32)
    y = x * lax.rsqrt(jnp.mean(x * x, axis=-1, keepdims=True) + NORM_EPS) * g_ref[...]
    o_ref[...] = (y * ra_ref[...] + _rot_half(y) * rb_ref[...]).astype(o_ref.dtype)


def _rmsnorm_rope(x, g, ra, rb, out_dtype, col_block=0):
    t = x.shape[0]
    bt = min(ROW_TILE, t)
    return pl.pallas_call(
        _rmsnorm_rope_kernel,
        grid=(t // bt,),
        in_specs=[pl.BlockSpec((bt, LANES), lambda i: (i, col_block)),
                  pl.BlockSpec((1, LANES), lambda i: (0, 0)),
                  pl.BlockSpec((bt, LANES), lambda i: (i, 0)),
                  pl.BlockSpec((bt, LANES), lambda i: (i, 0))],
        out_specs=pl.BlockSpec((bt, LANES), lambda i: (i, 0)),
        out_shape=jax.ShapeDtypeStruct((t, LANES), out_dtype),
        compiler_params=_params(("parallel",)),
        name="rmsnorm_rope",
    )(x, g.reshape(1, LANES).astype(_F32), ra, rb)


def _normed(x_ref, g_ref, h_ref):
    @pl.when(pl.program_id(1) == 0)
    def _normalise():
        x = x_ref[...].astype(_F32)
        y = x * lax.rsqrt(jnp.mean(x * x, axis=-1, keepdims=True) + NORM_EPS)
        h_ref[...] = (y * g_ref[...]).astype(h_ref.dtype)

    return h_ref[...]


def _mm_norm_kernel(x_ref, g_ref, w_ref, o_ref, h_ref):
    o_ref[...] = jnp.dot(_normed(x_ref, g_ref, h_ref), w_ref[...],
                         preferred_element_type=_F32).astype(o_ref.dtype)


def _mm_norm_swiglu_kernel(x_ref, g_ref, wg_ref, wu_ref, o_ref, h_ref):
    h = _normed(x_ref, g_ref, h_ref)
    g = jnp.dot(h, wg_ref[...], preferred_element_type=_F32)
    u = jnp.dot(h, wu_ref[...], preferred_element_type=_F32)
    o_ref[...] = (g * jax.nn.sigmoid(g) * u).astype(o_ref.dtype)


def _mm_norm_t_kernel(w_ref, x_ref, g_ref, o_ref, h_ref):
    y = lax.dot_general(w_ref[...], _normed(x_ref, g_ref, h_ref), (((1,), (1,)), ((), ())),
                        preferred_element_type=_F32).astype(o_ref.dtype)
    tk = o_ref.shape[2]
    for r in range(o_ref.shape[0]):
        o_ref[r] = y[:, r * tk:(r + 1) * tk]


def _mm_residual_kernel(x_ref, w_ref, r_ref, o_ref, *, scale):
    y = jnp.dot(x_ref[...], w_ref[...], preferred_element_type=_F32)
    o_ref[...] = r_ref[...] + scale * y


def _tile(n, pref):
    return pref if n % pref == 0 else n


def _matmul_norm(x, g, w, out_dtype, bm=None, bn=None, col_block=0):
    m = x.shape[0]
    k, n = w.shape
    bm = _tile(m, bm or MM_BM)
    bn = _tile(n, bn or MM_BN)
    return pl.pallas_call(
        _mm_norm_kernel,
        grid=(m // bm, n // bn),
        in_specs=[pl.BlockSpec((bm, k), lambda i, j: (i, col_block)),
                  pl.BlockSpec((1, k), lambda i, j: (0, 0)),
                  pl.BlockSpec((k, bn), lambda i, j: (0, j))],
        out_specs=pl.BlockSpec((bm, bn), lambda i, j: (i, j)),
        out_shape=jax.ShapeDtypeStruct((m, n), out_dtype),
        scratch_shapes=[pltpu.VMEM((bm, k), MXU_DTYPE)],
        compiler_params=_params(("parallel", "arbitrary")),
        name="matmul_norm",
    )(x, g.reshape(1, k).astype(_F32), w)


def _matmul_norm_t(w_t, x, g, tk, out_dtype, bn=None, col_block=0, row_tiles=1):
    n, k = w_t.shape
    t = x.shape[0]
    bn = _tile(n, bn or MM_BN)
    rt = row_tiles if (t // tk) % row_tiles == 0 else 1
    return pl.pallas_call(
        _mm_norm_t_kernel,
        grid=(t // (rt * tk), n // bn),
        in_specs=[pl.BlockSpec((bn, k), lambda i, j: (j, 0)),
                  pl.BlockSpec((rt * tk, k), lambda i, j: (i, col_block)),
                  pl.BlockSpec((1, k), lambda i, j: (0, 0))],
        out_specs=pl.BlockSpec((rt, bn, tk), lambda i, j: (i, j, 0)),
        out_shape=jax.ShapeDtypeStruct((t // tk, n, tk), out_dtype),
        scratch_shapes=[pltpu.VMEM((rt * tk, k), MXU_DTYPE)],
        compiler_params=_params(("parallel", "arbitrary")),
        name="matmul_norm_t",
    )(w_t, x, g.reshape(1, k).astype(_F32))


def _matmul_norm_swiglu(x, g, w_in, layer=None, bm=None, bn=512):
    m, k = x.shape
    f = w_in.shape[-1] // 2
    bm = _tile(m, bm or MM_BM)
    bn = _tile(f, bn)
    nb = f // bn
    if layer is None:
        w_specs = [pl.BlockSpec((k, bn), lambda i, j: (0, j)),
                   pl.BlockSpec((k, bn), lambda i, j: (0, j + nb))]
    else:
        w_specs = [pl.BlockSpec((None, k, bn), lambda i, j: (layer, 0, j)),
                   pl.BlockSpec((None, k, bn), lambda i, j: (layer, 0, j + nb))]
    return pl.pallas_call(
        _mm_norm_swiglu_kernel,
        grid=(m // bm, nb),
        in_specs=[pl.BlockSpec((bm, k), lambda i, j: (i, 0)),
                  pl.BlockSpec((1, k), lambda i, j: (0, 0))] + w_specs,
        out_specs=pl.BlockSpec((bm, bn), lambda i, j: (i, j)),
        out_shape=jax.ShapeDtypeStruct((m, f), MXU_DTYPE),
        scratch_shapes=[pltpu.VMEM((bm, k), MXU_DTYPE)],
        compiler_params=_params(("parallel", "arbitrary")),
        name="matmul_norm_swiglu",
    )(x, g.reshape(1, k).astype(_F32), w_in, w_in)


def _matmul_residual(x, w, res, scale, layer=None, bn=512):
    m, k = x.shape
    n = w.shape[-1]
    bn = _tile(n, bn)
    bm = _tile(m, MM_BM)
    for cand in (2 * MM_BM,):
        windows = 2 * (cand * k * 2 + k * bn * 2 + 2 * cand * bn * 4)
        if m % cand == 0 and windows <= MM_RESIDUAL_VMEM_MB << 20:
            bm = cand
    if layer is None:
        w_spec = pl.BlockSpec((k, bn), lambda i, j: (0, j))
    else:
        w_spec = pl.BlockSpec((None, k, bn), lambda i, j: (layer, 0, j))
    return pl.pallas_call(
        functools.partial(_mm_residual_kernel, scale=scale),
        grid=(m // bm, n // bn),
        in_specs=[pl.BlockSpec((bm, k), lambda i, j: (i, 0)),
                  w_spec,
                  pl.BlockSpec((bm, bn), lambda i, j: (i, j))],
        out_specs=pl.BlockSpec((bm, bn), lambda i, j: (i, j)),
        out_shape=jax.ShapeDtypeStruct((m, n), _F32),
        compiler_params=_params(("parallel", "arbitrary")),
        name="matmul_residual",
    )(x, w, res)


def _ffn(x, norm_g, w_in_all, w_out_all, layer):
    act = _matmul_norm_swiglu(x, norm_g, w_in_all, layer, bn=3 * MM_BN // 4)
    return _matmul_residual(act, w_out_all, x, 0.5, layer)


def _rope_tables(positions, d, half_width):
    inv = jnp.power(ROPE_THETA, -jnp.arange(0, d, 2, dtype=_F32) / d)
    ang = positions.astype(_F32).reshape(-1, 1) * inv
    pad = jnp.zeros((ang.shape[0], half_width - d // 2), _F32)
    cos, sin = jnp.cos(ang), jnp.sin(ang)
    ra = jnp.concatenate([cos, pad, cos, pad], axis=1)
    rb = jnp.concatenate([-sin, pad, sin, pad], axis=1)
    return ra, rb


def _softmax_step(s, vt, m_ref, acc_ref):
    m_prev = m_ref[...]
    m_new = jnp.maximum(m_prev, jnp.max(s, axis=0, keepdims=True))
    alpha = jnp.exp2(m_prev - m_new)
    p = jnp.exp2(s - m_new).astype(MXU_DTYPE)
    vt1 = jnp.concatenate([vt, jnp.ones((ONES_ROWS, vt.shape[1]), vt.dtype)], axis=0)
    acc_ref[...] = alpha * acc_ref[...] + jnp.dot(vt1, p, preferred_element_type=_F32)
    m_ref[...] = m_new


def _pipelined_sweep(n, score, update, sa_ref, sb_ref):
    sa_ref[...] = score(0, True)

    def pair(t):
        sb_ref[...] = score(t + 1, False)
        update(t, sa_ref[...])
        sa_ref[...] = score(t + 2, False)
        update(t + 1, sb_ref[...])

    done = 0
    for size in (8, 4, 2):
        def body(p, carry, size=size, done=done):
            for u in range(size // 2):
                pair(done + size * p + 2 * u)
            return carry

        trips = (n - 1 - done) // size
        lax.fori_loop(0, trips, body, 0)
        done = done + size * trips
    t = done

    @pl.when(t == n - 1)
    def _odd_tail():
        update(t, sa_ref[...])

    @pl.when(t < n - 1)
    def _even_tail():
        sb_ref[...] = score(t + 1, False)
        update(t, sa_ref[...])
        update(t + 1, sb_ref[...])


def _static_sweep(n, score, update, sa_ref, sb_ref, unroll=8):
    sa_ref[...] = score(0)
    bufs = (sa_ref, sb_ref)

    def run(base, count, last):
        for u in range(count):
            cur, nxt = bufs[u % 2], bufs[(u + 1) % 2]
            if not (last and u == count - 1):
                nxt[...] = score(base + u + 1)
            update(base + u, cur[...])

    n_loop = (n - 1) // unroll * unroll

    def body(p, carry):
        run(p * unroll, unroll, False)
        return carry

    lax.fori_loop(0, n_loop // unroll, body, 0)
    run(n_loop, n - n_loop, True)


def _softmax_finish(acc_ref, dv):
    acc = acc_ref[...]
    return acc[:dv, :] / acc[dv:dv + 1, :]


def _mla_attn_kernel(jt_ref, it_ref, q_ref, kn_ref, kr_ref, vt_ref, cos_ref, sin_ref, o_ref,
                     qs_ref, m_ref, acc_ref, sa_ref, sb_ref, *, nq, tq, n_off):
    half = A_ROPE // 2
    r1 = slice(A_NOPE, A_NOPE + half)
    r2 = slice(A_NOPE + LANES // 2, A_NOPE + LANES // 2 + half)

    def prepare(i, carry):
        qt = q_ref[i]
        x1 = qt[r1, :].astype(_F32)
        x2 = qt[r2, :].astype(_F32)
        cos = cos_ref[i]
        sin = sin_ref[i]
        qs_ref[i] = qt
        qs_ref[i, r1, :] = (x1 * cos - x2 * sin).astype(qs_ref.dtype)
        qs_ref[i, r2, :] = (x1 * sin + x2 * cos).astype(qs_ref.dtype)
        m_ref[i] = jnp.full((1, tq), -jnp.inf, _F32)
        acc_ref[i] = jnp.zeros(acc_ref.shape[1:], _F32)
        return carry

    lax.fori_loop(0, nq, prepare, 0)

    def scores(j, i):
        off = pl.multiple_of(j * tq, tq)
        kc = jnp.concatenate([kn_ref[pl.ds(off, tq), :], kr_ref[pl.ds(off, tq), :]], axis=1)
        return jnp.dot(kc, qs_ref[i], preferred_element_type=_F32)

    def update(j, i, s):
        _softmax_step(s, vt_ref[j], m_ref.at[i], acc_ref.at[i])

    kchunk = lax.broadcasted_iota(_I32, (tq, tq), 0) // CHUNK
    qchunk = lax.broadcasted_iota(_I32, (tq, tq), 1) // CHUNK
    _static_sweep(nq, lambda t: jnp.where(qchunk >= kchunk, scores(t, t), -jnp.inf),
                  lambda t, s: update(t, t, s), sa_ref, sb_ref)
    if n_off:
        _static_sweep(n_off, lambda t: scores(jt_ref[t], it_ref[t]),
                      lambda t, s: update(jt_ref[t], it_ref[t], s), sa_ref, sb_ref)

    def finish(i, carry):
        off = pl.multiple_of(i * tq, tq)
        o_ref[pl.ds(off, tq), :] = _softmax_finish(acc_ref.at[i], A_V).T.astype(o_ref.dtype)
        return carry

    lax.fori_loop(0, nq, finish, 0)


def _mla_attention(qt, kn, kr, vt, cos_t, sin_t, batch, seq, tq):
    nq = seq // tq
    pairs = [(j, i) for j in range(nq) for i in range(j + 1, nq)]
    jt = jnp.asarray([p[0] for p in pairs] or [0], _I32)
    it = jnp.asarray([p[1] for p in pairs] or [0], _I32)
    half = A_ROPE // 2
    smem = pl.BlockSpec(memory_space=pltpu.SMEM)
    return pl.pallas_call(
        functools.partial(_mla_attn_kernel, nq=nq, tq=tq, n_off=len(pairs)),
        grid=(batch, A_HEADS),
        in_specs=[smem, smem,
                  pl.BlockSpec((nq, 2 * LANES, tq), lambda b, h: (b, h, 0)),
                  pl.BlockSpec((seq, LANES), lambda b, h: (b, h)),
                  pl.BlockSpec((seq, LANES), lambda b, h: (b, 0)),
                  pl.BlockSpec((nq, A_V, tq), lambda b, h: (b, h, 0)),
                  pl.BlockSpec((nq, half, tq), lambda b, h: (b, 0, 0)),
                  pl.BlockSpec((nq, half, tq), lambda b, h: (b, 0, 0))],
        out_specs=pl.BlockSpec((seq, A_V), lambda b, h: (b, h)),
        out_shape=jax.ShapeDtypeStruct((batch * seq, A_HEADS * A_V), MXU_DTYPE),
        scratch_shapes=[pltpu.VMEM((nq, 2 * LANES, tq), MXU_DTYPE),
                        pltpu.VMEM((nq, 1, tq), _F32),
                        pltpu.VMEM((nq, A_V + ONES_ROWS, tq), _F32),
                        pltpu.VMEM((tq, tq), _F32),
                        pltpu.VMEM((tq, tq), _F32)],
        compiler_params=_params(("parallel", "parallel")),
        name="mla_attention",
    )(jt, it, qt, kn, kr, vt, cos_t, sin_t)


def _pad_rope_cols(w, n_heads, nope):
    k = w.shape[0]
    w = w.reshape(k, n_heads, nope + A_ROPE)
    half = A_ROPE // 2
    z = jnp.zeros((k, n_heads, LANES // 2 - half), w.dtype)
    w = jnp.concatenate([w[..., :nope], w[..., nope:nope + half], z, w[..., nope + half:], z], axis=-1)
    return w.reshape(k, n_heads * (nope + LANES))


def _mla_mixer(x, positions, mix_g, w_in, q_norm, kv_norm, w_uq, w_ukv, w_out, batch, seq):
    assert A_NOPE == LANES and A_V == LANES and A_ROPE == LANES // 2
    tq = min(MLA_TQ, seq)
    n_lat = A_Q_LORA + A_KV_LORA
    w_in_p = jnp.concatenate([w_in[:, :n_lat], _pad_rope_cols(w_in[:, n_lat:], 1, 0)], axis=1)
    lat = _matmul_norm(x, mix_g, w_in_p.astype(MXU_DTYPE), _F32, bm=MM_BM // 2,
                       bn=w_in_p.shape[1])
    ra, rb = _rope_tables(positions, A_ROPE, LANES // 2)
    kr = _rope(lat, ra, rb, MXU_DTYPE, col_block=n_lat // LANES, width=LANES)
    qscale = (A_NOPE + A_ROPE) ** -0.5 * LOG2E
    w_qt = (_pad_rope_cols(w_uq, A_HEADS, A_NOPE) * qscale).T.astype(MXU_DTYPE)
    qt = _matmul_norm_t(w_qt, lat, q_norm, tq, MXU_DTYPE, bn=2 * MM_BN, row_tiles=2)
    w_ukv = w_ukv.reshape(A_KV_LORA, A_HEADS, A_NOPE + A_V)
    w_kn = w_ukv[..., :A_NOPE].reshape(A_KV_LORA, A_HEADS * A_NOPE)
    w_vt = w_ukv[..., A_NOPE:].reshape(A_KV_LORA, A_HEADS * A_V).T
    kv_blk = A_Q_LORA // A_KV_LORA
    kn = _matmul_norm(lat, kv_norm, w_kn.astype(MXU_DTYPE), MXU_DTYPE, bm=2 * MM_BM, bn=2 * MM_BN,
                      col_block=kv_blk)
    vt = _matmul_norm_t(w_vt.astype(MXU_DTYPE), lat, kv_norm, tq, MXU_DTYPE, bn=2 * MM_BN,
                        col_block=kv_blk, row_tiles=2)
    half = A_ROPE // 2
    cos_t = ra[:, :half].reshape(-1, tq, half).transpose(0, 2, 1)
    sin_t = rb[:, LANES // 2:LANES // 2 + half].reshape(-1, tq, half).transpose(0, 2, 1)
    o = _mla_attention(qt, kn, kr, vt, cos_t, sin_t, batch, seq, tq)
    return _matmul_residual(o, w_out.astype(MXU_DTYPE), x, 1.0)


def _dsa_kernel(qi_ref, wit_ref, ki_ref, ra_ref, rb_ref, q_ref, k_ref, vt_ref, o_ref,
                qit_ref, key_ref, qt_ref, m_ref, acc_ref, sa_ref, sb_ref,
                *, tq, tk, topk, group, wi_scale, qscale):
    i = pl.program_id(1)
    h = pl.program_id(2)
    nkb = (i + 1) * (tq // tk)
    ra = ra_ref[...]
    rb = rb_ref[...]

    @pl.when(h == 0)
    def _select():
        for hd in range(B_IDX_HEADS):
            sl = slice(hd * LANES, (hd + 1) * LANES)
            x = qi_ref[:, sl].astype(_F32)
            x = x * ra + _rot_half(x) * rb
            qit_ref[sl, :] = x.T.astype(qit_ref.dtype)
        wit = wit_ref[...] * wi_scale

        def score_tile(j, carry):
            off = pl.multiple_of(j * tk, tk)
            ki = ki_ref[pl.ds(off, tk), :]
            sc = jnp.zeros((tk, tq), _F32)
            for hd in range(B_IDX_HEADS):
                d = jnp.dot(ki, qit_ref[hd * LANES:(hd + 1) * LANES, :],
                            preferred_element_type=_F32)
                sc = sc + jnp.maximum(d, 0.0) * wit[hd:hd + 1, :]
            kchunk = (off + lax.broadcasted_iota(_I32, (tk, tq), 0)) // CHUNK
            qchunk = (i * tq + lax.broadcasted_iota(_I32, (tk, tq), 1)) // CHUNK
            sc = jnp.where(qchunk >= kchunk, sc, -jnp.inf)
            bits = pltpu.bitcast(sc, _I32)
            key_ref[j] = jnp.where(bits >= 0, bits, bits ^ 0x7FFFFFFF)
            return carry

        lax.fori_loop(0, nkb, score_tile, 0)

        def count(pred):
            def tile(j, acc):
                hit = jnp.where(pred(key_ref[j]), 1.0, 0.0)
                return acc + jnp.sum(hit.reshape(tk // SUBLANES, SUBLANES, tq), axis=0)

            def four_tiles(p, acc):
                for u in range(4):
                    acc = tile(4 * p + u, acc)
                return acc

            acc = lax.fori_loop(0, nkb // 4, four_tiles, jnp.zeros((SUBLANES, tq), _F32))
            acc = lax.fori_loop(nkb // 4 * 4, nkb, tile, acc)
            return jnp.sum(acc, axis=0, keepdims=True)

        def bisect(b, t):
            cand = t + jnp.left_shift(jnp.int32(1), 31 - b)
            return jnp.where(count(lambda kt: kt >= cand) >= topk, cand, t)

        t = lax.fori_loop(0, 32, bisect, jnp.full((1, tq), _INT_MIN, _I32))

        need = topk - count(lambda kt: kt > t)
        lower = jnp.where(lax.broadcasted_iota(_I32, (tk, tk), 0)
                          >= lax.broadcasted_iota(_I32, (tk, tk), 1), 1.0, 0.0).astype(MXU_DTYPE)

        def demote_tile(j, seen):
            kt = key_ref[j]
            eq = kt == t
            rank = seen + jnp.dot(lower, jnp.where(eq, 1.0, 0.0).astype(MXU_DTYPE),
                                  preferred_element_type=_F32)
            key_ref[j] = jnp.where(eq, jnp.where(rank > need, t - 1, kt), kt)
            return rank[tk - 1:tk, :]

        lax.fori_loop(0, nkb, demote_tile, jnp.zeros((1, tq), _F32))
        thr = jnp.maximum(t, _KEY_OF_NEG_INF + 1)

        def bias_tile(j, carry):
            bias = jnp.where(key_ref[j] >= thr, 0.0, _NEG_BIG)
            key_ref[j] = pltpu.bitcast(bias, _I32)
            return carry

        lax.fori_loop(0, nkb, bias_tile, 0)

    q = q_ref[...]
    for g in range(group):
        x = q[:, g * LANES:(g + 1) * LANES].astype(_F32)
        x = (x * ra + _rot_half(x) * rb) * qscale
        qt_ref[:, g * tq:(g + 1) * tq] = x.T.astype(qt_ref.dtype)
    m_ref[...] = jnp.full(m_ref.shape, _NEG_BIG, _F32)
    acc_ref[...] = jnp.zeros(acc_ref.shape, _F32)

    def score(j, first):
        off = pl.multiple_of(j * tk, tk)
        bias = pltpu.bitcast(key_ref[j], _F32)
        bias = jnp.concatenate([bias] * group, axis=1)
        return jnp.dot(k_ref[pl.ds(off, tk), :], qt_ref[...], preferred_element_type=_F32) + bias

    def update(j, s):
        _softmax_step(s, vt_ref[j], m_ref, acc_ref)

    _pipelined_sweep(nkb, score, update, sa_ref, sb_ref)
    out = _softmax_finish(acc_ref, B_HEAD_DIM)
    for g in range(group):
        o_ref[:, g * LANES:(g + 1) * LANES] = out[:, g * tq:(g + 1) * tq].T.astype(o_ref.dtype)


def _dsa_attention(proj, k_roped, vt, ki, wit, ra, rb, batch, seq, tq, tk):
    nq = seq // tq
    group = B_HEADS // B_KV_HEADS
    topk = min(B_TOPK_MAX, seq // 4)
    qi_w = B_IDX_HEADS * B_IDX_DIM
    q_w = group * B_HEAD_DIM
    q_blk0 = qi_w // q_w
    kern = functools.partial(
        _dsa_kernel, tq=tq, tk=tk, topk=float(topk), group=group,
        wi_scale=B_IDX_HEADS ** -0.5 * B_IDX_DIM ** -0.5, qscale=B_HEAD_DIM ** -0.5 * LOG2E)
    return pl.pallas_call(
        kern,
        grid=(batch, nq, B_KV_HEADS),
        in_specs=[pl.BlockSpec((tq, qi_w), lambda b, i, h: (b * nq + i, 0)),
                  pl.BlockSpec((B_IDX_HEADS, tq), lambda b, i, h: (0, b * nq + i)),
                  pl.BlockSpec((seq, LANES), lambda b, i, h: (b, 0)),
                  pl.BlockSpec((tq, LANES), lambda b, i, h: (b * nq + i, 0)),
                  pl.BlockSpec((tq, LANES), lambda b, i, h: (b * nq + i, 0)),
                  pl.BlockSpec((tq, q_w), lambda b, i, h: (b * nq + i, q_blk0 + h)),
                  pl.BlockSpec((seq, B_HEAD_DIM), lambda b, i, h: (b, h)),
                  pl.BlockSpec((seq // tk, B_HEAD_DIM, tk), lambda b, i, h: (b, h, 0))],
        out_specs=pl.BlockSpec((tq, q_w), lambda b, i, h: (b * nq + i, h)),
        out_shape=jax.ShapeDtypeStruct((batch * seq, B_HEADS * B_HEAD_DIM), MXU_DTYPE),
        scratch_shapes=[pltpu.VMEM((qi_w, tq), MXU_DTYPE),
                        pltpu.VMEM((seq // tk, tk, tq), _I32),
                        pltpu.VMEM((B_HEAD_DIM, group * tq), MXU_DTYPE),
                        pltpu.VMEM((1, group * tq), _F32),
                        pltpu.VMEM((B_HEAD_DIM + ONES_ROWS, group * tq), _F32),
                        pltpu.VMEM((tk, group * tq), _F32),
                        pltpu.VMEM((tk, group * tq), _F32)],
        compiler_params=_params(("parallel", "arbitrary", "arbitrary")),
        name="dsa_attention",
    )(proj, wit, ki, ra, rb, proj, k_roped, vt)


def _dsa_mixer(x, positions, mix_g, w_in, idx_k_norm, w_out, batch, seq):
    assert B_HEAD_DIM == LANES and B_IDX_DIM == LANES and B_IDX_HEADS % SUBLANES == 0
    tq = min(DSA_TQ, seq)
    tk = min(DSA_TK, tq)
    nq, nk = B_HEADS * B_HEAD_DIM, B_KV_HEADS * B_HEAD_DIM
    nqi = B_IDX_HEADS * B_IDX_DIM
    o_q, o_k, o_v, o_qi, o_ki, o_wi = np.cumsum([0, nq, nk, nk, nqi, B_IDX_DIM])
    w_main = jnp.concatenate([w_in[:, o_qi:o_ki], w_in[:, o_q:o_v]], axis=1)
    w_side = jnp.concatenate(
        [w_in[:, o_ki:o_wi], w_in[:, o_wi:],
         jnp.zeros((w_in.shape[0], LANES - B_IDX_HEADS), w_in.dtype)], axis=1)
    proj = _matmul_norm(x, mix_g, w_main.astype(MXU_DTYPE), MXU_DTYPE)
    side = _matmul_norm(x, mix_g, w_side.astype(MXU_DTYPE), _F32, bn=2 * LANES)
    vt = _matmul_norm_t(w_in[:, o_v:o_qi].T.astype(MXU_DTYPE), x, mix_g, tk, MXU_DTYPE)
    ra, rb = _rope_tables(positions, B_HEAD_DIM, LANES // 2)
    k_roped = _rope(proj, ra, rb, MXU_DTYPE, col_block=(nqi + nq) // nk, width=nk)
    ki = _rmsnorm_rope(side, idx_k_norm, ra, rb, MXU_DTYPE, col_block=0)
    wit = side[:, LANES:LANES + B_IDX_HEADS].T
    o = _dsa_attention(proj, k_roped, vt, ki, wit, ra, rb, batch, seq, tq, tk)
    return _matmul_residual(o, w_out.astype(MXU_DTYPE), x, 1.0)


def _log_sigmoid(x):
    return jnp.minimum(x, 0.0) - jnp.log(1.0 + jnp.exp(-jnp.abs(x)))


def _mlstm_kernel(bias_ref, q_ref, k_ref, v_ref, o_ref, grow_ref, gcol_ref, hn_ref, y_ref,
                  c_ref, n_ref, m_ref, *, n_heads, dk, dv):
    @pl.when(pl.program_id(1) == 0)
    def _init():
        c_ref[...] = jnp.zeros(c_ref.shape, _F32)
        n_ref[...] = jnp.zeros(n_ref.shape, _F32)
        m_ref[...] = jnp.zeros(m_ref.shape, _F32)

    grow = grow_ref[0]
    gcol = gcol_ref[0]
    rows = lax.broadcasted_iota(_I32, (CHUNK, CHUNK), 0)
    cols = lax.broadcasted_iota(_I32, (CHUNK, CHUNK), 1)
    tril = rows >= cols
    triu = rows <= cols

    for h in range(n_heads):
        b_i = bias_ref[h]
        b_f = bias_ref[n_heads + h]
        li_row = grow[h:h + 1, :] + b_i
        lf_row = _log_sigmoid(grow[n_heads + h:n_heads + h + 1, :] + b_f)
        li_col = gcol[:, h:h + 1] + b_i
        lf_col = _log_sigmoid(gcol[:, n_heads + h:n_heads + h + 1] + b_f)

        b_col = jnp.sum(jnp.where(tril, lf_row, 0.0), axis=1, keepdims=True)
        b_row = jnp.sum(jnp.where(triu, lf_col, 0.0), axis=0, keepdims=True)
        b_end = jnp.sum(lf_row, axis=1, keepdims=True)

        m_prev = m_ref[h]
        a = b_col + m_prev
        d = jnp.where(tril, b_col - b_row + li_row, -jnp.inf)
        m_t = jnp.maximum(a, jnp.max(d, axis=1, keepdims=True))
        inter = jnp.exp(a - m_t)

        q = q_ref[:, h * dk:(h + 1) * dk]
        k = k_ref[:, h * dk:(h + 1) * dk]
        v = v_ref[:, h * dv:(h + 1) * dv]
        qf = q.astype(_F32) * dk ** -0.5
        qs = qf.astype(MXU_DTYPE)
        qk = lax.dot_general(qs, k, (((1,), (1,)), ((), ())),
                             preferred_element_type=_F32) * jnp.exp(d - m_t)
        ct = c_ref[h]
        num = inter * jnp.dot(qs, ct.astype(MXU_DTYPE), preferred_element_type=_F32) + jnp.dot(
            qk.astype(MXU_DTYPE), v, preferred_element_type=_F32)
        n_row = n_ref[h]
        den = inter * jnp.sum(qf * n_row, axis=1, keepdims=True) + jnp.sum(qk, axis=1, keepdims=True)
        hs = num / jnp.maximum(jnp.abs(den), jnp.exp(-m_t))

        g_col = b_end - b_col + li_col
        m_new = jnp.maximum(b_end + m_prev, jnp.max(g_col, axis=0, keepdims=True))
        decay = jnp.exp(b_end + m_prev - m_new)
        wk = jnp.exp(g_col - m_new)
        vw = (v.astype(_F32) * wk).astype(MXU_DTYPE)
        c_ref[h] = decay * ct + lax.dot_general(k, vw, (((0,), (0,)), ((), ())),
                                                preferred_element_type=_F32)
        n_ref[h] = decay * n_row + jnp.sum(wk * k.astype(_F32), axis=0, keepdims=True)
        m_ref[h] = m_new

        hs = hs * lax.rsqrt(jnp.mean(hs * hs, axis=1, keepdims=True) + NORM_EPS)
        gate = jax.nn.sigmoid(o_ref[:, h * dv:(h + 1) * dv].astype(_F32))
        y_ref[:, h * dv:(h + 1) * dv] = (hs * hn_ref[:, h * dv:(h + 1) * dv] * gate).astype(y_ref.dtype)


def _mlstm_scan(proj, g_row, g_col, gate_bias, head_norm, batch, seq, dk, dv):
    nc = seq // CHUNK
    hq = C_HEADS
    qk_w, v_w = hq * dk, hq * dv
    v_blk0 = 2 * qk_w // v_w
    return pl.pallas_call(
        functools.partial(_mlstm_kernel, n_heads=hq, dk=dk, dv=dv),
        grid=(batch, nc),
        in_specs=[pl.BlockSpec(memory_space=pltpu.SMEM),
                  pl.BlockSpec((CHUNK, qk_w), lambda b, c: (b * nc + c, 0)),
                  pl.BlockSpec((CHUNK, qk_w), lambda b, c: (b * nc + c, 1)),
                  pl.BlockSpec((CHUNK, v_w), lambda b, c: (b * nc + c, v_blk0)),
                  pl.BlockSpec((CHUNK, v_w), lambda b, c: (b * nc + c, v_blk0 + 1)),
                  pl.BlockSpec((1, 2 * hq, CHUNK), lambda b, c: (b * nc + c, 0, 0)),
                  pl.BlockSpec((1, CHUNK, 2 * hq), lambda b, c: (b * nc + c, 0, 0)),
                  pl.BlockSpec((1, v_w), lambda b, c: (0, 0))],
        out_specs=pl.BlockSpec((CHUNK, v_w), lambda b, c: (b * nc + c, 0)),
        out_shape=jax.ShapeDtypeStruct((batch * seq, v_w), MXU_DTYPE),
        scratch_shapes=[pltpu.VMEM((hq, dk, dv), _F32),
                        pltpu.VMEM((hq, 1, dk), _F32),
                        pltpu.VMEM((hq, 1, 1), _F32)],
        compiler_params=_params(("parallel", "arbitrary")),
        name="mlstm_scan",
    )(gate_bias.astype(_F32), proj, proj, proj, proj, g_row, g_col,
      head_norm.reshape(1, v_w).astype(_F32))


def _mlstm_mixer(x, mix_g, w_in, gate_bias, head_norm, w_out, batch, seq):
    d_model = x.shape[1]
    dv = d_model // C_HEADS
    dk = dv // 2
    hq = C_HEADS
    n_main = 2 * hq * dk + 2 * hq * dv
    w_gate = jnp.concatenate(
        [w_in[:, n_main:], jnp.zeros((w_in.shape[0], LANES - 2 * hq), w_in.dtype)], axis=1)
    proj = _matmul_norm(x, mix_g, w_in[:, :n_main].astype(MXU_DTYPE), MXU_DTYPE)
    gates = _matmul_norm(x, mix_g, w_gate.astype(MXU_DTYPE), _F32, bn=LANES)
    nc = seq // CHUNK
    g_col = gates[:, :2 * hq].reshape(batch * nc, CHUNK, 2 * hq)
    g_row = g_col.transpose(0, 2, 1)
    y = _mlstm_scan(proj, g_row, g_col, gate_bias, head_norm, batch, seq, dk, dv)
    return _matmul_residual(y, w_out.astype(MXU_DTYPE), x, 1.0)


def kernel(x, positions, ffn1_norm, ffn1_w_in, ffn1_w_out, mix_norm, ffn2_norm, ffn2_w_in, ffn2_w_out,
           a_w_in, a_q_norm, a_kv_norm, a_w_uq, a_w_ukv, a_w_out,
           b_w_in, b_idx_k_norm, b_w_out,
           c_w_in, c_gate_bias, c_head_norm, c_w_out, final_norm):
    batch, seq, d_model = x.shape
    depth = ffn1_norm.shape[0]
    x = x.reshape(batch * seq, d_model)
    ffn1_w_in, ffn1_w_out, ffn2_w_in, ffn2_w_out = (
        w.astype(MXU_DTYPE) for w in (ffn1_w_in, ffn1_w_out, ffn2_w_in, ffn2_w_out))
    for i in range(depth):
        x = _ffn(x, ffn1_norm[i], ffn1_w_in, ffn1_w_out, i)
        kind, j = i % N_MIXERS, i // N_MIXERS
        if kind == 0:
            x = _mla_mixer(x, positions, mix_norm[i], a_w_in[j], a_q_norm[j], a_kv_norm[j],
                           a_w_uq[j], a_w_ukv[j], a_w_out[j], batch, seq)
        elif kind == 1:
            x = _dsa_mixer(x, positions, mix_norm[i], b_w_in[j], b_idx_k_norm[j], b_w_out[j], batch, seq)
        else:
            x = _mlstm_mixer(x, mix_norm[i], c_w_in[j], c_gate_bias[j], c_head_norm[j], c_w_out[j],
                             batch, seq)
        x = _ffn(x, ffn2_norm[i], ffn2_w_in, ffn2_w_out, i)
    return _rmsnorm(x, final_norm, _F32).reshape(batch, seq, d_model)
```

```python
import functools

import jax
import jax.numpy as jnp
import numpy as np
from jax import lax
from jax.experimental import pallas as pl
from jax.experimental.pallas import tpu as pltpu

CHUNK = 64
ROPE_THETA = 10000.0
NORM_EPS = 1e-6
D_FF = 6144

A_HEADS = 64
A_Q_LORA = 1024
A_KV_LORA = 512
A_NOPE = 128
A_ROPE = 64
A_V = 128

B_HEADS = 32
B_KV_HEADS = 8
B_HEAD_DIM = 128
B_IDX_HEADS = 32
B_IDX_DIM = 128
B_TOPK_MAX = 256

C_HEADS = 8
N_MIXERS = 3

LANES = 128
SUBLANES = 8
ONES_ROWS = 16
LOG2E = 1.4426950408889634
MXU_DTYPE = jnp.bfloat16
VMEM_LIMIT_MB = 60

ROW_TILE = 256
MM_BM = 512
MM_BN = 1024
MM_RESIDUAL_VMEM_MB = 48
MLA_TQ = 512
DSA_TQ = 256
DSA_TK = 256

_F32 = jnp.float32
_I32 = jnp.int32
_NEG_BIG = -1e30
_INT_MIN = -(2 ** 31)
_KEY_OF_NEG_INF = -2139095041


def _params(semantics):
    return pltpu.CompilerParams(dimension_semantics=semantics,
                                vmem_limit_bytes=VMEM_LIMIT_MB << 20)


def _rot_half(x):
    return pltpu.roll(x, LANES // 2, 1)


def _rmsnorm_kernel(x_ref, g_ref, o_ref):
    x = x_ref[...].astype(_F32)
    y = x * lax.rsqrt(jnp.mean(x * x, axis=-1, keepdims=True) + NORM_EPS)
    o_ref[...] = (y * g_ref[...]).astype(o_ref.dtype)


def _rmsnorm(x, g, out_dtype, col_block=0, width=None):
    t = x.shape[0]
    width = x.shape[1] if width is None else width
    bt = min(ROW_TILE, t)
    return pl.pallas_call(
        _rmsnorm_kernel,
        grid=(t // bt,),
        in_specs=[pl.BlockSpec((bt, width), lambda i: (i, col_block)),
                  pl.BlockSpec((1, width), lambda i: (0, 0))],
        out_specs=pl.BlockSpec((bt, width), lambda i: (i, 0)),
        out_shape=jax.ShapeDtypeStruct((t, width), out_dtype),
        compiler_params=_params(("parallel",)),
        name="rmsnorm",
    )(x, g.reshape(1, width).astype(_F32))


def _rope_kernel(x_ref, ra_ref, rb_ref, o_ref, *, n_blocks):
    ra = ra_ref[...]
    rb = rb_ref[...]
    for c in range(n_blocks):
        sl = slice(c * LANES, (c + 1) * LANES)
        x = x_ref[:, sl].astype(_F32)
        o_ref[:, sl] = (x * ra + _rot_half(x) * rb).astype(o_ref.dtype)


def _rope(x, ra, rb, out_dtype, col_block=0, width=None):
    t = x.shape[0]
    width = x.shape[1] if width is None else width
    bt = min(ROW_TILE, t)
    return pl.pallas_call(
        functools.partial(_rope_kernel, n_blocks=width // LANES),
        grid=(t // bt,),
        in_specs=[pl.BlockSpec((bt, width), lambda i: (i, col_block)),
                  pl.BlockSpec((bt, LANES), lambda i: (i, 0)),
                  pl.BlockSpec((bt, LANES), lambda i: (i, 0))],
        out_specs=pl.BlockSpec((bt, width), lambda i: (i, 0)),
        out_shape=jax.ShapeDtypeStruct((t, width), out_dtype),
        compiler_params=_params(("parallel",)),
        name="rope",
    )(x, ra, rb)


def _rmsnorm_rope_kernel(x_ref, g_ref, ra_ref, rb_ref, o_ref):
    x = x_ref[...].astype(_F32)
    y = x * lax.rsqrt(jnp.mean(x * x, axis=-1, keepdims=True) + NORM_EPS) * g_ref[...]
    o_ref[...] = (y * ra_ref[...] + _rot_half(y) * rb_ref[...]).astype(o_ref.dtype)


def _rmsnorm_rope(x, g, ra, rb, out_dtype, col_block=0):
    t = x.shape[0]
    bt = min(ROW_TILE, t)
    return pl.pallas_call(
        _rmsnorm_rope_kernel,
        grid=(t // bt,),
        in_specs=[pl.BlockSpec((bt, LANES), lambda i: (i, col_block)),
                  pl.BlockSpec((1, LANES), lambda i: (0, 0)),
                  pl.BlockSpec((bt, LANES), lambda i: (i, 0)),
                  pl.BlockSpec((bt, LANES), lambda i: (i, 0))],
        out_specs=pl.BlockSpec((bt, LANES), lambda i: (i, 0)),
        out_shape=jax.ShapeDtypeStruct((t, LANES), out_dtype),
        compiler_params=_params(("parallel",)),
        name="rmsnorm_rope",
    )(x, g.reshape(1, LANES).astype(_F32), ra, rb)


def _normed(x_ref, g_ref, h_ref):
    @pl.when(pl.program_id(1) == 0)
    def _normalise():
        x = x_ref[...].astype(_F32)
        y = x * lax.rsqrt(jnp.mean(x * x, axis=-1, keepdims=True) + NORM_EPS)
        h_ref[...] = (y * g_ref[...]).astype(h_ref.dtype)

    return h_ref[...]


def _mm_norm_kernel(x_ref, g_ref, w_ref, o_ref, h_ref):
    o_ref[...] = jnp.dot(_normed(x_ref, g_ref, h_ref), w_ref[...],
                         preferred_element_type=_F32).astype(o_ref.dtype)


def _mm_norm_swiglu_kernel(x_ref, g_ref, wg_ref, wu_ref, o_ref, h_ref):
    h = _normed(x_ref, g_ref, h_ref)
    g = jnp.dot(h, wg_ref[...], preferred_element_type=_F32)
    u = jnp.dot(h, wu_ref[...], preferred_element_type=_F32)
    o_ref[...] = (g * jax.nn.sigmoid(g) * u).astype(o_ref.dtype)


def _mm_norm_t_kernel(w_ref, x_ref, g_ref, o_ref, h_ref):
    y = lax.dot_general(w_ref[...], _normed(x_ref, g_ref, h_ref), (((1,), (1,)), ((), ())),
                        preferred_element_type=_F32).astype(o_ref.dtype)
    tk = o_ref.shape[2]
    for r in range(o_ref.shape[0]):
        o_ref[r] = y[:, r * tk:(r + 1) * tk]


def _mm_residual_kernel(x_ref, w_ref, r_ref, o_ref, *, scale):
    y = jnp.dot(x_ref[...], w_ref[...], preferred_element_type=_F32)
    o_ref[...] = r_ref[...] + scale * y


def _tile(n, pref):
    return pref if n % pref == 0 else n


def _matmul_norm(x, g, w, out_dtype, bm=None, bn=None, col_block=0):
    m = x.shape[0]
    k, n = w.shape
    bm = _tile(m, bm or MM_BM)
    bn = _tile(n, bn or MM_BN)
    return pl.pallas_call(
        _mm_norm_kernel,
        grid=(m // bm, n // bn),
        in_specs=[pl.BlockSpec((bm, k), lambda i, j: (i, col_block)),
                  pl.BlockSpec((1, k), lambda i, j: (0, 0)),
                  pl.BlockSpec((k, bn), lambda i, j: (0, j))],
        out_specs=pl.BlockSpec((bm, bn), lambda i, j: (i, j)),
        out_shape=jax.ShapeDtypeStruct((m, n), out_dtype),
        scratch_shapes=[pltpu.VMEM((bm, k), MXU_DTYPE)],
        compiler_params=_params(("parallel", "arbitrary")),
        name="matmul_norm",
    )(x, g.reshape(1, k).astype(_F32), w)


def _matmul_norm_t(w_t, x, g, tk, out_dtype, bn=None, col_block=0, row_tiles=1):
    n, k = w_t.shape
    t = x.shape[0]
    bn = _tile(n, bn or MM_BN)
    rt = row_tiles if (t // tk) % row_tiles == 0 else 1
    return pl.pallas_call(
        _mm_norm_t_kernel,
        grid=(t // (rt * tk), n // bn),
        in_specs=[pl.BlockSpec((bn, k), lambda i, j: (j, 0)),
                  pl.BlockSpec((rt * tk, k), lambda i, j: (i, col_block)),
                  pl.BlockSpec((1, k), lambda i, j: (0, 0))],
        out_specs=pl.BlockSpec((rt, bn, tk), lambda i, j: (i, j, 0)),
        out_shape=jax.ShapeDtypeStruct((t // tk, n, tk), out_dtype),
        scratch_shapes=[pltpu.VMEM((rt * tk, k), MXU_DTYPE)],
        compiler_params=_params(("parallel", "arbitrary")),
        name="matmul_norm_t",
    )(w_t, x, g.reshape(1, k).astype(_F32))


def _matmul_norm_swiglu(x, g, w_in, layer=None, bm=None, bn=512):
    m, k = x.shape
    f = w_in.shape[-1] // 2
    bm = _tile(m, bm or MM_BM)
    bn = _tile(f, bn)
    nb = f // bn
    if layer is None:
        w_specs = [pl.BlockSpec((k, bn), lambda i, j: (0, j)),
                   pl.BlockSpec((k, bn), lambda i, j: (0, j + nb))]
    else:
        w_specs = [pl.BlockSpec((None, k, bn), lambda i, j: (layer, 0, j)),
                   pl.BlockSpec((None, k, bn), lambda i, j: (layer, 0, j + nb))]
    return pl.pallas_call(
        _mm_norm_swiglu_kernel,
        grid=(m // bm, nb),
        in_specs=[pl.BlockSpec((bm, k), lambda i, j: (i, 0)),
                  pl.BlockSpec((1, k), lambda i, j: (0, 0))] + w_specs,
        out_specs=pl.BlockSpec((bm, bn), lambda i, j: (i, j)),
        out_shape=jax.ShapeDtypeStruct((m, f), MXU_DTYPE),
        scratch_shapes=[pltpu.VMEM((bm, k), MXU_DTYPE)],
        compiler_params=_params(("parallel", "arbitrary")),
        name="matmul_norm_swiglu",
    )(x, g.reshape(1, k).astype(_F32), w_in, w_in)


def _matmul_residual(x, w, res, scale, layer=None, bn=512):
    m, k = x.shape
    n = w.shape[-1]
    bn = _tile(n, bn)
    bm = _tile(m, MM_BM)
    for cand in (2 * MM_BM,):
        windows = 2 * (cand * k * 2 + k * bn * 2 + 2 * cand * bn * 4)
        if m % cand == 0 and windows <= MM_RESIDUAL_VMEM_MB << 20:
            bm = cand
    if layer is None:
        w_spec = pl.BlockSpec((k, bn), lambda i, j: (0, j))
    else:
        w_spec = pl.BlockSpec((None, k, bn), lambda i, j: (layer, 0, j))
    return pl.pallas_call(
        functools.partial(_mm_residual_kernel, scale=scale),
        grid=(m // bm, n // bn),
        in_specs=[pl.BlockSpec((bm, k), lambda i, j: (i, 0)),
                  w_spec,
                  pl.BlockSpec((bm, bn), lambda i, j: (i, j))],
        out_specs=pl.BlockSpec((bm, bn), lambda i, j: (i, j)),
        out_shape=jax.ShapeDtypeStruct((m, n), _F32),
        compiler_params=_params(("parallel", "arbitrary")),
        name="matmul_residual",
    )(x, w, res)


def _ffn(x, norm_g, w_in_all, w_out_all, layer):
    act = _matmul_norm_swiglu(x, norm_g, w_in_all, layer, bn=3 * MM_BN // 4)
    return _matmul_residual(act, w_out_all, x, 0.5, layer)


def _rope_tables(positions, d, half_width):
    inv = jnp.power(ROPE_THETA, -jnp.arange(0, d, 2, dtype=_F32) / d)
    ang = positions.astype(_F32).reshape(-1, 1) * inv
    pad = jnp.zeros((ang.shape[0], half_width - d // 2), _F32)
    cos, sin = jnp.cos(ang), jnp.sin(ang)
    ra = jnp.concatenate([cos, pad, cos, pad], axis=1)
    rb = jnp.concatenate([-sin, pad, sin, pad], axis=1)
    return ra, rb


def _softmax_step(s, vt, m_ref, acc_ref):
    m_prev = m_ref[...]
    m_new = jnp.maximum(m_prev, jnp.max(s, axis=0, keepdims=True))
    alpha = jnp.exp2(m_prev - m_new)
    p = jnp.exp2(s - m_new).astype(MXU_DTYPE)
    vt1 = jnp.concatenate([vt, jnp.ones((ONES_ROWS, vt.shape[1]), vt.dtype)], axis=0)
    acc_ref[...] = alpha * acc_ref[...] + jnp.dot(vt1, p, preferred_element_type=_F32)
    m_ref[...] = m_new


def _pipelined_sweep(n, score, update, sa_ref, sb_ref):
    sa_ref[...] = score(0, True)

    def pair(t):
        sb_ref[...] = score(t + 1, False)
        update(t, sa_ref[...])
        sa_ref[...] = score(t + 2, False)
        update(t + 1, sb_ref[...])

    done = 0
    for size in (8, 4, 2):
        def body(p, carry, size=size, done=done):
            for u in range(size // 2):
                pair(done + size * p + 2 * u)
            return carry

        trips = (n - 1 - done) // size
        lax.fori_loop(0, trips, body, 0)
        done = done + size * trips
    t = done

    @pl.when(t == n - 1)
    def _odd_tail():
        update(t, sa_ref[...])

    @pl.when(t < n - 1)
    def _even_tail():
        sb_ref[...] = score(t + 1, False)
        update(t, sa_ref[...])
        update(t + 1, sb_ref[...])


def _static_sweep(n, score, update, sa_ref, sb_ref, unroll=8):
    sa_ref[...] = score(0)
    bufs = (sa_ref, sb_ref)

    def run(base, count, last):
        for u in range(count):
            cur, nxt = bufs[u % 2], bufs[(u + 1) % 2]
            if not (last and u == count - 1):
                nxt[...] = score(base + u + 1)
            update(base + u, cur[...])

    n_loop = (n - 1) // unroll * unroll

    def body(p, carry):
        run(p * unroll, unroll, False)
        return carry

    lax.fori_loop(0, n_loop // unroll, body, 0)
    run(n_loop, n - n_loop, True)


def _softmax_finish(acc_ref, dv):
    acc = acc_ref[...]
    return acc[:dv, :] / acc[dv:dv + 1, :]


def _mla_attn_kernel(jt_ref, it_ref, q_ref, kn_ref, kr_ref, vt_ref, cos_ref, sin_ref, o_ref,
                     qs_ref, m_ref, acc_ref, sa_ref, sb_ref, *, nq, tq, n_off):
    half = A_ROPE // 2
    r1 = slice(A_NOPE, A_NOPE + half)
    r2 = slice(A_NOPE + LANES // 2, A_NOPE + LANES // 2 + half)

    def prepare(i, carry):
        qt = q_ref[i]
        x1 = qt[r1, :].astype(_F32)
        x2 = qt[r2, :].astype(_F32)
        cos = cos_ref[i]
        sin = sin_ref[i]
        qs_ref[i] = qt
        qs_ref[i, r1, :] = (x1 * cos - x2 * sin).astype(qs_ref.dtype)
        qs_ref[i, r2, :] = (x1 * sin + x2 * cos).astype(qs_ref.dtype)
        m_ref[i] = jnp.full((1, tq), -jnp.inf, _F32)
        acc_ref[i] = jnp.zeros(acc_ref.shape[1:], _F32)
        return carry

    lax.fori_loop(0, nq, prepare, 0)

    def scores(j, i):
        off = pl.multiple_of(j * tq, tq)
        kc = jnp.concatenate([kn_ref[pl.ds(off, tq), :], kr_ref[pl.ds(off, tq), :]], axis=1)
        return jnp.dot(kc, qs_ref[i], preferred_element_type=_F32)

    def update(j, i, s):
        _softmax_step(s, vt_ref[j], m_ref.at[i], acc_ref.at[i])

    kchunk = lax.broadcasted_iota(_I32, (tq, tq), 0) // CHUNK
    qchunk = lax.broadcasted_iota(_I32, (tq, tq), 1) // CHUNK
    _static_sweep(nq, lambda t: jnp.where(qchunk >= kchunk, scores(t, t), -jnp.inf),
                  lambda t, s: update(t, t, s), sa_ref, sb_ref)
    if n_off:
        _static_sweep(n_off, lambda t: scores(jt_ref[t], it_ref[t]),
                      lambda t, s: update(jt_ref[t], it_ref[t], s), sa_ref, sb_ref)

    def finish(i, carry):
        off = pl.multiple_of(i * tq, tq)
        o_ref[pl.ds(off, tq), :] = _softmax_finish(acc_ref.at[i], A_V).T.astype(o_ref.dtype)
        return carry

    lax.fori_loop(0, nq, finish, 0)


def _mla_attention(qt, kn, kr, vt, cos_t, sin_t, batch, seq, tq):
    nq = seq // tq
    pairs = [(j, i) for j in range(nq) for i in range(j + 1, nq)]
    jt = jnp.asarray([p[0] for p in pairs] or [0], _I32)
    it = jnp.asarray([p[1] for p in pairs] or [0], _I32)
    half = A_ROPE // 2
    smem = pl.BlockSpec(memory_space=pltpu.SMEM)
    return pl.pallas_call(
        functools.partial(_mla_attn_kernel, nq=nq, tq=tq, n_off=len(pairs)),
        grid=(batch, A_HEADS),
        in_specs=[smem, smem,
                  pl.BlockSpec((nq, 2 * LANES, tq), lambda b, h: (b, h, 0)),
                  pl.BlockSpec((seq, LANES), lambda b, h: (b, h)),
                  pl.BlockSpec((seq, LANES), lambda b, h: (b, 0)),
                  pl.BlockSpec((nq, A_V, tq), lambda b, h: (b, h, 0)),
                  pl.BlockSpec((nq, half, tq), lambda b, h: (b, 0, 0)),
                  pl.BlockSpec((nq, half, tq), lambda b, h: (b, 0, 0))],
        out_specs=pl.BlockSpec((seq, A_V), lambda b, h: (b, h)),
        out_shape=jax.ShapeDtypeStruct((batch * seq, A_HEADS * A_V), MXU_DTYPE),
        scratch_shapes=[pltpu.VMEM((nq, 2 * LANES, tq), MXU_DTYPE),
                        pltpu.VMEM((nq, 1, tq), _F32),
                        pltpu.VMEM((nq, A_V + ONES_ROWS, tq), _F32),
                        pltpu.VMEM((tq, tq), _F32),
                        pltpu.VMEM((tq, tq), _F32)],
        compiler_params=_params(("parallel", "parallel")),
        name="mla_attention",
    )(jt, it, qt, kn, kr, vt, cos_t, sin_t)


def _pad_rope_cols(w, n_heads, nope):
    k = w.shape[0]
    w = w.reshape(k, n_heads, nope + A_ROPE)
    half = A_ROPE // 2
    z = jnp.zeros((k, n_heads, LANES // 2 - half), w.dtype)
    w = jnp.concatenate([w[..., :nope], w[..., nope:nope + half], z, w[..., nope + half:], z], axis=-1)
    return w.reshape(k, n_heads * (nope + LANES))


def _mla_mixer(x, positions, mix_g, w_in, q_norm, kv_norm, w_uq, w_ukv, w_out, batch, seq):
    assert A_NOPE == LANES and A_V == LANES and A_ROPE == LANES // 2
    tq = min(MLA_TQ, seq)
    n_lat = A_Q_LORA + A_KV_LORA
    w_in_p = jnp.concatenate([w_in[:, :n_lat], _pad_rope_cols(w_in[:, n_lat:], 1, 0)], axis=1)
    lat = _matmul_norm(x, mix_g, w_in_p.astype(MXU_DTYPE), _F32, bm=MM_BM // 2,
                       bn=w_in_p.shape[1])
    ra, rb = _rope_tables(positions, A_ROPE, LANES // 2)
    kr = _rope(lat, ra, rb, MXU_DTYPE, col_block=n_lat // LANES, width=LANES)
    qscale = (A_NOPE + A_ROPE) ** -0.5 * LOG2E
    w_qt = (_pad_rope_cols(w_uq, A_HEADS, A_NOPE) * qscale).T.astype(MXU_DTYPE)
    qt = _matmul_norm_t(w_qt, lat, q_norm, tq, MXU_DTYPE, bn=2 * MM_BN, row_tiles=2)
    w_ukv = w_ukv.reshape(A_KV_LORA, A_HEADS, A_NOPE + A_V)
    w_kn = w_ukv[..., :A_NOPE].reshape(A_KV_LORA, A_HEADS * A_NOPE)
    w_vt = w_ukv[..., A_NOPE:].reshape(A_KV_LORA, A_HEADS * A_V).T
    kv_blk = A_Q_LORA // A_KV_LORA
    kn = _matmul_norm(lat, kv_norm, w_kn.astype(MXU_DTYPE), MXU_DTYPE, bm=2 * MM_BM, bn=2 * MM_BN,
                      col_block=kv_blk)
    vt = _matmul_norm_t(w_vt.astype(MXU_DTYPE), lat, kv_norm, tq, MXU_DTYPE, bn=2 * MM_BN,
                        col_block=kv_blk, row_tiles=2)
    half = A_ROPE // 2
    cos_t = ra[:, :half].reshape(-1, tq, half).transpose(0, 2, 1)
    sin_t = rb[:, LANES // 2:LANES // 2 + half].reshape(-1, tq, half).transpose(0, 2, 1)
    o = _mla_attention(qt, kn, kr, vt, cos_t, sin_t, batch, seq, tq)
    return _matmul_residual(o, w_out.astype(MXU_DTYPE), x, 1.0)


def _dsa_kernel(qi_ref, wit_ref, ki_ref, ra_ref, rb_ref, q_ref, k_ref, vt_ref, o_ref,
                qit_ref, key_ref, qt_ref, m_ref, acc_ref, sa_ref, sb_ref,
                *, tq, tk, topk, group, wi_scale, qscale):
    i = pl.program_id(1)
    h = pl.program_id(2)
    nkb = (i + 1) * (tq // tk)
    ra = ra_ref[...]
    rb = rb_ref[...]

    @pl.when(h == 0)
    def _select():
        for hd in range(B_IDX_HEADS):
            sl = slice(hd * LANES, (hd + 1) * LANES)
            x = qi_ref[:, sl].astype(_F32)
            x = x * ra + _rot_half(x) * rb
            qit_ref[sl, :] = x.T.astype(qit_ref.dtype)
        wit = wit_ref[...] * wi_scale

        def score_tile(j, carry):
            off = pl.multiple_of(j * tk, tk)
            ki = ki_ref[pl.ds(off, tk), :]
            sc = jnp.zeros((tk, tq), _F32)
            for hd in range(B_IDX_HEADS):
                d = jnp.dot(ki, qit_ref[hd * LANES:(hd + 1) * LANES, :],
                            preferred_element_type=_F32)
                sc = sc + jnp.maximum(d, 0.0) * wit[hd:hd + 1, :]
            kchunk = (off + lax.broadcasted_iota(_I32, (tk, tq), 0)) // CHUNK
            qchunk = (i * tq + lax.broadcasted_iota(_I32, (tk, tq), 1)) // CHUNK
            sc = jnp.where(qchunk >= kchunk, sc, -jnp.inf)
            bits = pltpu.bitcast(sc, _I32)
            key_ref[j] = jnp.where(bits >= 0, bits, bits ^ 0x7FFFFFFF)
            return carry

        lax.fori_loop(0, nkb, score_tile, 0)

        def count(pred):
            def tile(j, acc):
                hit = jnp.where(pred(key_ref[j]), 1.0, 0.0)
                return acc + jnp.sum(hit.reshape(tk // SUBLANES, SUBLANES, tq), axis=0)

            def four_tiles(p, acc):
                for u in range(4):
                    acc = tile(4 * p + u, acc)
                return acc

            acc = lax.fori_loop(0, nkb // 4, four_tiles, jnp.zeros((SUBLANES, tq), _F32))
            acc = lax.fori_loop(nkb // 4 * 4, nkb, tile, acc)
            return jnp.sum(acc, axis=0, keepdims=True)

        def bisect(b, t):
            cand = t + jnp.left_shift(jnp.int32(1), 31 - b)
            return jnp.where(count(lambda kt: kt >= cand) >= topk, cand, t)

        t = lax.fori_loop(0, 32, bisect, jnp.full((1, tq), _INT_MIN, _I32))

        need = topk - count(lambda kt: kt > t)
        lower = jnp.where(lax.broadcasted_iota(_I32, (tk, tk), 0)
                          >= lax.broadcasted_iota(_I32, (tk, tk), 1), 1.0, 0.0).astype(MXU_DTYPE)

        def demote_tile(j, seen):
            kt = key_ref[j]
            eq = kt == t
            rank = seen + jnp.dot(lower, jnp.where(eq, 1.0, 0.0).astype(MXU_DTYPE),
                                  preferred_element_type=_F32)
            key_ref[j] = jnp.where(eq, jnp.where(rank > need, t - 1, kt), kt)
            return rank[tk - 1:tk, :]

        lax.fori_loop(0, nkb, demote_tile, jnp.zeros((1, tq), _F32))
        thr = jnp.maximum(t, _KEY_OF_NEG_INF + 1)

        def bias_tile(j, carry):
            bias = jnp.where(key_ref[j] >= thr, 0.0, _NEG_BIG)
            key_ref[j] = pltpu.bitcast(bias, _I32)
            return carry

        lax.fori_loop(0, nkb, bias_tile, 0)

    q = q_ref[...]
    for g in range(group):
        x = q[:, g * LANES:(g + 1) * LANES].astype(_F32)
        x = (x * ra + _rot_half(x) * rb) * qscale
        qt_ref[:, g * tq:(g + 1) * tq] = x.T.astype(qt_ref.dtype)
    m_ref[...] = jnp.full(m_ref.shape, _NEG_BIG, _F32)
    acc_ref[...] = jnp.zeros(acc_ref.shape, _F32)

    def score(j, first):
        off = pl.multiple_of(j * tk, tk)
        bias = pltpu.bitcast(key_ref[j], _F32)
        bias = jnp.concatenate([bias] * group, axis=1)
        return jnp.dot(k_ref[pl.ds(off, tk), :], qt_ref[...], preferred_element_type=_F32) + bias

    def update(j, s):
        _softmax_step(s, vt_ref[j], m_ref, acc_ref)

    _pipelined_sweep(nkb, score, update, sa_ref, sb_ref)
    out = _softmax_finish(acc_ref, B_HEAD_DIM)
    for g in range(group):
        o_ref[:, g * LANES:(g + 1) * LANES] = out[:, g * tq:(g + 1) * tq].T.astype(o_ref.dtype)


def _dsa_attention(proj, k_roped, vt, ki, wit, ra, rb, batch, seq, tq, tk):
    nq = seq // tq
    group = B_HEADS // B_KV_HEADS
    topk = min(B_TOPK_MAX, seq // 4)
    qi_w = B_IDX_HEADS * B_IDX_DIM
    q_w = group * B_HEAD_DIM
    q_blk0 = qi_w // q_w
    kern = functools.partial(
        _dsa_kernel, tq=tq, tk=tk, topk=float(topk), group=group,
        wi_scale=B_IDX_HEADS ** -0.5 * B_IDX_DIM ** -0.5, qscale=B_HEAD_DIM ** -0.5 * LOG2E)
    return pl.pallas_call(
        kern,
        grid=(batch, nq, B_KV_HEADS),
        in_specs=[pl.BlockSpec((tq, qi_w), lambda b, i, h: (b * nq + i, 0)),
                  pl.BlockSpec((B_IDX_HEADS, tq), lambda b, i, h: (0, b * nq + i)),
                  pl.BlockSpec((seq, LANES), lambda b, i, h: (b, 0)),
                  pl.BlockSpec((tq, LANES), lambda b, i, h: (b * nq + i, 0)),
                  pl.BlockSpec((tq, LANES), lambda b, i, h: (b * nq + i, 0)),
                  pl.BlockSpec((tq, q_w), lambda b, i, h: (b * nq + i, q_blk0 + h)),
                  pl.BlockSpec((seq, B_HEAD_DIM), lambda b, i, h: (b, h)),
                  pl.BlockSpec((seq // tk, B_HEAD_DIM, tk), lambda b, i, h: (b, h, 0))],
        out_specs=pl.BlockSpec((tq, q_w), lambda b, i, h: (b * nq + i, h)),
        out_shape=jax.ShapeDtypeStruct((batch * seq, B_HEADS * B_HEAD_DIM), MXU_DTYPE),
        scratch_shapes=[pltpu.VMEM((qi_w, tq), MXU_DTYPE),
                        pltpu.VMEM((seq // tk, tk, tq), _I32),
                        pltpu.VMEM((B_HEAD_DIM, group * tq), MXU_DTYPE),
                        pltpu.VMEM((1, group * tq), _F32),
                        pltpu.VMEM((B_HEAD_DIM + ONES_ROWS, group * tq), _F32),
                        pltpu.VMEM((tk, group * tq), _F32),
                        pltpu.VMEM((tk, group * tq), _F32)],
        compiler_params=_params(("parallel", "arbitrary", "arbitrary")),
        name="dsa_attention",
    )(proj, wit, ki, ra, rb, proj, k_roped, vt)


def _dsa_mixer(x, positions, mix_g, w_in, idx_k_norm, w_out, batch, seq):
    assert B_HEAD_DIM == LANES and B_IDX_DIM == LANES and B_IDX_HEADS % SUBLANES == 0
    tq = min(DSA_TQ, seq)
    tk = min(DSA_TK, tq)
    nq, nk = B_HEADS * B_HEAD_DIM, B_KV_HEADS * B_HEAD_DIM
    nqi = B_IDX_HEADS * B_IDX_DIM
    o_q, o_k, o_v, o_qi, o_ki, o_wi = np.cumsum([0, nq, nk, nk, nqi, B_IDX_DIM])
    w_main = jnp.concatenate([w_in[:, o_qi:o_ki], w_in[:, o_q:o_v]], axis=1)
    w_side = jnp.concatenate(
        [w_in[:, o_ki:o_wi], w_in[:, o_wi:],
         jnp.zeros((w_in.shape[0], LANES - B_IDX_HEADS), w_in.dtype)], axis=1)
    proj = _matmul_norm(x, mix_g, w_main.astype(MXU_DTYPE), MXU_DTYPE, bn=3 * MM_BN // 2)
    side = _matmul_norm(x, mix_g, w_side.astype(MXU_DTYPE), _F32, bn=2 * LANES)
    vt = _matmul_norm_t(w_in[:, o_v:o_qi].T.astype(MXU_DTYPE), x, mix_g, tk, MXU_DTYPE)
    ra, rb = _rope_tables(positions, B_HEAD_DIM, LANES // 2)
    k_roped = _rope(proj, ra, rb, MXU_DTYPE, col_block=(nqi + nq) // nk, width=nk)
    ki = _rmsnorm_rope(side, idx_k_norm, ra, rb, MXU_DTYPE, col_block=0)
    wit = side[:, LANES:LANES + B_IDX_HEADS].T
    o = _dsa_attention(proj, k_roped, vt, ki, wit, ra, rb, batch, seq, tq, tk)
    return _matmul_residual(o, w_out.astype(MXU_DTYPE), x, 1.0)


def _log_sigmoid(x):
    return jnp.minimum(x, 0.0) - jnp.log(1.0 + jnp.exp(-jnp.abs(x)))


def _mlstm_kernel(bias_ref, q_ref, k_ref, v_ref, o_ref, grow_ref, gcol_ref, hn_ref, y_ref,
                  c_ref, n_ref, m_ref, *, n_heads, dk, dv):
    @pl.when(pl.program_id(1) == 0)
    def _init():
        c_ref[...] = jnp.zeros(c_ref.shape, _F32)
        n_ref[...] = jnp.zeros(n_ref.shape, _F32)
        m_ref[...] = jnp.zeros(m_ref.shape, _F32)

    grow = grow_ref[0]
    gcol = gcol_ref[0]
    rows = lax.broadcasted_iota(_I32, (CHUNK, CHUNK), 0)
    cols = lax.broadcasted_iota(_I32, (CHUNK, CHUNK), 1)
    tril = rows >= cols
    triu = rows <= cols

    for h in range(n_heads):
        b_i = bias_ref[h]
        b_f = bias_ref[n_heads + h]
        li_row = grow[h:h + 1, :] + b_i
        lf_row = _log_sigmoid(grow[n_heads + h:n_heads + h + 1, :] + b_f)
        li_col = gcol[:, h:h + 1] + b_i
        lf_col = _log_sigmoid(gcol[:, n_heads + h:n_heads + h + 1] + b_f)

        b_col = jnp.sum(jnp.where(tril, lf_row, 0.0), axis=1, keepdims=True)
        b_row = jnp.sum(jnp.where(triu, lf_col, 0.0), axis=0, keepdims=True)
        b_end = jnp.sum(lf_row, axis=1, keepdims=True)

        m_prev = m_ref[h]
        a = b_col + m_prev
        d = jnp.where(tril, b_col - b_row + li_row, -jnp.inf)
        m_t = jnp.maximum(a, jnp.max(d, axis=1, keepdims=True))
        inter = jnp.exp(a - m_t)

        q = q_ref[:, h * dk:(h + 1) * dk]
        k = k_ref[:, h * dk:(h + 1) * dk]
        v = v_ref[:, h * dv:(h + 1) * dv]
        qf = q.astype(_F32) * dk ** -0.5
        qs = qf.astype(MXU_DTYPE)
        qk = lax.dot_general(qs, k, (((1,), (1,)), ((), ())),
                             preferred_element_type=_F32) * jnp.exp(d - m_t)
        ct = c_ref[h]
        num = inter * jnp.dot(qs, ct.astype(MXU_DTYPE), preferred_element_type=_F32) + jnp.dot(
            qk.astype(MXU_DTYPE), v, preferred_element_type=_F32)
        n_row = n_ref[h]
        den = inter * jnp.sum(qf * n_row, axis=1, keepdims=True) + jnp.sum(qk, axis=1, keepdims=True)
        hs = num / jnp.maximum(jnp.abs(den), jnp.exp(-m_t))

        g_col = b_end - b_col + li_col
        m_new = jnp.maximum(b_end + m_prev, jnp.max(g_col, axis=0, keepdims=True))
        decay = jnp.exp(b_end + m_prev - m_new)
        wk = jnp.exp(g_col - m_new)
        vw = (v.astype(_F32) * wk).astype(MXU_DTYPE)
        c_ref[h] = decay * ct + lax.dot_general(k, vw, (((0,), (0,)), ((), ())),
                                                preferred_element_type=_F32)
        n_ref[h] = decay * n_row + jnp.sum(wk * k.astype(_F32), axis=0, keepdims=True)
        m_ref[h] = m_new

        hs = hs * lax.rsqrt(jnp.mean(hs * hs, axis=1, keepdims=True) + NORM_EPS)
        gate = jax.nn.sigmoid(o_ref[:, h * dv:(h + 1) * dv].astype(_F32))
        y_ref[:, h * dv:(h + 1) * dv] = (hs * hn_ref[:, h * dv:(h + 1) * dv] * gate).astype(y_ref.dtype)


def _mlstm_scan(proj, g_row, g_col, gate_bias, head_norm, batch, seq, dk, dv):
    nc = seq // CHUNK
    hq = C_HEADS
    qk_w, v_w = hq * dk, hq * dv
    v_blk0 = 2 * qk_w // v_w
    return pl.pallas_call(
        functools.partial(_mlstm_kernel, n_heads=hq, dk=dk, dv=dv),
        grid=(batch, nc),
        in_specs=[pl.BlockSpec(memory_space=pltpu.SMEM),
                  pl.BlockSpec((CHUNK, qk_w), lambda b, c: (b * nc + c, 0)),
                  pl.BlockSpec((CHUNK, qk_w), lambda b, c: (b * nc + c, 1)),
                  pl.BlockSpec((CHUNK, v_w), lambda b, c: (b * nc + c, v_blk0)),
                  pl.BlockSpec((CHUNK, v_w), lambda b, c: (b * nc + c, v_blk0 + 1)),
                  pl.BlockSpec((1, 2 * hq, CHUNK), lambda b, c: (b * nc + c, 0, 0)),
                  pl.BlockSpec((1, CHUNK, 2 * hq), lambda b, c: (b * nc + c, 0, 0)),
                  pl.BlockSpec((1, v_w), lambda b, c: (0, 0))],
        out_specs=pl.BlockSpec((CHUNK, v_w), lambda b, c: (b * nc + c, 0)),
        out_shape=jax.ShapeDtypeStruct((batch * seq, v_w), MXU_DTYPE),
        scratch_shapes=[pltpu.VMEM((hq, dk, dv), _F32),
                        pltpu.VMEM((hq, 1, dk), _F32),
                        pltpu.VMEM((hq, 1, 1), _F32)],
        compiler_params=_params(("parallel", "arbitrary")),
        name="mlstm_scan",
    )(gate_bias.astype(_F32), proj, proj, proj, proj, g_row, g_col,
      head_norm.reshape(1, v_w).astype(_F32))


def _mlstm_mixer(x, mix_g, w_in, gate_bias, head_norm, w_out, batch, seq):
    d_model = x.shape[1]
    dv = d_model // C_HEADS
    dk = dv // 2
    hq = C_HEADS
    n_main = 2 * hq * dk + 2 * hq * dv
    w_gate = jnp.concatenate(
        [w_in[:, n_main:], jnp.zeros((w_in.shape[0], LANES - 2 * hq), w_in.dtype)], axis=1)
    proj = _matmul_norm(x, mix_g, w_in[:, :n_main].astype(MXU_DTYPE), MXU_DTYPE,
                        bn=3 * MM_BN // 2)
    gates = _matmul_norm(x, mix_g, w_gate.astype(MXU_DTYPE), _F32, bn=LANES)
    nc = seq // CHUNK
    g_col = gates[:, :2 * hq].reshape(batch * nc, CHUNK, 2 * hq)
    g_row = g_col.transpose(0, 2, 1)
    y = _mlstm_scan(proj, g_row, g_col, gate_bias, head_norm, batch, seq, dk, dv)
    return _matmul_residual(y, w_out.astype(MXU_DTYPE), x, 1.0)


def kernel(x, positions, ffn1_norm, ffn1_w_in, ffn1_w_out, mix_norm, ffn2_norm, ffn2_w_in, ffn2_w_out,
           a_w_in, a_q_norm, a_kv_norm, a_w_uq, a_w_ukv, a_w_out,
           b_w_in, b_idx_k_norm, b_w_out,
           c_w_in, c_gate_bias, c_head_norm, c_w_out, final_norm):
    batch, seq, d_model = x.shape
    depth = ffn1_norm.shape[0]
    x = x.reshape(batch * seq, d_model)
    ffn1_w_in, ffn1_w_out, ffn2_w_in, ffn2_w_out = (
        w.astype(MXU_DTYPE) for w in (ffn1_w_in, ffn1_w_out, ffn2_w_in, ffn2_w_out))
    for i in range(depth):
        x = _ffn(x, ffn1_norm[i], ffn1_w_in, ffn1_w_out, i)
        kind, j = i % N_MIXERS, i // N_MIXERS
        if kind == 0:
            x = _mla_mixer(x, positions, mix_norm[i], a_w_in[j], a_q_norm[j], a_kv_norm[j],
                           a_w_uq[j], a_w_ukv[j], a_w_out[j], batch, seq)
        elif kind == 1:
            x = _dsa_mixer(x, positions, mix_norm[i], b_w_in[j], b_idx_k_norm[j], b_w_out[j], batch, seq)
        else:
            x = _mlstm_mixer(x, mix_norm[i], c_w_in[j], c_gate_bias[j], c_head_norm[j], c_w_out[j],
                             batch, seq)
        x = _ffn(x, ffn2_norm[i], ffn2_w_in, ffn2_w_out, i)
    return _rmsnorm(x, final_norm, _F32).reshape(batch, seq, d_model)
```

```python
import functools

import jax
import jax.numpy as jnp
import numpy as np
from jax import lax
from jax.experimental import pallas as pl
from jax.experimental.pallas import tpu as pltpu

CHUNK = 64
ROPE_THETA = 10000.0
NORM_EPS = 1e-6
D_FF = 6144

A_HEADS = 64
A_Q_LORA = 1024
A_KV_LORA = 512
A_NOPE = 128
A_ROPE = 64
A_V = 128

B_HEADS = 32
B_KV_HEADS = 8
B_HEAD_DIM = 128
B_IDX_HEADS = 32
B_IDX_DIM = 128
B_TOPK_MAX = 256

C_HEADS = 8
N_MIXERS = 3

LANES = 128
SUBLANES = 8
ONES_ROWS = 16
LOG2E = 1.4426950408889634
MXU_DTYPE = jnp.bfloat16
VMEM_LIMIT_MB = 60

ROW_TILE = 256
MM_BM = 512
MM_BN = 1024
MM_RESIDUAL_VMEM_MB = 48
MLA_TQ = 512
DSA_TQ = 256
DSA_TK = 256

_F32 = jnp.float32
_I32 = jnp.int32
_NEG_BIG = -1e30
_INT_MIN = -(2 ** 31)
_KEY_OF_NEG_INF = -2139095041


def _params(semantics):
    return pltpu.CompilerParams(dimension_semantics=semantics,
                                vmem_limit_bytes=VMEM_LIMIT_MB << 20)


def _rot_half(x):
    return pltpu.roll(x, LANES // 2, 1)


def _rmsnorm_kernel(x_ref, g_ref, o_ref):
    x = x_ref[...].astype(_F32)
    y = x * lax.rsqrt(jnp.mean(x * x, axis=-1, keepdims=True) + NORM_EPS)
    o_ref[...] = (y * g_ref[...]).astype(o_ref.dtype)


def _rmsnorm(x, g, out_dtype, col_block=0, width=None):
    t = x.shape[0]
    width = x.shape[1] if width is None else width
    bt = min(ROW_TILE, t)
    return pl.pallas_call(
        _rmsnorm_kernel,
        grid=(t // bt,),
        in_specs=[pl.BlockSpec((bt, width), lambda i: (i, col_block)),
                  pl.BlockSpec((1, width), lambda i: (0, 0))],
        out_specs=pl.BlockSpec((bt, width), lambda i: (i, 0)),
        out_shape=jax.ShapeDtypeStruct((t, width), out_dtype),
        compiler_params=_params(("parallel",)),
        name="rmsnorm",
    )(x, g.reshape(1, width).astype(_F32))


def _rope_kernel(x_ref, ra_ref, rb_ref, o_ref, *, n_blocks):
    ra = ra_ref[...]
    rb = rb_ref[...]
    for c in range(n_blocks):
        sl = slice(c * LANES, (c + 1) * LANES)
        x = x_ref[:, sl].astype(_F32)
        o_ref[:, sl] = (x * ra + _rot_half(x) * rb).astype(o_ref.dtype)


def _rope(x, ra, rb, out_dtype, col_block=0, width=None):
    t = x.shape[0]
    width = x.shape[1] if width is None else width
    bt = min(ROW_TILE, t)
    return pl.pallas_call(
        functools.partial(_rope_kernel, n_blocks=width // LANES),
        grid=(t // bt,),
        in_specs=[pl.BlockSpec((bt, width), lambda i: (i, col_block)),
                  pl.BlockSpec((bt, LANES), lambda i: (i, 0)),
                  pl.BlockSpec((bt, LANES), lambda i: (i, 0))],
        out_specs=pl.BlockSpec((bt, width), lambda i: (i, 0)),
        out_shape=jax.ShapeDtypeStruct((t, width), out_dtype),
        compiler_params=_params(("parallel",)),
        name="rope",
    )(x, ra, rb)


def _rmsnorm_rope_kernel(x_ref, g_ref, ra_ref, rb_ref, o_ref):
    x = x_ref[...].astype(_F32)
    y = x * lax.rsqrt(jnp.mean(x * x, axis=-1, keepdims=True) + NORM_EPS) * g_ref[...]
    o_ref[...] = (y * ra_ref[...] + _rot_half(y) * rb_ref[...]).astype(o_ref.dtype)


def _rmsnorm_rope(x, g, ra, rb, out_dtype, col_block=0):
    t = x.shape[0]
    bt = min(ROW_TILE, t)
    return pl.pallas_call(
        _rmsnorm_rope_kernel,
        grid=(t // bt,),
        in_specs=[pl.BlockSpec((bt, LANES), lambda i: (i, col_block)),
                  pl.BlockSpec((1, LANES), lambda i: (0, 0)),
                  pl.BlockSpec((bt, LANES), lambda i: (i, 0)),
                  pl.BlockSpec((bt, LANES), lambda i: (i, 0))],
        out_specs=pl.BlockSpec((bt, LANES), lambda i: (i, 0)),
        out_shape=jax.ShapeDtypeStruct((t, LANES), out_dtype),
        compiler_params=_params(("parallel",)),
        name="rmsnorm_rope",
    )(x, g.reshape(1, LANES).astype(_F32), ra, rb)


def _normed(x_ref, g_ref, h_ref):
    @pl.when(pl.program_id(1) == 0)
    def _normalise():
        x = x_ref[...].astype(_F32)
        y = x * lax.rsqrt(jnp.mean(x * x, axis=-1, keepdims=True) + NORM_EPS)
        h_ref[...] = (y * g_ref[...]).astype(h_ref.dtype)

    return h_ref[...]


def _mm_norm_kernel(x_ref, g_ref, w_ref, o_ref, h_ref):
    o_ref[...] = jnp.dot(_normed(x_ref, g_ref, h_ref), w_ref[...],
                         preferred_element_type=_F32).astype(o_ref.dtype)


def _mm_norm_swiglu_kernel(x_ref, g_ref, wg_ref, wu_ref, o_ref, h_ref):
    h = _normed(x_ref, g_ref, h_ref)
    g = jnp.dot(h, wg_ref[...], preferred_element_type=_F32)
    u = jnp.dot(h, wu_ref[...], preferred_element_type=_F32)
    o_ref[...] = (g * jax.nn.sigmoid(g) * u).astype(o_ref.dtype)


def _mm_norm_t_kernel(w_ref, x_ref, g_ref, o_ref, h_ref):
    y = lax.dot_general(w_ref[...], _normed(x_ref, g_ref, h_ref), (((1,), (1,)), ((), ())),
                        preferred_element_type=_F32).astype(o_ref.dtype)
    tk = o_ref.shape[2]
    for r in range(o_ref.shape[0]):
        o_ref[r] = y[:, r * tk:(r + 1) * tk]


def _mm_residual_kernel(x_ref, w_ref, r_ref, o_ref, *, scale):
    y = jnp.dot(x_ref[...], w_ref[...], preferred_element_type=_F32)
    o_ref[...] = r_ref[...] + scale * y


def _tile(n, pref):
    return pref if n % pref == 0 else n


def _matmul_norm(x, g, w, out_dtype, bm=None, bn=None, col_block=0):
    m = x.shape[0]
    k, n = w.shape
    bm = _tile(m, bm or MM_BM)
    bn = _tile(n, bn or MM_BN)
    return pl.pallas_call(
        _mm_norm_kernel,
        grid=(m // bm, n // bn),
        in_specs=[pl.BlockSpec((bm, k), lambda i, j: (i, col_block)),
                  pl.BlockSpec((1, k), lambda i, j: (0, 0)),
                  pl.BlockSpec((k, bn), lambda i, j: (0, j))],
        out_specs=pl.BlockSpec((bm, bn), lambda i, j: (i, j)),
        out_shape=jax.ShapeDtypeStruct((m, n), out_dtype),
        scratch_shapes=[pltpu.VMEM((bm, k), MXU_DTYPE)],
        compiler_params=_params(("parallel", "arbitrary")),
        name="matmul_norm",
    )(x, g.reshape(1, k).astype(_F32), w)


def _matmul_norm_t(w_t, x, g, tk, out_dtype, bn=None, col_block=0, row_tiles=1):
    n, k = w_t.shape
    t = x.shape[0]
    bn = _tile(n, bn or MM_BN)
    rt = row_tiles if (t // tk) % row_tiles == 0 else 1
    return pl.pallas_call(
        _mm_norm_t_kernel,
        grid=(t // (rt * tk), n // bn),
        in_specs=[pl.BlockSpec((bn, k), lambda i, j: (j, 0)),
                  pl.BlockSpec((rt * tk, k), lambda i, j: (i, col_block)),
                  pl.BlockSpec((1, k), lambda i, j: (0, 0))],
        out_specs=pl.BlockSpec((rt, bn, tk), lambda i, j: (i, j, 0)),
        out_shape=jax.ShapeDtypeStruct((t // tk, n, tk), out_dtype),
        scratch_shapes=[pltpu.VMEM((rt * tk, k), MXU_DTYPE)],
        compiler_params=_params(("parallel", "arbitrary")),
        name="matmul_norm_t",
    )(w_t, x, g.reshape(1, k).astype(_F32))


def _matmul_norm_swiglu(x, g, w_in, layer=None, bm=None, bn=512):
    m, k = x.shape
    f = w_in.shape[-1] // 2
    bm = _tile(m, bm or MM_BM)
    bn = _tile(f, bn)
    nb = f // bn
    if layer is None:
        w_specs = [pl.BlockSpec((k, bn), lambda i, j: (0, j)),
                   pl.BlockSpec((k, bn), lambda i, j: (0, j + nb))]
    else:
        w_specs = [pl.BlockSpec((None, k, bn), lambda i, j: (layer, 0, j)),
                   pl.BlockSpec((None, k, bn), lambda i, j: (layer, 0, j + nb))]
    return pl.pallas_call(
        _mm_norm_swiglu_kernel,
        grid=(m // bm, nb),
        in_specs=[pl.BlockSpec((bm, k), lambda i, j: (i, 0)),
                  pl.BlockSpec((1, k), lambda i, j: (0, 0))] + w_specs,
        out_specs=pl.BlockSpec((bm, bn), lambda i, j: (i, j)),
        out_shape=jax.ShapeDtypeStruct((m, f), MXU_DTYPE),
        scratch_shapes=[pltpu.VMEM((bm, k), MXU_DTYPE)],
        compiler_params=_params(("parallel", "arbitrary")),
        name="matmul_norm_swiglu",
    )(x, g.reshape(1, k).astype(_F32), w_in, w_in)


def _matmul_residual(x, w, res, scale, layer=None, bn=512):
    m, k = x.shape
    n = w.shape[-1]
    bn = _tile(n, bn)
    bm = _tile(m, MM_BM)
    for cand in (2 * MM_BM,):
        windows = 2 * (cand * k * 2 + k * bn * 2 + 2 * cand * bn * 4)
        if m % cand == 0 and windows <= MM_RESIDUAL_VMEM_MB << 20:
            bm = cand
    if layer is None:
        w_spec = pl.BlockSpec((k, bn), lambda i, j: (0, j))
    else:
        w_spec = pl.BlockSpec((None, k, bn), lambda i, j: (layer, 0, j))
    return pl.pallas_call(
        functools.partial(_mm_residual_kernel, scale=scale),
        grid=(m // bm, n // bn),
        in_specs=[pl.BlockSpec((bm, k), lambda i, j: (i, 0)),
                  w_spec,
                  pl.BlockSpec((bm, bn), lambda i, j: (i, j))],
        out_specs=pl.BlockSpec((bm, bn), lambda i, j: (i, j)),
        out_shape=jax.ShapeDtypeStruct((m, n), _F32),
        compiler_params=_params(("parallel", "arbitrary")),
        name="matmul_residual",
    )(x, w, res)


def _ffn(x, norm_g, w_in_all, w_out_all, layer):
    act = _matmul_norm_swiglu(x, norm_g, w_in_all, layer, bn=3 * MM_BN // 4)
    return _matmul_residual(act, w_out_all, x, 0.5, layer)


def _rope_tables(positions, d, half_width):
    inv = jnp.power(ROPE_THETA, -jnp.arange(0, d, 2, dtype=_F32) / d)
    ang = positions.astype(_F32).reshape(-1, 1) * inv
    pad = jnp.zeros((ang.shape[0], half_width - d // 2), _F32)
    cos, sin = jnp.cos(ang), jnp.sin(ang)
    ra = jnp.concatenate([cos, pad, cos, pad], axis=1)
    rb = jnp.concatenate([-sin, pad, sin, pad], axis=1)
    return ra, rb


def _softmax_step(s, vt, m_ref, acc_ref):
    m_prev = m_ref[...]
    m_new = jnp.maximum(m_prev, jnp.max(s, axis=0, keepdims=True))
    alpha = jnp.exp2(m_prev - m_new)
    p = jnp.exp2(s - m_new).astype(MXU_DTYPE)
    vt1 = jnp.concatenate([vt, jnp.ones((ONES_ROWS, vt.shape[1]), vt.dtype)], axis=0)
    acc_ref[...] = alpha * acc_ref[...] + jnp.dot(vt1, p, preferred_element_type=_F32)
    m_ref[...] = m_new


def _pipelined_sweep(n, score, update, sa_ref, sb_ref):
    sa_ref[...] = score(0, True)

    def pair(t):
        sb_ref[...] = score(t + 1, False)
        update(t, sa_ref[...])
        sa_ref[...] = score(t + 2, False)
        update(t + 1, sb_ref[...])

    done = 0
    for size in (16, 8, 4, 2):
        def body(p, carry, size=size, done=done):
            for u in range(size // 2):
                pair(done + size * p + 2 * u)
            return carry

        trips = (n - 1 - done) // size
        lax.fori_loop(0, trips, body, 0)
        done = done + size * trips
    t = done

    @pl.when(t == n - 1)
    def _odd_tail():
        update(t, sa_ref[...])

    @pl.when(t < n - 1)
    def _even_tail():
        sb_ref[...] = score(t + 1, False)
        update(t, sa_ref[...])
        update(t + 1, sb_ref[...])


def _static_sweep(n, score, update, sa_ref, sb_ref, unroll=24):
    sa_ref[...] = score(0)
    bufs = (sa_ref, sb_ref)

    def run(base, count, last):
        for u in range(count):
            cur, nxt = bufs[u % 2], bufs[(u + 1) % 2]
            if not (last and u == count - 1):
                nxt[...] = score(base + u + 1)
            update(base + u, cur[...])

    n_loop = (n - 1) // unroll * unroll

    def body(p, carry):
        run(p * unroll, unroll, False)
        return carry

    lax.fori_loop(0, n_loop // unroll, body, 0)
    run(n_loop, n - n_loop, True)


def _softmax_finish(acc_ref, dv):
    acc = acc_ref[...]
    return acc[:dv, :] / acc[dv:dv + 1, :]


def _mla_attn_kernel(jt_ref, it_ref, q_ref, kn_ref, kr_ref, vt_ref, cos_ref, sin_ref, o_ref,
                     qs_ref, m_ref, acc_ref, sa_ref, sb_ref, *, nq, tq, n_off):
    half = A_ROPE // 2
    r1 = slice(A_NOPE, A_NOPE + half)
    r2 = slice(A_NOPE + LANES // 2, A_NOPE + LANES // 2 + half)

    def prepare(i, carry):
        qt = q_ref[i]
        x1 = qt[r1, :].astype(_F32)
        x2 = qt[r2, :].astype(_F32)
        cos = cos_ref[i]
        sin = sin_ref[i]
        qs_ref[i] = qt
        qs_ref[i, r1, :] = (x1 * cos - x2 * sin).astype(qs_ref.dtype)
        qs_ref[i, r2, :] = (x1 * sin + x2 * cos).astype(qs_ref.dtype)
        m_ref[i] = jnp.full((1, tq), -jnp.inf, _F32)
        acc_ref[i] = jnp.zeros(acc_ref.shape[1:], _F32)
        return carry

    lax.fori_loop(0, nq, prepare, 0)

    def scores(j, i):
        off = pl.multiple_of(j * tq, tq)
        kc = jnp.concatenate([kn_ref[pl.ds(off, tq), :], kr_ref[pl.ds(off, tq), :]], axis=1)
        return jnp.dot(kc, qs_ref[i], preferred_element_type=_F32)

    def update(j, i, s):
        _softmax_step(s, vt_ref[j], m_ref.at[i], acc_ref.at[i])

    kchunk = lax.broadcasted_iota(_I32, (tq, tq), 0) // CHUNK
    qchunk = lax.broadcasted_iota(_I32, (tq, tq), 1) // CHUNK
    _static_sweep(nq, lambda t: jnp.where(qchunk >= kchunk, scores(t, t), -jnp.inf),
                  lambda t, s: update(t, t, s), sa_ref, sb_ref)
    if n_off:
        _static_sweep(n_off, lambda t: scores(jt_ref[t], it_ref[t]),
                      lambda t, s: update(jt_ref[t], it_ref[t], s), sa_ref, sb_ref)

    def finish(i, carry):
        off = pl.multiple_of(i * tq, tq)
        o_ref[pl.ds(off, tq), :] = _softmax_finish(acc_ref.at[i], A_V).T.astype(o_ref.dtype)
        return carry

    lax.fori_loop(0, nq, finish, 0)


def _mla_attention(qt, kn, kr, vt, cos_t, sin_t, batch, seq, tq):
    nq = seq // tq
    pairs = [(j, i) for j in range(nq) for i in range(j + 1, nq)]
    jt = jnp.asarray([p[0] for p in pairs] or [0], _I32)
    it = jnp.asarray([p[1] for p in pairs] or [0], _I32)
    half = A_ROPE // 2
    smem = pl.BlockSpec(memory_space=pltpu.SMEM)
    return pl.pallas_call(
        functools.partial(_mla_attn_kernel, nq=nq, tq=tq, n_off=len(pairs)),
        grid=(batch, A_HEADS),
        in_specs=[smem, smem,
                  pl.BlockSpec((nq, 2 * LANES, tq), lambda b, h: (b, h, 0)),
                  pl.BlockSpec((seq, LANES), lambda b, h: (b, h)),
                  pl.BlockSpec((seq, LANES), lambda b, h: (b, 0)),
                  pl.BlockSpec((nq, A_V, tq), lambda b, h: (b, h, 0)),
                  pl.BlockSpec((nq, half, tq), lambda b, h: (b, 0, 0)),
                  pl.BlockSpec((nq, half, tq), lambda b, h: (b, 0, 0))],
        out_specs=pl.BlockSpec((seq, A_V), lambda b, h: (b, h)),
        out_shape=jax.ShapeDtypeStruct((batch * seq, A_HEADS * A_V), MXU_DTYPE),
        scratch_shapes=[pltpu.VMEM((nq, 2 * LANES, tq), MXU_DTYPE),
                        pltpu.VMEM((nq, 1, tq), _F32),
                        pltpu.VMEM((nq, A_V + ONES_ROWS, tq), _F32),
                        pltpu.VMEM((tq, tq), _F32),
                        pltpu.VMEM((tq, tq), _F32)],
        compiler_params=_params(("parallel", "parallel")),
        name="mla_attention",
    )(jt, it, qt, kn, kr, vt, cos_t, sin_t)


def _pad_rope_cols(w, n_heads, nope):
    k = w.shape[0]
    w = w.reshape(k, n_heads, nope + A_ROPE)
    half = A_ROPE // 2
    z = jnp.zeros((k, n_heads, LANES // 2 - half), w.dtype)
    w = jnp.concatenate([w[..., :nope], w[..., nope:nope + half], z, w[..., nope + half:], z], axis=-1)
    return w.reshape(k, n_heads * (nope + LANES))


def _mla_mixer(x, positions, mix_g, w_in, q_norm, kv_norm, w_uq, w_ukv, w_out, batch, seq):
    assert A_NOPE == LANES and A_V == LANES and A_ROPE == LANES // 2
    tq = min(MLA_TQ, seq)
    assert tq % CHUNK == 0 and seq % tq == 0
    n_lat = A_Q_LORA + A_KV_LORA
    w_in_p = jnp.concatenate([w_in[:, :n_lat], _pad_rope_cols(w_in[:, n_lat:], 1, 0)], axis=1)
    lat = _matmul_norm(x, mix_g, w_in_p.astype(MXU_DTYPE), _F32, bm=MM_BM // 2,
                       bn=w_in_p.shape[1])
    ra, rb = _rope_tables(positions, A_ROPE, LANES // 2)
    kr = _rope(lat, ra, rb, MXU_DTYPE, col_block=n_lat // LANES, width=LANES)
    qscale = (A_NOPE + A_ROPE) ** -0.5 * LOG2E
    w_qt = (_pad_rope_cols(w_uq, A_HEADS, A_NOPE) * qscale).T.astype(MXU_DTYPE)
    qt = _matmul_norm_t(w_qt, lat, q_norm, tq, MXU_DTYPE, bn=2 * MM_BN, row_tiles=2)
    w_ukv = w_ukv.reshape(A_KV_LORA, A_HEADS, A_NOPE + A_V)
    w_kn = w_ukv[..., :A_NOPE].reshape(A_KV_LORA, A_HEADS * A_NOPE)
    w_vt = w_ukv[..., A_NOPE:].reshape(A_KV_LORA, A_HEADS * A_V).T
    kv_blk = A_Q_LORA // A_KV_LORA
    kn = _matmul_norm(lat, kv_norm, w_kn.astype(MXU_DTYPE), MXU_DTYPE, bm=2 * MM_BM, bn=2 * MM_BN,
                      col_block=kv_blk)
    vt = _matmul_norm_t(w_vt.astype(MXU_DTYPE), lat, kv_norm, tq, MXU_DTYPE, bn=2 * MM_BN,
                        col_block=kv_blk, row_tiles=2)
    half = A_ROPE // 2
    cos_t = ra[:, :half].reshape(-1, tq, half).transpose(0, 2, 1)
    sin_t = rb[:, LANES // 2:LANES // 2 + half].reshape(-1, tq, half).transpose(0, 2, 1)
    o = _mla_attention(qt, kn, kr, vt, cos_t, sin_t, batch, seq, tq)
    return _matmul_residual(o, w_out.astype(MXU_DTYPE), x, 1.0)


def _dsa_kernel(qi_ref, wit_ref, ki_ref, ra_ref, rb_ref, q_ref, k_ref, vt_ref, o_ref,
                qit_ref, key_ref, qt_ref, m_ref, acc_ref, sa_ref, sb_ref,
                *, tq, tk, topk, group, wi_scale, qscale):
    i = pl.program_id(1)
    h = pl.program_id(2)
    nkb = (i + 1) * (tq // tk)
    ra = ra_ref[...]
    rb = rb_ref[...]

    @pl.when(h == 0)
    def _select():
        for hd in range(B_IDX_HEADS):
            sl = slice(hd * LANES, (hd + 1) * LANES)
            x = qi_ref[:, sl].astype(_F32)
            x = x * ra + _rot_half(x) * rb
            qit_ref[sl, :] = x.T.astype(qit_ref.dtype)
        wit = wit_ref[...] * wi_scale

        def score_tile(j, carry):
            off = pl.multiple_of(j * tk, tk)
            ki = ki_ref[pl.ds(off, tk), :]
            sc = jnp.zeros((tk, tq), _F32)
            for hd in range(B_IDX_HEADS):
                d = jnp.dot(ki, qit_ref[hd * LANES:(hd + 1) * LANES, :],
                            preferred_element_type=_F32)
                sc = sc + jnp.maximum(d, 0.0) * wit[hd:hd + 1, :]
            kchunk = (off + lax.broadcasted_iota(_I32, (tk, tq), 0)) // CHUNK
            qchunk = (i * tq + lax.broadcasted_iota(_I32, (tk, tq), 1)) // CHUNK
            sc = jnp.where(qchunk >= kchunk, sc, -jnp.inf)
            bits = pltpu.bitcast(sc, _I32)
            key_ref[j] = jnp.where(bits >= 0, bits, bits ^ 0x7FFFFFFF)
            return carry

        lax.fori_loop(0, nkb, score_tile, 0)

        def count(pred):
            def tile(j, acc):
                hit = jnp.where(pred(key_ref[j]), 1.0, 0.0)
                return acc + jnp.sum(hit.reshape(tk // SUBLANES, SUBLANES, tq), axis=0)

            def four_tiles(p, acc):
                for u in range(4):
                    acc = tile(4 * p + u, acc)
                return acc

            acc = lax.fori_loop(0, nkb // 4, four_tiles, jnp.zeros((SUBLANES, tq), _F32))
            acc = lax.fori_loop(nkb // 4 * 4, nkb, tile, acc)
            return jnp.sum(acc, axis=0, keepdims=True)

        def bisect(b, t):
            cand = t + jnp.left_shift(jnp.int32(1), 31 - b)
            return jnp.where(count(lambda kt: kt >= cand) >= topk, cand, t)

        t = lax.fori_loop(0, 32, bisect, jnp.full((1, tq), _INT_MIN, _I32))

        need = topk - count(lambda kt: kt > t)
        lower = jnp.where(lax.broadcasted_iota(_I32, (tk, tk), 0)
                          >= lax.broadcasted_iota(_I32, (tk, tk), 1), 1.0, 0.0).astype(MXU_DTYPE)

        def demote_tile(j, seen):
            kt = key_ref[j]
            eq = kt == t
            rank = seen + jnp.dot(lower, jnp.where(eq, 1.0, 0.0).astype(MXU_DTYPE),
                                  preferred_element_type=_F32)
            key_ref[j] = jnp.where(eq, jnp.where(rank > need, t - 1, kt), kt)
            return rank[tk - 1:tk, :]

        lax.fori_loop(0, nkb, demote_tile, jnp.zeros((1, tq), _F32))
        thr = jnp.maximum(t, _KEY_OF_NEG_INF + 1)

        def bias_tile(j, carry):
            bias = jnp.where(key_ref[j] >= thr, 0.0, _NEG_BIG)
            key_ref[j] = pltpu.bitcast(bias, _I32)
            return carry

        lax.fori_loop(0, nkb, bias_tile, 0)

    q = q_ref[...]
    for g in range(group):
        x = q[:, g * LANES:(g + 1) * LANES].astype(_F32)
        x = (x * ra + _rot_half(x) * rb) * qscale
        qt_ref[:, g * tq:(g + 1) * tq] = x.T.astype(qt_ref.dtype)
    m_ref[...] = jnp.full(m_ref.shape, _NEG_BIG, _F32)
    acc_ref[...] = jnp.zeros(acc_ref.shape, _F32)

    def score(j, first):
        off = pl.multiple_of(j * tk, tk)
        bias = pltpu.bitcast(key_ref[j], _F32)
        bias = jnp.concatenate([bias] * group, axis=1)
        return jnp.dot(k_ref[pl.ds(off, tk), :], qt_ref[...], preferred_element_type=_F32) + bias

    def update(j, s):
        _softmax_step(s, vt_ref[j], m_ref, acc_ref)

    _pipelined_sweep(nkb, score, update, sa_ref, sb_ref)
    out = _softmax_finish(acc_ref, B_HEAD_DIM)
    for g in range(group):
        o_ref[:, g * LANES:(g + 1) * LANES] = out[:, g * tq:(g + 1) * tq].T.astype(o_ref.dtype)


def _dsa_attention(proj, k_roped, vt, ki, wit, ra, rb, batch, seq, tq, tk):
    nq = seq // tq
    group = B_HEADS // B_KV_HEADS
    topk = min(B_TOPK_MAX, seq // 4)
    qi_w = B_IDX_HEADS * B_IDX_DIM
    q_w = group * B_HEAD_DIM
    q_blk0 = qi_w // q_w
    kern = functools.partial(
        _dsa_kernel, tq=tq, tk=tk, topk=float(topk), group=group,
        wi_scale=B_IDX_HEADS ** -0.5 * B_IDX_DIM ** -0.5, qscale=B_HEAD_DIM ** -0.5 * LOG2E)
    return pl.pallas_call(
        kern,
        grid=(batch, nq, B_KV_HEADS),
        in_specs=[pl.BlockSpec((tq, qi_w), lambda b, i, h: (b * nq + i, 0)),
                  pl.BlockSpec((B_IDX_HEADS, tq), lambda b, i, h: (0, b * nq + i)),
                  pl.BlockSpec((seq, LANES), lambda b, i, h: (b, 0)),
                  pl.BlockSpec((tq, LANES), lambda b, i, h: (b * nq + i, 0)),
                  pl.BlockSpec((tq, LANES), lambda b, i, h: (b * nq + i, 0)),
                  pl.BlockSpec((tq, q_w), lambda b, i, h: (b * nq + i, q_blk0 + h)),
                  pl.BlockSpec((seq, B_HEAD_DIM), lambda b, i, h: (b, h)),
                  pl.BlockSpec((seq // tk, B_HEAD_DIM, tk), lambda b, i, h: (b, h, 0))],
        out_specs=pl.BlockSpec((tq, q_w), lambda b, i, h: (b * nq + i, h)),
        out_shape=jax.ShapeDtypeStruct((batch * seq, B_HEADS * B_HEAD_DIM), MXU_DTYPE),
        scratch_shapes=[pltpu.VMEM((qi_w, tq), MXU_DTYPE),
                        pltpu.VMEM((seq // tk, tk, tq), _I32),
                        pltpu.VMEM((B_HEAD_DIM, group * tq), MXU_DTYPE),
                        pltpu.VMEM((1, group * tq), _F32),
                        pltpu.VMEM((B_HEAD_DIM + ONES_ROWS, group * tq), _F32),
                        pltpu.VMEM((tk, group * tq), _F32),
                        pltpu.VMEM((tk, group * tq), _F32)],
        compiler_params=_params(("parallel", "arbitrary", "arbitrary")),
        name="dsa_attention",
    )(proj, wit, ki, ra, rb, proj, k_roped, vt)


def _dsa_mixer(x, positions, mix_g, w_in, idx_k_norm, w_out, batch, seq):
    assert B_HEAD_DIM == LANES and B_IDX_DIM == LANES and B_IDX_HEADS % SUBLANES == 0
    tq = min(DSA_TQ, seq)
    tk = min(DSA_TK, tq)
    assert tq % CHUNK == 0 and tq % tk == 0 and seq % tq == 0
    nq, nk = B_HEADS * B_HEAD_DIM, B_KV_HEADS * B_HEAD_DIM
    nqi = B_IDX_HEADS * B_IDX_DIM
    o_q, o_k, o_v, o_qi, o_ki, o_wi = np.cumsum([0, nq, nk, nk, nqi, B_IDX_DIM])
    w_main = jnp.concatenate([w_in[:, o_qi:o_ki], w_in[:, o_q:o_v]], axis=1)
    w_side = jnp.concatenate(
        [w_in[:, o_ki:o_wi], w_in[:, o_wi:],
         jnp.zeros((w_in.shape[0], LANES - B_IDX_HEADS), w_in.dtype)], axis=1)
    proj = _matmul_norm(x, mix_g, w_main.astype(MXU_DTYPE), MXU_DTYPE, bn=3 * MM_BN // 2)
    side = _matmul_norm(x, mix_g, w_side.astype(MXU_DTYPE), _F32, bn=2 * LANES)
    vt = _matmul_norm_t(w_in[:, o_v:o_qi].T.astype(MXU_DTYPE), x, mix_g, tk, MXU_DTYPE)
    ra, rb = _rope_tables(positions, B_HEAD_DIM, LANES // 2)
    k_roped = _rope(proj, ra, rb, MXU_DTYPE, col_block=(nqi + nq) // nk, width=nk)
    ki = _rmsnorm_rope(side, idx_k_norm, ra, rb, MXU_DTYPE, col_block=0)
    wit = side[:, LANES:LANES + B_IDX_HEADS].T
    o = _dsa_attention(proj, k_roped, vt, ki, wit, ra, rb, batch, seq, tq, tk)
    return _matmul_residual(o, w_out.astype(MXU_DTYPE), x, 1.0)


def _log_sigmoid(x):
    return jnp.minimum(x, 0.0) - jnp.log(1.0 + jnp.exp(-jnp.abs(x)))


def _mlstm_kernel(bias_ref, q_ref, k_ref, v_ref, o_ref, grow_ref, gcol_ref, hn_ref, y_ref,
                  c_ref, n_ref, m_ref, *, n_heads, dk, dv):
    @pl.when(pl.program_id(1) == 0)
    def _init():
        c_ref[...] = jnp.zeros(c_ref.shape, _F32)
        n_ref[...] = jnp.zeros(n_ref.shape, _F32)
        m_ref[...] = jnp.zeros(m_ref.shape, _F32)

    grow = grow_ref[0]
    gcol = gcol_ref[0]
    rows = lax.broadcasted_iota(_I32, (CHUNK, CHUNK), 0)
    cols = lax.broadcasted_iota(_I32, (CHUNK, CHUNK), 1)
    tril = rows >= cols
    triu = rows <= cols

    for h in range(n_heads):
        b_i = bias_ref[h]
        b_f = bias_ref[n_heads + h]
        li_row = grow[h:h + 1, :] + b_i
        lf_row = _log_sigmoid(grow[n_heads + h:n_heads + h + 1, :] + b_f)
        li_col = gcol[:, h:h + 1] + b_i
        lf_col = _log_sigmoid(gcol[:, n_heads + h:n_heads + h + 1] + b_f)

        b_col = jnp.sum(jnp.where(tril, lf_row, 0.0), axis=1, keepdims=True)
        b_row = jnp.sum(jnp.where(triu, lf_col, 0.0), axis=0, keepdims=True)
        b_end = jnp.sum(lf_row, axis=1, keepdims=True)

        m_prev = m_ref[h]
        a = b_col + m_prev
        d = jnp.where(tril, b_col - b_row + li_row, -jnp.inf)
        m_t = jnp.maximum(a, jnp.max(d, axis=1, keepdims=True))
        inter = jnp.exp(a - m_t)

        q = q_ref[:, h * dk:(h + 1) * dk]
        k = k_ref[:, h * dk:(h + 1) * dk]
        v = v_ref[:, h * dv:(h + 1) * dv]
        qf = q.astype(_F32) * dk ** -0.5
        qs = qf.astype(MXU_DTYPE)
        qk = lax.dot_general(qs, k, (((1,), (1,)), ((), ())),
                             preferred_element_type=_F32) * jnp.exp(d - m_t)
        ct = c_ref[h]
        num = inter * jnp.dot(qs, ct.astype(MXU_DTYPE), preferred_element_type=_F32) + jnp.dot(
            qk.astype(MXU_DTYPE), v, preferred_element_type=_F32)
        n_row = n_ref[h]
        den = inter * jnp.sum(qf * n_row, axis=1, keepdims=True) + jnp.sum(qk, axis=1, keepdims=True)
        hs = num / jnp.maximum(jnp.abs(den), jnp.exp(-m_t))

        g_col = b_end - b_col + li_col
        m_new = jnp.maximum(b_end + m_prev, jnp.max(g_col, axis=0, keepdims=True))
        decay = jnp.exp(b_end + m_prev - m_new)
        wk = jnp.exp(g_col - m_new)
        vw = (v.astype(_F32) * wk).astype(MXU_DTYPE)
        c_ref[h] = decay * ct + lax.dot_general(k, vw, (((0,), (0,)), ((), ())),
                                                preferred_element_type=_F32)
        n_ref[h] = decay * n_row + jnp.sum(wk * k.astype(_F32), axis=0, keepdims=True)
        m_ref[h] = m_new

        hs = hs * lax.rsqrt(jnp.mean(hs * hs, axis=1, keepdims=True) + NORM_EPS)
        gate = jax.nn.sigmoid(o_ref[:, h * dv:(h + 1) * dv].astype(_F32))
        y_ref[:, h * dv:(h + 1) * dv] = (hs * hn_ref[:, h * dv:(h + 1) * dv] * gate).astype(y_ref.dtype)


def _mlstm_scan(proj, g_row, g_col, gate_bias, head_norm, batch, seq, dk, dv):
    nc = seq // CHUNK
    hq = C_HEADS
    qk_w, v_w = hq * dk, hq * dv
    v_blk0 = 2 * qk_w // v_w
    return pl.pallas_call(
        functools.partial(_mlstm_kernel, n_heads=hq, dk=dk, dv=dv),
        grid=(batch, nc),
        in_specs=[pl.BlockSpec(memory_space=pltpu.SMEM),
                  pl.BlockSpec((CHUNK, qk_w), lambda b, c: (b * nc + c, 0)),
                  pl.BlockSpec((CHUNK, qk_w), lambda b, c: (b * nc + c, 1)),
                  pl.BlockSpec((CHUNK, v_w), lambda b, c: (b * nc + c, v_blk0)),
                  pl.BlockSpec((CHUNK, v_w), lambda b, c: (b * nc + c, v_blk0 + 1)),
                  pl.BlockSpec((1, 2 * hq, CHUNK), lambda b, c: (b * nc + c, 0, 0)),
                  pl.BlockSpec((1, CHUNK, 2 * hq), lambda b, c: (b * nc + c, 0, 0)),
                  pl.BlockSpec((1, v_w), lambda b, c: (0, 0))],
        out_specs=pl.BlockSpec((CHUNK, v_w), lambda b, c: (b * nc + c, 0)),
        out_shape=jax.ShapeDtypeStruct((batch * seq, v_w), MXU_DTYPE),
        scratch_shapes=[pltpu.VMEM((hq, dk, dv), _F32),
                        pltpu.VMEM((hq, 1, dk), _F32),
                        pltpu.VMEM((hq, 1, 1), _F32)],
        compiler_params=_params(("parallel", "arbitrary")),
        name="mlstm_scan",
    )(gate_bias.astype(_F32), proj, proj, proj, proj, g_row, g_col,
      head_norm.reshape(1, v_w).astype(_F32))


def _mlstm_mixer(x, mix_g, w_in, gate_bias, head_norm, w_out, batch, seq):
    d_model = x.shape[1]
    dv = d_model // C_HEADS
    dk = dv // 2
    hq = C_HEADS
    n_main = 2 * hq * dk + 2 * hq * dv
    w_gate = jnp.concatenate(
        [w_in[:, n_main:], jnp.zeros((w_in.shape[0], LANES - 2 * hq), w_in.dtype)], axis=1)
    proj = _matmul_norm(x, mix_g, w_in[:, :n_main].astype(MXU_DTYPE), MXU_DTYPE,
                        bn=3 * MM_BN // 2)
    gates = _matmul_norm(x, mix_g, w_gate.astype(MXU_DTYPE), _F32, bn=LANES)
    nc = seq // CHUNK
    g_col = gates[:, :2 * hq].reshape(batch * nc, CHUNK, 2 * hq)
    g_row = g_col.transpose(0, 2, 1)
    y = _mlstm_scan(proj, g_row, g_col, gate_bias, head_norm, batch, seq, dk, dv)
    return _matmul_residual(y, w_out.astype(MXU_DTYPE), x, 1.0)


def kernel(x, positions, ffn1_norm, ffn1_w_in, ffn1_w_out, mix_norm, ffn2_norm, ffn2_w_in, ffn2_w_out,
           a_w_in, a_q_norm, a_kv_norm, a_w_uq, a_w_ukv, a_w_out,
           b_w_in, b_idx_k_norm, b_w_out,
           c_w_in, c_gate_bias, c_head_norm, c_w_out, final_norm):
    batch, seq, d_model = x.shape
    depth = ffn1_norm.shape[0]
    x = x.reshape(batch * seq, d_model)
    ffn1_w_in, ffn1_w_out, ffn2_w_in, ffn2_w_out = (
        w.astype(MXU_DTYPE) for w in (ffn1_w_in, ffn1_w_out, ffn2_w_in, ffn2_w_out))
    for i in range(depth):
        x = _ffn(x, ffn1_norm[i], ffn1_w_in, ffn1_w_out, i)
        kind, j = i % N_MIXERS, i // N_MIXERS
        if kind == 0:
            x = _mla_mixer(x, positions, mix_norm[i], a_w_in[j], a_q_norm[j], a_kv_norm[j],
                           a_w_uq[j], a_w_ukv[j], a_w_out[j], batch, seq)
        elif kind == 1:
            x = _dsa_mixer(x, positions, mix_norm[i], b_w_in[j], b_idx_k_norm[j], b_w_out[j], batch, seq)
        else:
            x = _mlstm_mixer(x, mix_norm[i], c_w_in[j], c_gate_bias[j], c_head_norm[j], c_w_out[j],
                             batch, seq)
        x = _ffn(x, ffn2_norm[i], ffn2_w_in, ffn2_w_out, i)
    return _rmsnorm(x, final_norm, _F32).reshape(batch, seq, d_model)
```

```python
import functools

import jax
import jax.numpy as jnp
import numpy as np
from jax import lax
from jax.experimental import pallas as pl
from jax.experimental.pallas import tpu as pltpu

CHUNK = 64
ROPE_THETA = 10000.0
NORM_EPS = 1e-6
D_FF = 6144

A_HEADS = 64
A_Q_LORA = 1024
A_KV_LORA = 512
A_NOPE = 128
A_ROPE = 64
A_V = 128

B_HEADS = 32
B_KV_HEADS = 8
B_HEAD_DIM = 128
B_IDX_HEADS = 32
B_IDX_DIM = 128
B_TOPK_MAX = 256

C_HEADS = 8
N_MIXERS = 3

LANES = 128
SUBLANES = 8
ONES_ROWS = 16
LOG2E = 1.4426950408889634
MXU_DTYPE = jnp.bfloat16
VMEM_LIMIT_MB = 60

ROW_TILE = 256
MM_BM = 512
MM_BN = 1024
MM_RESIDUAL_VMEM_MB = 48
MLA_TQ = 512
DSA_TQ = 256
DSA_TK = 256

_F32 = jnp.float32
_I32 = jnp.int32
_NEG_BIG = -1e30
_INT_MIN = -(2 ** 31)
_KEY_OF_NEG_INF = -2139095041


def _params(semantics):
    return pltpu.CompilerParams(dimension_semantics=semantics,
                                vmem_limit_bytes=VMEM_LIMIT_MB << 20)


def _rot_half(x):
    return pltpu.roll(x, LANES // 2, 1)


def _rmsnorm_kernel(x_ref, g_ref, o_ref):
    x = x_ref[...].astype(_F32)
    y = x * lax.rsqrt(jnp.mean(x * x, axis=-1, keepdims=True) + NORM_EPS)
    o_ref[...] = (y * g_ref[...]).astype(o_ref.dtype)


def _rmsnorm(x, g, out_dtype, col_block=0, width=None):
    t = x.shape[0]
    width = x.shape[1] if width is None else width
    bt = min(ROW_TILE, t)
    return pl.pallas_call(
        _rmsnorm_kernel,
        grid=(t // bt,),
        in_specs=[pl.BlockSpec((bt, width), lambda i: (i, col_block)),
                  pl.BlockSpec((1, width), lambda i: (0, 0))],
        out_specs=pl.BlockSpec((bt, width), lambda i: (i, 0)),
        out_shape=jax.ShapeDtypeStruct((t, width), out_dtype),
        compiler_params=_params(("parallel",)),
        name="rmsnorm",
    )(x, g.reshape(1, width).astype(_F32))


def _rope_kernel(x_ref, ra_ref, rb_ref, o_ref, *, n_blocks):
    ra = ra_ref[...]
    rb = rb_ref[...]
    for c in range(n_blocks):
        sl = slice(c * LANES, (c + 1) * LANES)
        x = x_ref[:, sl].astype(_F32)
        o_ref[:, sl] = (x * ra + _rot_half(x) * rb).astype(o_ref.dtype)


def _rope(x, ra, rb, out_dtype, col_block=0, width=None):
    t = x.shape[0]
    width = x.shape[1] if width is None else width
    bt = min(ROW_TILE, t)
    return pl.pallas_call(
        functools.partial(_rope_kernel, n_blocks=width // LANES),
        grid=(t // bt,),
        in_specs=[pl.BlockSpec((bt, width), lambda i: (i, col_block)),
                  pl.BlockSpec((bt, LANES), lambda i: (i, 0)),
                  pl.BlockSpec((bt, LANES), lambda i: (i, 0))],
        out_specs=pl.BlockSpec((bt, width), lambda i: (i, 0)),
        out_shape=jax.ShapeDtypeStruct((t, width), out_dtype),
        compiler_params=_params(("parallel",)),
        name="rope",
    )(x, ra, rb)


def _rmsnorm_rope_kernel(x_ref, g_ref, ra_ref, rb_ref, o_ref):
    x = x_ref[...].astype(_F32)
    y = x * lax.rsqrt(jnp.mean(x * x, axis=-1, keepdims=True) + NORM_EPS) * g_ref[...]
    o_ref[...] = (y * ra_ref[...] + _rot_half(y) * rb_ref[...]).astype(o_ref.dtype)


def _rmsnorm_rope(x, g, ra, rb, out_dtype, col_block=0):
    t = x.shape[0]
    bt = min(ROW_TILE, t)
    return pl.pallas_call(
        _rmsnorm_rope_kernel,
        grid=(t // bt,),
        in_specs=[pl.BlockSpec((bt, LANES), lambda i: (i, col_block)),
                  pl.BlockSpec((1, LANES), lambda i: (0, 0)),
                  pl.BlockSpec((bt, LANES), lambda i: (i, 0)),
                  pl.BlockSpec((bt, LANES), lambda i: (i, 0))],
        out_specs=pl.BlockSpec((bt, LANES), lambda i: (i, 0)),
        out_shape=jax.ShapeDtypeStruct((t, LANES), out_dtype),
        compiler_params=_params(("parallel",)),
        name="rmsnorm_rope",
    )(x, g.reshape(1, LANES).astype(_F32), ra, rb)


def _normed(x_ref, g_ref, h_ref):
    @pl.when(pl.program_id(1) == 0)
    def _normalise():
        x = x_ref[...].astype(_F32)
        y = x * lax.rsqrt(jnp.mean(x * x, axis=-1, keepdims=True) + NORM_EPS)
        h_ref[...] = (y * g_ref[...]).astype(h_ref.dtype)

    return h_ref[...]


def _mm_norm_kernel(x_ref, g_ref, w_ref, o_ref, h_ref):
    o_ref[...] = jnp.dot(_normed(x_ref, g_ref, h_ref), w_ref[...],
                         preferred_element_type=_F32).astype(o_ref.dtype)


def _mm_norm_swiglu_kernel(x_ref, g_ref, wg_ref, wu_ref, o_ref, h_ref):
    h = _normed(x_ref, g_ref, h_ref)
    g = jnp.dot(h, wg_ref[...], preferred_element_type=_F32)
    u = jnp.dot(h, wu_ref[...], preferred_element_type=_F32)
    o_ref[...] = (g * jax.nn.sigmoid(g) * u).astype(o_ref.dtype)


def _mm_norm_t_kernel(w_ref, x_ref, g_ref, o_ref, h_ref):
    y = lax.dot_general(w_ref[...], _normed(x_ref, g_ref, h_ref), (((1,), (1,)), ((), ())),
                        preferred_element_type=_F32).astype(o_ref.dtype)
    tk = o_ref.shape[2]
    for r in range(o_ref.shape[0]):
        o_ref[r] = y[:, r * tk:(r + 1) * tk]


def _mm_residual_kernel(x_ref, w_ref, r_ref, o_ref, *, scale):
    y = jnp.dot(x_ref[...], w_ref[...], preferred_element_type=_F32)
    o_ref[...] = r_ref[...] + scale * y


def _tile(n, pref):
    return pref if n % pref == 0 else n


def _matmul_norm(x, g, w, out_dtype, bm=None, bn=None, col_block=0):
    m = x.shape[0]
    k, n = w.shape
    bm = _tile(m, bm or MM_BM)
    bn = _tile(n, bn or MM_BN)
    return pl.pallas_call(
        _mm_norm_kernel,
        grid=(m // bm, n // bn),
        in_specs=[pl.BlockSpec((bm, k), lambda i, j: (i, col_block)),
                  pl.BlockSpec((1, k), lambda i, j: (0, 0)),
                  pl.BlockSpec((k, bn), lambda i, j: (0, j))],
        out_specs=pl.BlockSpec((bm, bn), lambda i, j: (i, j)),
        out_shape=jax.ShapeDtypeStruct((m, n), out_dtype),
        scratch_shapes=[pltpu.VMEM((bm, k), MXU_DTYPE)],
        compiler_params=_params(("parallel", "arbitrary")),
        name="matmul_norm",
    )(x, g.reshape(1, k).astype(_F32), w)


def _matmul_norm_t(w_t, x, g, tk, out_dtype, bn=None, col_block=0, row_tiles=1):
    n, k = w_t.shape
    t = x.shape[0]
    bn = _tile(n, bn or MM_BN)
    rt = row_tiles if (t // tk) % row_tiles == 0 else 1
    return pl.pallas_call(
        _mm_norm_t_kernel,
        grid=(t // (rt * tk), n // bn),
        in_specs=[pl.BlockSpec((bn, k), lambda i, j: (j, 0)),
                  pl.BlockSpec((rt * tk, k), lambda i, j: (i, col_block)),
                  pl.BlockSpec((1, k), lambda i, j: (0, 0))],
        out_specs=pl.BlockSpec((rt, bn, tk), lambda i, j: (i, j, 0)),
        out_shape=jax.ShapeDtypeStruct((t // tk, n, tk), out_dtype),
        scratch_shapes=[pltpu.VMEM((rt * tk, k), MXU_DTYPE)],
        compiler_params=_params(("parallel", "arbitrary")),
        name="matmul_norm_t",
    )(w_t, x, g.reshape(1, k).astype(_F32))


def _matmul_norm_swiglu(x, g, w_in, layer=None, bm=None, bn=512):
    m, k = x.shape
    f = w_in.shape[-1] // 2
    bm = _tile(m, bm or MM_BM)
    bn = _tile(f, bn)
    nb = f // bn
    if layer is None:
        w_specs = [pl.BlockSpec((k, bn), lambda i, j: (0, j)),
                   pl.BlockSpec((k, bn), lambda i, j: (0, j + nb))]
    else:
        w_specs = [pl.BlockSpec((None, k, bn), lambda i, j: (layer, 0, j)),
                   pl.BlockSpec((None, k, bn), lambda i, j: (layer, 0, j + nb))]
    return pl.pallas_call(
        _mm_norm_swiglu_kernel,
        grid=(m // bm, nb),
        in_specs=[pl.BlockSpec((bm, k), lambda i, j: (i, 0)),
                  pl.BlockSpec((1, k), lambda i, j: (0, 0))] + w_specs,
        out_specs=pl.BlockSpec((bm, bn), lambda i, j: (i, j)),
        out_shape=jax.ShapeDtypeStruct((m, f), MXU_DTYPE),
        scratch_shapes=[pltpu.VMEM((bm, k), MXU_DTYPE)],
        compiler_params=_params(("parallel", "arbitrary")),
        name="matmul_norm_swiglu",
    )(x, g.reshape(1, k).astype(_F32), w_in, w_in)


def _matmul_residual(x, w, res, scale, layer=None, bn=512):
    m, k = x.shape
    n = w.shape[-1]
    bn = _tile(n, bn)
    bm = _tile(m, MM_BM)
    for cand in (2 * MM_BM,):
        windows = 2 * (cand * k * 2 + k * bn * 2 + 2 * cand * bn * 4)
        if m % cand == 0 and windows <= MM_RESIDUAL_VMEM_MB << 20:
            bm = cand
    if layer is None:
        w_spec = pl.BlockSpec((k, bn), lambda i, j: (0, j))
    else:
        w_spec = pl.BlockSpec((None, k, bn), lambda i, j: (layer, 0, j))
    return pl.pallas_call(
        functools.partial(_mm_residual_kernel, scale=scale),
        grid=(m // bm, n // bn),
        in_specs=[pl.BlockSpec((bm, k), lambda i, j: (i, 0)),
                  w_spec,
                  pl.BlockSpec((bm, bn), lambda i, j: (i, j))],
        out_specs=pl.BlockSpec((bm, bn), lambda i, j: (i, j)),
        out_shape=jax.ShapeDtypeStruct((m, n), _F32),
        compiler_params=_params(("parallel", "arbitrary")),
        name="matmul_residual",
    )(x, w, res)


def _ffn(x, norm_g, w_in_all, w_out_all, layer):
    act = _matmul_norm_swiglu(x, norm_g, w_in_all, layer, bn=3 * MM_BN // 4)
    return _matmul_residual(act, w_out_all, x, 0.5, layer)


def _rope_tables(positions, d, half_width):
    inv = jnp.power(ROPE_THETA, -jnp.arange(0, d, 2, dtype=_F32) / d)
    ang = positions.astype(_F32).reshape(-1, 1) * inv
    pad = jnp.zeros((ang.shape[0], half_width - d // 2), _F32)
    cos, sin = jnp.cos(ang), jnp.sin(ang)
    ra = jnp.concatenate([cos, pad, cos, pad], axis=1)
    rb = jnp.concatenate([-sin, pad, sin, pad], axis=1)
    return ra, rb


def _softmax_step(s, vt, m_ref, acc_ref):
    m_prev = m_ref[...]
    m_new = jnp.maximum(m_prev, jnp.max(s, axis=0, keepdims=True))
    alpha = jnp.exp2(m_prev - m_new)
    p = jnp.exp2(s - m_new).astype(MXU_DTYPE)
    vt1 = jnp.concatenate([vt, jnp.ones((ONES_ROWS, vt.shape[1]), vt.dtype)], axis=0)
    acc_ref[...] = alpha * acc_ref[...] + jnp.dot(vt1, p, preferred_element_type=_F32)
    m_ref[...] = m_new


def _pipelined_sweep(n, score, update, sa_ref, sb_ref):
    sa_ref[...] = score(0, True)

    def pair(t):
        sb_ref[...] = score(t + 1, False)
        update(t, sa_ref[...])
        sa_ref[...] = score(t + 2, False)
        update(t + 1, sb_ref[...])

    done = 0
    for size in (16, 8, 4, 2):
        def body(p, carry, size=size, done=done):
            for u in range(size // 2):
                pair(done + size * p + 2 * u)
            return carry

        trips = (n - 1 - done) // size
        lax.fori_loop(0, trips, body, 0)
        done = done + size * trips
    t = done

    @pl.when(t == n - 1)
    def _odd_tail():
        update(t, sa_ref[...])

    @pl.when(t < n - 1)
    def _even_tail():
        sb_ref[...] = score(t + 1, False)
        update(t, sa_ref[...])
        update(t + 1, sb_ref[...])


def _static_sweep(n, score, update, sa_ref, sb_ref, unroll=40):
    sa_ref[...] = score(0)
    bufs = (sa_ref, sb_ref)

    def run(base, count, last):
        for u in range(count):
            cur, nxt = bufs[u % 2], bufs[(u + 1) % 2]
            if not (last and u == count - 1):
                nxt[...] = score(base + u + 1)
            update(base + u, cur[...])

    n_loop = (n - 1) // unroll * unroll

    def body(p, carry):
        run(p * unroll, unroll, False)
        return carry

    lax.fori_loop(0, n_loop // unroll, body, 0)
    run(n_loop, n - n_loop, True)


def _softmax_finish(acc_ref, dv):
    acc = acc_ref[...]
    return acc[:dv, :] / acc[dv:dv + 1, :]


def _mla_attn_kernel(jt_ref, it_ref, q_ref, kn_ref, kr_ref, vt_ref, cos_ref, sin_ref, o_ref,
                     qs_ref, m_ref, acc_ref, sa_ref, sb_ref, *, nq, tq, n_off):
    half = A_ROPE // 2
    r1 = slice(A_NOPE, A_NOPE + half)
    r2 = slice(A_NOPE + LANES // 2, A_NOPE + LANES // 2 + half)

    def prepare(i, carry):
        qt = q_ref[i]
        x1 = qt[r1, :].astype(_F32)
        x2 = qt[r2, :].astype(_F32)
        cos = cos_ref[i]
        sin = sin_ref[i]
        qs_ref[i] = qt
        qs_ref[i, r1, :] = (x1 * cos - x2 * sin).astype(qs_ref.dtype)
        qs_ref[i, r2, :] = (x1 * sin + x2 * cos).astype(qs_ref.dtype)
        m_ref[i] = jnp.full((1, tq), -jnp.inf, _F32)
        acc_ref[i] = jnp.zeros(acc_ref.shape[1:], _F32)
        return carry

    lax.fori_loop(0, nq, prepare, 0)

    def scores(j, i):
        off = pl.multiple_of(j * tq, tq)
        kc = jnp.concatenate([kn_ref[pl.ds(off, tq), :], kr_ref[pl.ds(off, tq), :]], axis=1)
        return jnp.dot(kc, qs_ref[i], preferred_element_type=_F32)

    def update(j, i, s):
        _softmax_step(s, vt_ref[j], m_ref.at[i], acc_ref.at[i])

    kchunk = lax.broadcasted_iota(_I32, (tq, tq), 0) // CHUNK
    qchunk = lax.broadcasted_iota(_I32, (tq, tq), 1) // CHUNK
    _static_sweep(nq, lambda t: jnp.where(qchunk >= kchunk, scores(t, t), -jnp.inf),
                  lambda t, s: update(t, t, s), sa_ref, sb_ref)
    if n_off:
        _static_sweep(n_off, lambda t: scores(jt_ref[t], it_ref[t]),
                      lambda t, s: update(jt_ref[t], it_ref[t], s), sa_ref, sb_ref)

    def finish(i, carry):
        off = pl.multiple_of(i * tq, tq)
        o_ref[pl.ds(off, tq), :] = _softmax_finish(acc_ref.at[i], A_V).T.astype(o_ref.dtype)
        return carry

    lax.fori_loop(0, nq, finish, 0)


def _mla_attention(qt, kn, kr, vt, cos_t, sin_t, batch, seq, tq):
    nq = seq // tq
    pairs = [(j, i) for j in range(nq) for i in range(j + 1, nq)]
    jt = jnp.asarray([p[0] for p in pairs] or [0], _I32)
    it = jnp.asarray([p[1] for p in pairs] or [0], _I32)
    half = A_ROPE // 2
    smem = pl.BlockSpec(memory_space=pltpu.SMEM)
    return pl.pallas_call(
        functools.partial(_mla_attn_kernel, nq=nq, tq=tq, n_off=len(pairs)),
        grid=(batch, A_HEADS),
        in_specs=[smem, smem,
                  pl.BlockSpec((nq, 2 * LANES, tq), lambda b, h: (b, h, 0)),
                  pl.BlockSpec((seq, LANES), lambda b, h: (b, h)),
                  pl.BlockSpec((seq, LANES), lambda b, h: (b, 0)),
                  pl.BlockSpec((nq, A_V, tq), lambda b, h: (b, h, 0)),
                  pl.BlockSpec((nq, half, tq), lambda b, h: (b, 0, 0)),
                  pl.BlockSpec((nq, half, tq), lambda b, h: (b, 0, 0))],
        out_specs=pl.BlockSpec((seq, A_V), lambda b, h: (b, h)),
        out_shape=jax.ShapeDtypeStruct((batch * seq, A_HEADS * A_V), MXU_DTYPE),
        scratch_shapes=[pltpu.VMEM((nq, 2 * LANES, tq), MXU_DTYPE),
                        pltpu.VMEM((nq, 1, tq), _F32),
                        pltpu.VMEM((nq, A_V + ONES_ROWS, tq), _F32),
                        pltpu.VMEM((tq, tq), _F32),
                        pltpu.VMEM((tq, tq), _F32)],
        compiler_params=_params(("parallel", "parallel")),
        name="mla_attention",
    )(jt, it, qt, kn, kr, vt, cos_t, sin_t)


def _pad_rope_cols(w, n_heads, nope):
    k = w.shape[0]
    w = w.reshape(k, n_heads, nope + A_ROPE)
    half = A_ROPE // 2
    z = jnp.zeros((k, n_heads, LANES // 2 - half), w.dtype)
    w = jnp.concatenate([w[..., :nope], w[..., nope:nope + half], z, w[..., nope + half:], z], axis=-1)
    return w.reshape(k, n_heads * (nope + LANES))


def _mla_mixer(x, positions, mix_g, w_in, q_norm, kv_norm, w_uq, w_ukv, w_out, batch, seq):
    assert A_NOPE == LANES and A_V == LANES and A_ROPE == LANES // 2
    tq = min(MLA_TQ, seq)
    assert tq % CHUNK == 0 and seq % tq == 0
    n_lat = A_Q_LORA + A_KV_LORA
    w_in_p = jnp.concatenate([w_in[:, :n_lat], _pad_rope_cols(w_in[:, n_lat:], 1, 0)], axis=1)
    lat = _matmul_norm(x, mix_g, w_in_p.astype(MXU_DTYPE), _F32, bm=MM_BM // 2,
                       bn=w_in_p.shape[1])
    ra, rb = _rope_tables(positions, A_ROPE, LANES // 2)
    kr = _rope(lat, ra, rb, MXU_DTYPE, col_block=n_lat // LANES, width=LANES)
    qscale = (A_NOPE + A_ROPE) ** -0.5 * LOG2E
    w_qt = (_pad_rope_cols(w_uq, A_HEADS, A_NOPE) * qscale).T.astype(MXU_DTYPE)
    qt = _matmul_norm_t(w_qt, lat, q_norm, tq, MXU_DTYPE, bn=2 * MM_BN, row_tiles=2)
    w_ukv = w_ukv.reshape(A_KV_LORA, A_HEADS, A_NOPE + A_V)
    w_kn = w_ukv[..., :A_NOPE].reshape(A_KV_LORA, A_HEADS * A_NOPE)
    w_vt = w_ukv[..., A_NOPE:].reshape(A_KV_LORA, A_HEADS * A_V).T
    kv_blk = A_Q_LORA // A_KV_LORA
    kn = _matmul_norm(lat, kv_norm, w_kn.astype(MXU_DTYPE), MXU_DTYPE, bm=2 * MM_BM, bn=2 * MM_BN,
                      col_block=kv_blk)
    vt = _matmul_norm_t(w_vt.astype(MXU_DTYPE), lat, kv_norm, tq, MXU_DTYPE, bn=2 * MM_BN,
                        col_block=kv_blk, row_tiles=2)
    half = A_ROPE // 2
    cos_t = ra[:, :half].reshape(-1, tq, half).transpose(0, 2, 1)
    sin_t = rb[:, LANES // 2:LANES // 2 + half].reshape(-1, tq, half).transpose(0, 2, 1)
    o = _mla_attention(qt, kn, kr, vt, cos_t, sin_t, batch, seq, tq)
    return _matmul_residual(o, w_out.astype(MXU_DTYPE), x, 1.0)


def _dsa_kernel(qi_ref, wit_ref, ki_ref, ra_ref, rb_ref, q_ref, k_ref, vt_ref, o_ref,
                qit_ref, key_ref, qt_ref, m_ref, acc_ref, sa_ref, sb_ref,
                *, tq, tk, topk, group, wi_scale, qscale):
    i = pl.program_id(1)
    h = pl.program_id(2)
    nkb = (i + 1) * (tq // tk)
    ra = ra_ref[...]
    rb = rb_ref[...]

    @pl.when(h == 0)
    def _select():
        for hd in range(B_IDX_HEADS):
            sl = slice(hd * LANES, (hd + 1) * LANES)
            x = qi_ref[:, sl].astype(_F32)
            x = x * ra + _rot_half(x) * rb
            qit_ref[sl, :] = x.T.astype(qit_ref.dtype)
        wit = wit_ref[...] * wi_scale

        def score_tile(j, carry):
            off = pl.multiple_of(j * tk, tk)
            ki = ki_ref[pl.ds(off, tk), :]
            sc = jnp.zeros((tk, tq), _F32)
            for hd in range(B_IDX_HEADS):
                d = jnp.dot(ki, qit_ref[hd * LANES:(hd + 1) * LANES, :],
                            preferred_element_type=_F32)
                sc = sc + jnp.maximum(d, 0.0) * wit[hd:hd + 1, :]
            kchunk = (off + lax.broadcasted_iota(_I32, (tk, tq), 0)) // CHUNK
            qchunk = (i * tq + lax.broadcasted_iota(_I32, (tk, tq), 1)) // CHUNK
            sc = jnp.where(qchunk >= kchunk, sc, -jnp.inf)
            bits = pltpu.bitcast(sc, _I32)
            key_ref[j] = jnp.where(bits >= 0, bits, bits ^ 0x7FFFFFFF)
            return carry

        lax.fori_loop(0, nkb, score_tile, 0)

        def count(pred):
            def tile(j, acc):
                hit = jnp.where(pred(key_ref[j]), 1.0, 0.0)
                return acc + jnp.sum(hit.reshape(tk // SUBLANES, SUBLANES, tq), axis=0)

            def four_tiles(p, acc):
                for u in range(4):
                    acc = tile(4 * p + u, acc)
                return acc

            acc = lax.fori_loop(0, nkb // 4, four_tiles, jnp.zeros((SUBLANES, tq), _F32))
            acc = lax.fori_loop(nkb // 4 * 4, nkb, tile, acc)
            return jnp.sum(acc, axis=0, keepdims=True)

        def bisect(b, t):
            cand = t + jnp.left_shift(jnp.int32(1), 31 - b)
            return jnp.where(count(lambda kt: kt >= cand) >= topk, cand, t)

        t = lax.fori_loop(0, 32, bisect, jnp.full((1, tq), _INT_MIN, _I32))

        need = topk - count(lambda kt: kt > t)
        lower = jnp.where(lax.broadcasted_iota(_I32, (tk, tk), 0)
                          >= lax.broadcasted_iota(_I32, (tk, tk), 1), 1.0, 0.0).astype(MXU_DTYPE)

        def demote_tile(j, seen):
            kt = key_ref[j]
            eq = kt == t
            rank = seen + jnp.dot(lower, jnp.where(eq, 1.0, 0.0).astype(MXU_DTYPE),
                                  preferred_element_type=_F32)
            key_ref[j] = jnp.where(eq, jnp.where(rank > need, t - 1, kt), kt)
            return rank[tk - 1:tk, :]

        lax.fori_loop(0, nkb, demote_tile, jnp.zeros((1, tq), _F32))
        thr = jnp.maximum(t, _KEY_OF_NEG_INF + 1)

        def bias_tile(j, carry):
            bias = jnp.where(key_ref[j] >= thr, 0.0, _NEG_BIG)
            key_ref[j] = pltpu.bitcast(bias, _I32)
            return carry

        lax.fori_loop(0, nkb, bias_tile, 0)

    q = q_ref[...]
    for g in range(group):
        x = q[:, g * LANES:(g + 1) * LANES].astype(_F32)
        x = (x * ra + _rot_half(x) * rb) * qscale
        qt_ref[:, g * tq:(g + 1) * tq] = x.T.astype(qt_ref.dtype)
    m_ref[...] = jnp.full(m_ref.shape, _NEG_BIG, _F32)
    acc_ref[...] = jnp.zeros(acc_ref.shape, _F32)

    def score(j, first):
        off = pl.multiple_of(j * tk, tk)
        bias = pltpu.bitcast(key_ref[j], _F32)
        bias = jnp.concatenate([bias] * group, axis=1)
        return jnp.dot(k_ref[pl.ds(off, tk), :], qt_ref[...], preferred_element_type=_F32) + bias

    def update(j, s):
        _softmax_step(s, vt_ref[j], m_ref, acc_ref)

    _pipelined_sweep(nkb, score, update, sa_ref, sb_ref)
    out = _softmax_finish(acc_ref, B_HEAD_DIM)
    for g in range(group):
        o_ref[:, g * LANES:(g + 1) * LANES] = out[:, g * tq:(g + 1) * tq].T.astype(o_ref.dtype)


def _dsa_attention(proj, k_roped, vt, ki, wit, ra, rb, batch, seq, tq, tk):
    nq = seq // tq
    group = B_HEADS // B_KV_HEADS
    topk = min(B_TOPK_MAX, seq // 4)
    qi_w = B_IDX_HEADS * B_IDX_DIM
    q_w = group * B_HEAD_DIM
    q_blk0 = qi_w // q_w
    kern = functools.partial(
        _dsa_kernel, tq=tq, tk=tk, topk=float(topk), group=group,
        wi_scale=B_IDX_HEADS ** -0.5 * B_IDX_DIM ** -0.5, qscale=B_HEAD_DIM ** -0.5 * LOG2E)
    return pl.pallas_call(
        kern,
        grid=(batch, nq, B_KV_HEADS),
        in_specs=[pl.BlockSpec((tq, qi_w), lambda b, i, h: (b * nq + i, 0)),
                  pl.BlockSpec((B_IDX_HEADS, tq), lambda b, i, h: (0, b * nq + i)),
                  pl.BlockSpec((seq, LANES), lambda b, i, h: (b, 0)),
                  pl.BlockSpec((tq, LANES), lambda b, i, h: (b * nq + i, 0)),
                  pl.BlockSpec((tq, LANES), lambda b, i, h: (b * nq + i, 0)),
                  pl.BlockSpec((tq, q_w), lambda b, i, h: (b * nq + i, q_blk0 + h)),
                  pl.BlockSpec((seq, B_HEAD_DIM), lambda b, i, h: (b, h)),
                  pl.BlockSpec((seq // tk, B_HEAD_DIM, tk), lambda b, i, h: (b, h, 0))],
        out_specs=pl.BlockSpec((tq, q_w), lambda b, i, h: (b * nq + i, h)),
        out_shape=jax.ShapeDtypeStruct((batch * seq, B_HEADS * B_HEAD_DIM), MXU_DTYPE),
        scratch_shapes=[pltpu.VMEM((qi_w, tq), MXU_DTYPE),
                        pltpu.VMEM((seq // tk, tk, tq), _I32),
                        pltpu.VMEM((B_HEAD_DIM, group * tq), MXU_DTYPE),
                        pltpu.VMEM((1, group * tq), _F32),
                        pltpu.VMEM((B_HEAD_DIM + ONES_ROWS, group * tq), _F32),
                        pltpu.VMEM((tk, group * tq), _F32),
                        pltpu.VMEM((tk, group * tq), _F32)],
        compiler_params=_params(("parallel", "arbitrary", "arbitrary")),
        name="dsa_attention",
    )(proj, wit, ki, ra, rb, proj, k_roped, vt)


def _dsa_mixer(x, positions, mix_g, w_in, idx_k_norm, w_out, batch, seq):
    assert B_HEAD_DIM == LANES and B_IDX_DIM == LANES and B_IDX_HEADS % SUBLANES == 0
    tq = min(DSA_TQ, seq)
    tk = min(DSA_TK, tq)
    assert tq % CHUNK == 0 and tq % tk == 0 and seq % tq == 0
    nq, nk = B_HEADS * B_HEAD_DIM, B_KV_HEADS * B_HEAD_DIM
    nqi = B_IDX_HEADS * B_IDX_DIM
    o_q, o_k, o_v, o_qi, o_ki, o_wi = np.cumsum([0, nq, nk, nk, nqi, B_IDX_DIM])
    w_main = jnp.concatenate([w_in[:, o_qi:o_ki], w_in[:, o_q:o_v]], axis=1)
    w_side = jnp.concatenate(
        [w_in[:, o_ki:o_wi], w_in[:, o_wi:],
         jnp.zeros((w_in.shape[0], LANES - B_IDX_HEADS), w_in.dtype)], axis=1)
    proj = _matmul_norm(x, mix_g, w_main.astype(MXU_DTYPE), MXU_DTYPE, bn=3 * MM_BN // 2)
    side = _matmul_norm(x, mix_g, w_side.astype(MXU_DTYPE), _F32, bn=2 * LANES)
    vt = _matmul_norm_t(w_in[:, o_v:o_qi].T.astype(MXU_DTYPE), x, mix_g, tk, MXU_DTYPE)
    ra, rb = _rope_tables(positions, B_HEAD_DIM, LANES // 2)
    k_roped = _rope(proj, ra, rb, MXU_DTYPE, col_block=(nqi + nq) // nk, width=nk)
    ki = _rmsnorm_rope(side, idx_k_norm, ra, rb, MXU_DTYPE, col_block=0)
    wit = side[:, LANES:LANES + B_IDX_HEADS].T
    o = _dsa_attention(proj, k_roped, vt, ki, wit, ra, rb, batch, seq, tq, tk)
    return _matmul_residual(o, w_out.astype(MXU_DTYPE), x, 1.0)


def _log_sigmoid(x):
    return jnp.minimum(x, 0.0) - jnp.log(1.0 + jnp.exp(-jnp.abs(x)))


def _mlstm_kernel(bias_ref, q_ref, k_ref, v_ref, o_ref, grow_ref, gcol_ref, hn_ref, y_ref,
                  c_ref, n_ref, m_ref, *, n_heads, dk, dv):
    @pl.when(pl.program_id(1) == 0)
    def _init():
        c_ref[...] = jnp.zeros(c_ref.shape, _F32)
        n_ref[...] = jnp.zeros(n_ref.shape, _F32)
        m_ref[...] = jnp.zeros(m_ref.shape, _F32)

    grow = grow_ref[0]
    gcol = gcol_ref[0]
    rows = lax.broadcasted_iota(_I32, (CHUNK, CHUNK), 0)
    cols = lax.broadcasted_iota(_I32, (CHUNK, CHUNK), 1)
    tril = rows >= cols
    triu = rows <= cols

    for h in range(n_heads):
        b_i = bias_ref[h]
        b_f = bias_ref[n_heads + h]
        li_row = grow[h:h + 1, :] + b_i
        lf_row = _log_sigmoid(grow[n_heads + h:n_heads + h + 1, :] + b_f)
        li_col = gcol[:, h:h + 1] + b_i
        lf_col = _log_sigmoid(gcol[:, n_heads + h:n_heads + h + 1] + b_f)

        b_col = jnp.sum(jnp.where(tril, lf_row, 0.0), axis=1, keepdims=True)
        b_row = jnp.sum(jnp.where(triu, lf_col, 0.0), axis=0, keepdims=True)
        b_end = jnp.sum(lf_row, axis=1, keepdims=True)

        m_prev = m_ref[h]
        a = b_col + m_prev
        d = jnp.where(tril, b_col - b_row + li_row, -jnp.inf)
        m_t = jnp.maximum(a, jnp.max(d, axis=1, keepdims=True))
        inter = jnp.exp(a - m_t)

        q = q_ref[:, h * dk:(h + 1) * dk]
        k = k_ref[:, h * dk:(h + 1) * dk]
        v = v_ref[:, h * dv:(h + 1) * dv]
        qf = q.astype(_F32) * dk ** -0.5
        qs = qf.astype(MXU_DTYPE)
        qk = lax.dot_general(qs, k, (((1,), (1,)), ((), ())),
                             preferred_element_type=_F32) * jnp.exp(d - m_t)
        ct = c_ref[h]
        num = inter * jnp.dot(qs, ct.astype(MXU_DTYPE), preferred_element_type=_F32) + jnp.dot(
            qk.astype(MXU_DTYPE), v, preferred_element_type=_F32)
        n_row = n_ref[h]
        den = inter * jnp.sum(qf * n_row, axis=1, keepdims=True) + jnp.sum(qk, axis=1, keepdims=True)
        hs = num / jnp.maximum(jnp.abs(den), jnp.exp(-m_t))

        g_col = b_end - b_col + li_col
        m_new = jnp.maximum(b_end + m_prev, jnp.max(g_col, axis=0, keepdims=True))
        decay = jnp.exp(b_end + m_prev - m_new)
        wk = jnp.exp(g_col - m_new)
        vw = (v.astype(_F32) * wk).astype(MXU_DTYPE)
        c_ref[h] = decay * ct + lax.dot_general(k, vw, (((0,), (0,)), ((), ())),
                                                preferred_element_type=_F32)
        n_ref[h] = decay * n_row + jnp.sum(wk * k.astype(_F32), axis=0, keepdims=True)
        m_ref[h] = m_new

        hs = hs * lax.rsqrt(jnp.mean(hs * hs, axis=1, keepdims=True) + NORM_EPS)
        gate = jax.nn.sigmoid(o_ref[:, h * dv:(h + 1) * dv].astype(_F32))
        y_ref[:, h * dv:(h + 1) * dv] = (hs * hn_ref[:, h * dv:(h + 1) * dv] * gate).astype(y_ref.dtype)


def _mlstm_scan(proj, g_row, g_col, gate_bias, head_norm, batch, seq, dk, dv):
    nc = seq // CHUNK
    hq = C_HEADS
    qk_w, v_w = hq * dk, hq * dv
    v_blk0 = 2 * qk_w // v_w
    return pl.pallas_call(
        functools.partial(_mlstm_kernel, n_heads=hq, dk=dk, dv=dv),
        grid=(batch, nc),
        in_specs=[pl.BlockSpec(memory_space=pltpu.SMEM),
                  pl.BlockSpec((CHUNK, qk_w), lambda b, c: (b * nc + c, 0)),
                  pl.BlockSpec((CHUNK, qk_w), lambda b, c: (b * nc + c, 1)),
                  pl.BlockSpec((CHUNK, v_w), lambda b, c: (b * nc + c, v_blk0)),
                  pl.BlockSpec((CHUNK, v_w), lambda b, c: (b * nc + c, v_blk0 + 1)),
                  pl.BlockSpec((1, 2 * hq, CHUNK), lambda b, c: (b * nc + c, 0, 0)),
                  pl.BlockSpec((1, CHUNK, 2 * hq), lambda b, c: (b * nc + c, 0, 0)),
                  pl.BlockSpec((1, v_w), lambda b, c: (0, 0))],
        out_specs=pl.BlockSpec((CHUNK, v_w), lambda b, c: (b * nc + c, 0)),
        out_shape=jax.ShapeDtypeStruct((batch * seq, v_w), MXU_DTYPE),
        scratch_shapes=[pltpu.VMEM((hq, dk, dv), _F32),
                        pltpu.VMEM((hq, 1, dk), _F32),
                        pltpu.VMEM((hq, 1, 1), _F32)],
        compiler_params=_params(("parallel", "arbitrary")),
        name="mlstm_scan",
    )(gate_bias.astype(_F32), proj, proj, proj, proj, g_row, g_col,
      head_norm.reshape(1, v_w).astype(_F32))


def _mlstm_mixer(x, mix_g, w_in, gate_bias, head_norm, w_out, batch, seq):
    d_model = x.shape[1]
    dv = d_model // C_HEADS
    dk = dv // 2
    hq = C_HEADS
    n_main = 2 * hq * dk + 2 * hq * dv
    w_gate = jnp.concatenate(
        [w_in[:, n_main:], jnp.zeros((w_in.shape[0], LANES - 2 * hq), w_in.dtype)], axis=1)
    proj = _matmul_norm(x, mix_g, w_in[:, :n_main].astype(MXU_DTYPE), MXU_DTYPE,
                        bn=3 * MM_BN // 2)
    gates = _matmul_norm(x, mix_g, w_gate.astype(MXU_DTYPE), _F32, bn=LANES)
    nc = seq // CHUNK
    g_col = gates[:, :2 * hq].reshape(batch * nc, CHUNK, 2 * hq)
    g_row = g_col.transpose(0, 2, 1)
    y = _mlstm_scan(proj, g_row, g_col, gate_bias, head_norm, batch, seq, dk, dv)
    return _matmul_residual(y, w_out.astype(MXU_DTYPE), x, 1.0)


def kernel(x, positions, ffn1_norm, ffn1_w_in, ffn1_w_out, mix_norm, ffn2_norm, ffn2_w_in, ffn2_w_out,
           a_w_in, a_q_norm, a_kv_norm, a_w_uq, a_w_ukv, a_w_out,
           b_w_in, b_idx_k_norm, b_w_out,
           c_w_in, c_gate_bias, c_head_norm, c_w_out, final_norm):
    batch, seq, d_model = x.shape
    depth = ffn1_norm.shape[0]
    x = x.reshape(batch * seq, d_model)
    ffn1_w_in, ffn1_w_out, ffn2_w_in, ffn2_w_out = (
        w.astype(MXU_DTYPE) for w in (ffn1_w_in, ffn1_w_out, ffn2_w_in, ffn2_w_out))
    for i in range(depth):
        x = _ffn(x, ffn1_norm[i], ffn1_w_in, ffn1_w_out, i)
        kind, j = i % N_MIXERS, i // N_MIXERS
        if kind == 0:
            x = _mla_mixer(x, positions, mix_norm[i], a_w_in[j], a_q_norm[j], a_kv_norm[j],
                           a_w_uq[j], a_w_ukv[j], a_w_out[j], batch, seq)
        elif kind == 1:
            x = _dsa_mixer(x, positions, mix_norm[i], b_w_in[j], b_idx_k_norm[j], b_w_out[j], batch, seq)
        else:
            x = _mlstm_mixer(x, mix_norm[i], c_w_in[j], c_gate_bias[j], c_head_norm[j], c_w_out[j],
                             batch, seq)
        x = _ffn(x, ffn2_norm[i], ffn2_w_in, ffn2_w_out, i)
    return _rmsnorm(x, final_norm, _F32).reshape(batch, seq, d_model)
```
